```python
import math
import jax
import jax.numpy as jnp
from jax import lax
import numpy as np

D_MODEL = 1024
BATCH = 32
SEQ = 2048
DEPTH = 2

HEAD_DIM = 64
ROT_DIM = HEAD_DIM // 4
ROPE_THETA = 500000.0
QBLOCK = 128
EPS = 1e-6

MLA_HEADS = 8
MLA_Q_RANK = 256
MLA_KV_RANK = 128
MLA_NOPE = 64
MLA_ROPE = 32
MLA_V = 64

DIL_GROUPS = ((128, 1), (512, 4), (2048, 16))
DIL_HEADS = 4

DSA_HEADS = 8
IDX_HEADS = 8
IDX_DIM = 64
TOPK_MAX = 256

IN_SIZES = (MLA_Q_RANK, MLA_KV_RANK, MLA_ROPE,
            3 * DIL_HEADS * HEAD_DIM, 3 * DIL_HEADS * HEAD_DIM, 3 * DIL_HEADS * HEAD_DIM,
            DSA_HEADS * HEAD_DIM, HEAD_DIM, HEAD_DIM, IDX_HEADS * IDX_DIM, IDX_DIM, IDX_HEADS)
C_IN = sum(IN_SIZES)
N_BRANCH = 3

N_GROUPS = 4
EXPERTS_PER_GROUP = 8
N_EXPERTS = N_GROUPS * EXPERTS_PER_GROUP
TOP_K_SUB = 2
D_EXPERT = 256

kernel_name = "hybrid_mla_dilated_dsa_hmoe_deepnorm"

F32 = jnp.float32


def layer_norm(x, g, b):
    xf = x.astype(F32)
    mu = jnp.mean(xf, axis=-1, keepdims=True)
    xc = xf - mu
    var = jnp.mean(xc * xc, axis=-1, keepdims=True)
    return (xc * lax.rsqrt(var + EPS) * g + b).astype(x.dtype)


def rms_norm(x, g):
    xf = x.astype(F32)
    return (xf * lax.rsqrt(jnp.mean(xf * xf, axis=-1, keepdims=True) + EPS) * g).astype(x.dtype)


def rope_tables(seq, rot_dim):
    inv = ROPE_THETA ** (-jnp.arange(0, rot_dim, 2, dtype=F32) / rot_dim)
    ang = jnp.arange(seq, dtype=F32)[:, None] * inv[None, :]
    return jnp.cos(ang), jnp.sin(ang)


def apply_rope(x, cos, sin):
    half = cos.shape[-1]
    rot = 2 * half
    shape = (1, x.shape[1]) + (1,) * (x.ndim - 3) + (half,)
    c = cos.reshape(shape)
    s = sin.reshape(shape)
    x1 = x[..., :half].astype(F32)
    x2 = x[..., half:rot].astype(F32)
    return jnp.concatenate([(x1 * c - x2 * s).astype(x.dtype),
                            (x2 * c + x1 * s).astype(x.dtype),
                            x[..., rot:]], axis=-1)


def causal_mask(lo, hi):
    return (lo + jnp.arange(hi - lo))[:, None] >= jnp.arange(hi)[None, :]


def mla_branch(c_q, c_kv, k_rope, q_norm_g, w_uq, kv_norm_g, w_ukv, cos_m, sin_m):
    B, S, _ = c_q.shape
    q = (rms_norm(c_q, q_norm_g) @ w_uq).reshape(B, S, MLA_HEADS, MLA_NOPE + MLA_ROPE)
    q_nope = q[..., :MLA_NOPE]
    q_rope = apply_rope(q[..., MLA_NOPE:], cos_m, sin_m)
    kv = (rms_norm(c_kv, kv_norm_g) @ w_ukv).reshape(B, S, MLA_HEADS, MLA_NOPE + MLA_V)
    k_nope = kv[..., :MLA_NOPE]
    v = kv[..., MLA_NOPE:]
    k_rope = apply_rope(k_rope, cos_m, sin_m)
    scale = (MLA_NOPE + MLA_ROPE) ** -0.5
    outs = []
    for i in range(S // QBLOCK):
        lo, hi = i * QBLOCK, (i + 1) * QBLOCK
        s = (jnp.einsum('bqhd,bkhd->bhqk', q_nope[:, lo:hi], k_nope[:, :hi])
             + jnp.einsum('bqhr,bkr->bhqk', q_rope[:, lo:hi], k_rope[:, :hi])).astype(F32) * scale
        s = jnp.where(causal_mask(lo, hi), s, -jnp.inf)
        p = jax.nn.softmax(s, axis=-1).astype(v.dtype)
        outs.append(jnp.einsum('bhqk,bkhd->bqhd', p, v[:, :hi]))
    return jnp.concatenate(outs, axis=1).reshape(B, S, MLA_HEADS * MLA_V)


def dilated_group(q, k, v, window, dilation):
    B, S, H, hd = q.shape
    band = window // dilation
    M = S // dilation
    nblk = -(-M // band)
    Mp = nblk * band

    def lattice(t):
        t = t.reshape(B, M, dilation, H, hd).transpose(0, 2, 1, 3, 4)
        t = jnp.pad(t, ((0, 0), (0, 0), (0, Mp - M), (0, 0), (0, 0)))
        return t.reshape(B, dilation, nblk, band, H, hd)

    def with_prev(t):
        prev = jnp.pad(t[:, :, :-1], ((0, 0), (0, 0), (1, 0), (0, 0), (0, 0), (0, 0)))
        return jnp.concatenate([prev, t], axis=3)

    ql = lattice(q)
    kb = with_prev(lattice(k))
    vb = with_prev(lattice(v))
    s = jnp.einsum('brnqhd,brnkhd->brnhqk', ql, kb).astype(F32) * (hd ** -0.5)
    dist = (band + jnp.arange(band))[:, None] - jnp.arange(2 * band)[None, :]
    in_band = (dist >= 0) & (dist <= band)
    has_prev = (jnp.arange(nblk) > 0)[:, None, None] | (jnp.arange(2 * band) >= band)[None, None, :]
    mask = in_band[None] & has_prev
    s = jnp.where(mask[None, None, :, None], s, -jnp.inf)
    lse = jax.nn.logsumexp(s, axis=-1)
    p = jnp.exp(s - lse[..., None]).astype(v.dtype)
    o = jnp.einsum('brnhqk,brnkhd->brnqhd', p, vb)
    o = o.reshape(B, dilation, Mp, H, hd)[:, :, :M].transpose(0, 2, 1, 3, 4).reshape(B, S, H, hd)
    lse = lse.transpose(0, 1, 2, 4, 3).reshape(B, dilation, Mp, H)[:, :, :M]
    lse = lse.transpose(0, 2, 1, 3).reshape(B, S, H)
    return o, lse


def dilated_branch(group_cols, cos_p, sin_p):
    outs, lses = [], []
    for (window, dilation), cols in zip(DIL_GROUPS, group_cols):
        B, S, _ = cols.shape
        qkv = cols.reshape(B, S, 3, DIL_HEADS, HEAD_DIM)
        q = apply_rope(qkv[:, :, 0], cos_p, sin_p)
        k = apply_rope(qkv[:, :, 1], cos_p, sin_p)
        o, lse = dilated_group(q, k, qkv[:, :, 2], window, dilation)
        outs.append(o)
        lses.append(lse)
    w = jax.nn.softmax(jnp.stack(lses, axis=0), axis=0).astype(outs[0].dtype)
    o = jnp.einsum('gbsh,gbshd->bshd', w, jnp.stack(outs, axis=0))
    return o.reshape(B, S, DIL_HEADS * HEAD_DIM)


def dsa_branch(q, k, v, iq, ik, iw, cos_p, sin_p):
    B, S, _ = q.shape
    q = apply_rope(q.reshape(B, S, DSA_HEADS, HEAD_DIM), cos_p, sin_p)
    k = apply_rope(k, cos_p, sin_p)
    iq = apply_rope(iq.reshape(B, S, IDX_HEADS, IDX_DIM), cos_p, sin_p)
    ik = apply_rope(ik, cos_p, sin_p)
    iw = iw.astype(F32) * (IDX_HEADS ** -0.5 * IDX_DIM ** -0.5)
    top = min(TOPK_MAX, S // 4)
    gather = jax.vmap(lambda t, i: t[i])
    outs = []
    for i in range(S // QBLOCK):
        lo, hi = i * QBLOCK, (i + 1) * QBLOCK
        kk = min(top, hi)
        rel = jax.nn.relu(jnp.einsum('bqhd,bsd->bqhs', iq[:, lo:hi], ik[:, :hi]).astype(F32))
        score = jnp.einsum('bqh,bqhs->bqs', iw[:, lo:hi], rel)
        score = jnp.where(causal_mask(lo, hi)[None], score, -jnp.inf)
        _, idx = lax.top_k(score, kk)
        kg = gather(k[:, :hi], idx)
        vg = gather(v[:, :hi], idx)
        s = jnp.einsum('bqhd,bqkd->bqhk', q[:, lo:hi], kg).astype(F32) * (HEAD_DIM ** -0.5)
        ok = idx <= (lo + jnp.arange(QBLOCK))[None, :, None]
        s = jnp.where(ok[:, :, None, :], s, -jnp.inf)
        p = jax.nn.softmax(s, axis=-1).astype(v.dtype)
        outs.append(jnp.einsum('bqhk,bqkd->bqhd', p, vg))
    return jnp.concatenate(outs, axis=1).reshape(B, S, DSA_HEADS * HEAD_DIM)


def token_mixer(x, w_in, q_norm_g, w_uq, kv_norm_g, w_ukv, w_gate, b_gate, w_a, w_b, w_c, w_o,
                cos_m, sin_m, cos_p, sin_p):
    B, S, D = x.shape
    split_points = tuple(int(v) for v in np.cumsum(IN_SIZES)[:-1])
    parts = jnp.split(x @ w_in, split_points, axis=-1)
    c_q, c_kv, k_r = parts[0], parts[1], parts[2]
    dil_cols = parts[3:6]
    cq, ck, cv, iq, ik, iw = parts[6:12]
    o_a = mla_branch(c_q, c_kv, k_r, q_norm_g, w_uq, kv_norm_g, w_ukv, cos_m, sin_m)
    o_b = dilated_branch(dil_cols, cos_p, sin_p)
    o_c = dsa_branch(cq, ck, cv, iq, ik, iw, cos_p, sin_p)
    g = jax.nn.sigmoid((x @ w_gate + b_gate).astype(F32)).astype(x.dtype).reshape(B, S, N_BRANCH, D)
    merged = g[:, :, 0] * (o_a @ w_a) + g[:, :, 1] * (o_b @ w_b) + g[:, :, 2] * (o_c @ w_c)
    return merged @ w_o


def hier_moe(x, w_group, b_group, w_sub, b_sub, w1, w3, w2):
    B, S, D = x.shape
    T = B * S
    t = x.reshape(T, D)
    g_prob = jax.nn.softmax((t @ w_group + b_group).astype(F32), axis=-1)
    g_p, g_idx = lax.top_k(g_prob, 1)
    sub = (t @ w_sub + b_sub).astype(F32).reshape(T, N_GROUPS, EXPERTS_PER_GROUP)
    sub = jnp.take_along_axis(sub, g_idx[:, :, None], axis=1)[:, 0]
    e_val, e_idx = lax.top_k(sub, TOP_K_SUB)
    e_w = jax.nn.softmax(e_val, axis=-1) * g_p
    expert = g_idx * EXPERTS_PER_GROUP + e_idx
    combine = jnp.einsum('tk,tke->te', e_w, jax.nn.one_hot(expert, N_EXPERTS, dtype=F32))
    combine = combine.astype(x.dtype)
    y = jnp.zeros_like(t)
    for e in range(N_EXPERTS):
        h = jax.nn.silu(t @ w1[e]) * (t @ w3[e])
        y = y + combine[:, e:e + 1] * (h @ w2[e])
    return y.reshape(B, S, D)


def setup_inputs(seed: int = 0) -> dict:
    key = jax.random.key(seed)
    ks = jax.random.split(key, 24)
    L, D = DEPTH, D_MODEL
    beta = (8 * DEPTH) ** -0.25

    def nrm(k, shape, scale):
        return jax.random.normal(k, shape, F32) * scale

    return {
        "x": nrm(ks[0], (BATCH, SEQ, D), 1.0),
        "w_in": nrm(ks[1], (L, D, C_IN), D ** -0.5),
        "q_norm_g": 1.0 + nrm(ks[2], (L, MLA_Q_RANK), 0.01),
        "w_uq": nrm(ks[3], (L, MLA_Q_RANK, MLA_HEADS * (MLA_NOPE + MLA_ROPE)), MLA_Q_RANK ** -0.5),
        "kv_norm_g": 1.0 + nrm(ks[4], (L, MLA_KV_RANK), 0.01),
        "w_ukv": nrm(ks[5], (L, MLA_KV_RANK, MLA_HEADS * (MLA_NOPE + MLA_V)), MLA_KV_RANK ** -0.5),
        "w_gate": nrm(ks[6], (L, D, N_BRANCH * D), D ** -0.5),
        "b_gate": nrm(ks[7], (L, N_BRANCH * D), 0.01),
        "w_a": nrm(ks[8], (L, MLA_HEADS * MLA_V, D), beta * (MLA_HEADS * MLA_V) ** -0.5),
        "w_b": nrm(ks[9], (L, DIL_HEADS * HEAD_DIM, D), beta * (DIL_HEADS * HEAD_DIM) ** -0.5),
        "w_c": nrm(ks[10], (L, DSA_HEADS * HEAD_DIM, D), beta * (DSA_HEADS * HEAD_DIM) ** -0.5),
        "w_o": nrm(ks[11], (L, D, D), beta * D ** -0.5),
        "ln1_g": 1.0 + nrm(ks[12], (L, D), 0.01),
        "ln1_b": nrm(ks[13], (L, D), 0.01),
        "w_group": nrm(ks[14], (L, D, N_GROUPS), D ** -0.5),
        "b_group": nrm(ks[15], (L, N_GROUPS), 0.01),
        "w_sub": nrm(ks[16], (L, D, N_EXPERTS), D ** -0.5),
        "b_sub": nrm(ks[17], (L, N_EXPERTS), 0.01),
        "w1": nrm(ks[18], (L, N_EXPERTS, D, D_EXPERT), D ** -0.5),
        "w3": nrm(ks[19], (L, N_EXPERTS, D, D_EXPERT), D ** -0.5),
        "w2": nrm(ks[20], (L, N_EXPERTS, D_EXPERT, D), beta * D_EXPERT ** -0.5),
        "ln2_g": 1.0 + nrm(ks[21], (L, D), 0.01),
        "ln2_b": nrm(ks[22], (L, D), 0.01),
    }


def reference(x, w_in, q_norm_g, w_uq, kv_norm_g, w_ukv, w_gate, b_gate, w_a, w_b, w_c, w_o,
              ln1_g, ln1_b, w_group, b_group, w_sub, b_sub, w1, w3, w2, ln2_g, ln2_b):
    S = x.shape[1]
    alpha = (2 * DEPTH) ** 0.25
    cos_m, sin_m = rope_tables(S, MLA_ROPE)
    cos_p, sin_p = rope_tables(S, ROT_DIM)
    for l in range(DEPTH):
        mix = token_mixer(x, w_in[l], q_norm_g[l], w_uq[l], kv_norm_g[l], w_ukv[l], w_gate[l],
                          b_gate[l], w_a[l], w_b[l], w_c[l], w_o[l], cos_m, sin_m, cos_p, sin_p)
        x = layer_norm(alpha * x + mix, ln1_g[l], ln1_b[l])
        ffn = hier_moe(x, w_group[l], b_group[l], w_sub[l], b_sub[l], w1[l], w3[l], w2[l])
        x = layer_norm(alpha * x + ffn, ln2_g[l], ln2_b[l])
    return x
```

```python
import functools

import jax
import jax.numpy as jnp
from jax import lax
from jax.experimental import pallas as pl
from jax.experimental.pallas import tpu as pltpu

F32 = jnp.float32
BF16 = jnp.bfloat16
I32 = jnp.int32

HEAD_DIM = 64
ROT_DIM = HEAD_DIM // 4
ROPE_THETA = 500000.0
EPS = 1e-6
MLA_HEADS = 8
MLA_Q_RANK = 256
MLA_KV_RANK = 128
MLA_NOPE = 64
MLA_ROPE = 32
MLA_V = 64
DIL_GROUPS = ((128, 1), (512, 4), (2048, 16))
DIL_HEADS = 4
DSA_HEADS = 8
IDX_HEADS = 8
IDX_DIM = 64
TOPK_MAX = 256
N_GROUPS = 4
EXPERTS_PER_GROUP = 8
N_EXPERTS = N_GROUPS * EXPERTS_PER_GROUP
D_EXPERT = 256
N_BRANCH = 3

LANES = 128
VMEM_LIMIT = 56 * 1024 * 1024
NEG_INF = float("-inf")
INT_MIN = -2 ** 31
M_FLOOR = -1e30


def _params(*sem):
    return pltpu.CompilerParams(dimension_semantics=sem, vmem_limit_bytes=VMEM_LIMIT)


def _dot(a, b):
    return jnp.dot(a, b, preferred_element_type=F32)


def _dot_t(a, b):
    return lax.dot_general(a, b, (((1,), (1,)), ((), ())), preferred_element_type=F32)


def _rope_chunk(v, tab_ref, kind, shift):
    c = tab_ref[kind, 0]
    s1 = tab_ref[kind, 1]
    s2 = tab_ref[kind, 2]
    return v * c + pltpu.roll(v, shift, 1) * s1 + pltpu.roll(v, LANES - shift, 1) * s2


def _layer_norm(z, g, b):
    mu = jnp.mean(z, axis=-1, keepdims=True)
    zc = z - mu
    var = jnp.mean(zc * zc, axis=-1, keepdims=True)
    return zc * lax.rsqrt(var + EPS) * g + b


def _proj_kernel(*refs, segs, kinds, shift, has_tab):
    x_ref, w_ref = refs[0], refs[1]
    tab_ref = refs[2] if has_tab else None
    outs = refs[3:] if has_tab else refs[2:]
    acc = _dot(x_ref[...].astype(BF16), w_ref[...])
    for (c0, width), o_ref in zip(segs, outs):
        for c in range(width // LANES):
            cc = c0 // LANES + c
            v = acc[:, cc * LANES:(cc + 1) * LANES]
            if kinds[cc] >= 0:
                v = _rope_chunk(v, tab_ref, kinds[cc], shift)
            o_ref[:, c * LANES:(c + 1) * LANES] = v.astype(o_ref.dtype)


def _proj(x, w, seq, *, tn, segs, dtypes, kinds, tab=None, shift=ROT_DIM // 2, tm=512):
    T, K = x.shape
    N = w.shape[1]
    nj = N // tn
    spt = seq // tm
    in_specs = [pl.BlockSpec((tm, K), lambda i, j: (i, 0)),
                pl.BlockSpec((K, tn), lambda i, j: (0, j))]
    args = [x, w]
    if tab is not None:
        in_specs.append(pl.BlockSpec((tab.shape[0], 3, tm, LANES), lambda i, j: (0, 0, i % spt, 0)))
        args.append(tab)
    out_specs = [pl.BlockSpec((tm, wd), lambda i, j: (i, j)) for (_, wd) in segs]
    out_shape = [jax.ShapeDtypeStruct((T, wd * nj), dt) for (_, wd), dt in zip(segs, dtypes)]
    kern = functools.partial(_proj_kernel, segs=segs, kinds=kinds, shift=shift, has_tab=tab is not None)
    return pl.pallas_call(kern, grid=(T // tm, nj), in_specs=in_specs, out_specs=out_specs,
                          out_shape=out_shape, compiler_params=_params("parallel", "arbitrary"),
                          name="in_proj")(*args)


def _mla_up_kernel(g_ref, qg_ref, kvg_ref, wq_ref, wk_ref, wv_ref, tab_ref, q_out, k_out, v_out):
    g = g_ref[...]
    cq = g[:, :MLA_Q_RANK]
    ckv = g[:, MLA_Q_RANK:MLA_Q_RANK + MLA_KV_RANK]
    kr = g[:, MLA_Q_RANK + MLA_KV_RANK:]
    cqn = (cq * lax.rsqrt(jnp.mean(cq * cq, axis=-1, keepdims=True) + EPS) * qg_ref[...]).astype(BF16)
    ckvn = (ckv * lax.rsqrt(jnp.mean(ckv * ckv, axis=-1, keepdims=True) + EPS) * kvg_ref[...]).astype(BF16)
    q = _dot(cqn, wq_ref[...])
    k = _dot(ckvn, wk_ref[...])
    v_out[...] = _dot(ckvn, wv_ref[...]).astype(v_out.dtype)
    for h in range(MLA_HEADS):
        sl = slice(h * LANES, (h + 1) * LANES)
        q_out[:, sl] = _rope_chunk(q[:, sl], tab_ref, 0, MLA_ROPE // 2).astype(q_out.dtype)
        k_out[:, sl] = _rope_chunk(k[:, sl] + kr, tab_ref, 0, MLA_ROPE // 2).astype(k_out.dtype)


def _mla_up(grp, qg, kvg, wq, wk, wv, tab, seq, tm=512):
    T = grp.shape[0]
    spt = seq // tm
    const = lambda i: (0, 0)
    return pl.pallas_call(
        _mla_up_kernel, grid=(T // tm,),
        in_specs=[pl.BlockSpec((tm, grp.shape[1]), lambda i: (i, 0)),
                  pl.BlockSpec(qg.shape, const), pl.BlockSpec(kvg.shape, const),
                  pl.BlockSpec(wq.shape, const), pl.BlockSpec(wk.shape, const), pl.BlockSpec(wv.shape, const),
                  pl.BlockSpec((1, 3, tm, LANES), lambda i: (0, 0, i % spt, 0))],
        out_specs=[pl.BlockSpec((tm, MLA_HEADS * LANES), lambda i: (i, 0)),
                   pl.BlockSpec((tm, MLA_HEADS * LANES), lambda i: (i, 0)),
                   pl.BlockSpec((tm, MLA_HEADS * MLA_V), lambda i: (i, 0))],
        out_shape=[jax.ShapeDtypeStruct((T, MLA_HEADS * LANES), BF16),
                   jax.ShapeDtypeStruct((T, MLA_HEADS * LANES), BF16),
                   jax.ShapeDtypeStruct((T, MLA_HEADS * MLA_V), BF16)],
        compiler_params=_params("parallel"), name="mla_up")(grp, qg, kvg, wq, wk, wv, tab)


def _mla_attn_kernel(q_ref, k_ref, v_ref, o_ref, *, tq, scale):
    qi = pl.program_id(2)
    row = lax.broadcasted_iota(I32, (tq, tq), 0)
    col = lax.broadcasted_iota(I32, (tq, tq), 1)
    diag_bias = jnp.where(col <= row, 0.0, NEG_INF).astype(F32)
    outs = []
    for h in range(2):
        q = q_ref[:, h * LANES:(h + 1) * LANES]

        def step(j, carry, masked, q=q, h=h):
            m, l, acc = carry
            ks = pl.multiple_of(j * tq, tq)
            k = k_ref[pl.ds(ks, tq), h * LANES:(h + 1) * LANES]
            v = v_ref[pl.ds(ks, tq), h * MLA_V:(h + 1) * MLA_V]
            s = _dot_t(q, k) * scale
            if masked:
                s = s + diag_bias
            m_new = jnp.maximum(m, jnp.max(s, axis=1, keepdims=True))
            alpha = jnp.exp(m - m_new)
            p = jnp.exp(s - m_new)
            l = alpha * l + jnp.sum(p, axis=1, keepdims=True)
            acc = alpha * acc + _dot(p.astype(BF16), v)
            return m_new, l, acc

        init = (jnp.full((tq, 1), NEG_INF, F32), jnp.zeros((tq, 1), F32), jnp.zeros((tq, MLA_V), F32))
        carry = lax.fori_loop(0, qi, functools.partial(step, masked=False), init)
        _, l, acc = step(qi, carry, True)
        outs.append(acc / l)
    o_ref[...] = jnp.concatenate(outs, axis=1).astype(o_ref.dtype)


def _mla_attn(q, k, v, batch, seq, tq=256):
    T = q.shape[0]
    nq = seq // tq
    scale = (MLA_NOPE + MLA_ROPE) ** -0.5
    kern = functools.partial(_mla_attn_kernel, tq=tq, scale=scale)
    return pl.pallas_call(
        kern, grid=(batch, MLA_HEADS // 2, nq),
        in_specs=[pl.BlockSpec((tq, 2 * LANES), lambda b, hp, i: (b * nq + i, hp)),
                  pl.BlockSpec((seq, 2 * LANES), lambda b, hp, i: (b, hp)),
                  pl.BlockSpec((seq, 2 * MLA_V), lambda b, hp, i: (b, hp))],
        out_specs=pl.BlockSpec((tq, 2 * MLA_V), lambda b, hp, i: (b * nq + i, hp)),
        out_shape=jax.ShapeDtypeStruct((T, MLA_HEADS * MLA_V), BF16),
        compiler_params=_params("parallel", "parallel", "arbitrary"), name="mla_attn")(q, k, v)


def _dil_group(x_refs, o_ref, m_s, l_s, a_s, *, gi, ng, window, d, seq):
    band = window // d
    M = seq // d
    assert M % band == 0 and band % 8 == 0
    nblk = M // band
    NCH = DIL_HEADS * HEAD_DIM // LANES
    q_refs, k_refs, v_refs = x_refs[:NCH], x_refs[NCH:2 * NCH], x_refs[2 * NCH:]
    ri = lax.broadcasted_iota(I32, (band, band), 0)
    ci = lax.broadcasted_iota(I32, (band, band), 1)
    bias0 = jnp.where(ci <= ri, 0.0, NEG_INF).astype(F32)
    ri2 = lax.broadcasted_iota(I32, (band, 2 * band), 0)
    ci2 = lax.broadcasted_iota(I32, (band, 2 * band), 1)
    bias1 = jnp.where((ci2 >= ri2), jnp.where(ci2 <= ri2 + band, 0.0, NEG_INF), NEG_INF).astype(F32)

    def rows(start, size):
        if d == 1:
            return pl.ds(start, size)
        return pl.ds(start, size, stride=d)

    def tile(r, n, first):
        q_rows = rows(n * (band * d) + r, band)
        k_rows = q_rows if first else rows((n - 1) * (band * d) + r, 2 * band)
        bias = bias0 if first else bias1
        for c in range(NCH):
            q = q_refs[c][q_rows, :].astype(BF16)
            k = k_refs[c][k_rows, :].astype(BF16)
            v = v_refs[c][k_rows, :].astype(BF16)
            accs, ms, ls = [], [], []
            for h in range(LANES // HEAD_DIM):
                sl = slice(h * HEAD_DIM, (h + 1) * HEAD_DIM)
                s = _dot_t(q[:, sl], k[:, sl]) + bias
                m = jnp.max(s, axis=1, keepdims=True)
                p = jnp.exp(s - m)
                l = jnp.sum(p, axis=1, keepdims=True)
                accs.append(_dot(p.astype(BF16), v[:, sl]))
                ms.append(jnp.broadcast_to(m, (band, HEAD_DIM)))
                ls.append(jnp.broadcast_to(l, (band, HEAD_DIM)))
            A = jnp.concatenate(accs, axis=1)
            Mb = jnp.concatenate(ms, axis=1)
            Lb = jnp.concatenate(ls, axis=1)
            if gi > 0:
                m_old = m_s[c, q_rows, :]
                m_new = jnp.maximum(m_old, Mb)
                w_old = jnp.exp(m_old - m_new)
                w_cur = jnp.exp(Mb - m_new)
                A = a_s[c, q_rows, :] * w_old + A * w_cur
                Lb = l_s[c, q_rows, :] * w_old + Lb * w_cur
                Mb = m_new
            if gi == ng - 1:
                a_s[c, q_rows, :] = A / Lb
            else:
                m_s[c, q_rows, :] = Mb
                l_s[c, q_rows, :] = Lb
                a_s[c, q_rows, :] = A

    def r_body(r, _):
        tile(r, 0, True)
        if nblk > 1:
            def n_body(n, __):
                tile(r, n, False)
                return 0
            lax.fori_loop(1, nblk, n_body, 0)
        return 0

    lax.fori_loop(0, d, r_body, 0)
    if gi == ng - 1:
        for c in range(NCH):
            o_ref[:, c * LANES:(c + 1) * LANES] = a_s[c].astype(o_ref.dtype)


def _dil_kernel(*refs, seq):
    ncol = 3 * DIL_HEADS * HEAD_DIM // LANES
    x_refs = refs[:ncol]
    o_ref, m_s, l_s, a_s = refs[ncol:]
    g = pl.program_id(1)
    ng = len(DIL_GROUPS)
    for gi, (window, d) in enumerate(DIL_GROUPS):
        @pl.when(g == gi)
        def _(gi=gi, window=window, d=d):
            _dil_group(x_refs, o_ref, m_s, l_s, a_s, gi=gi, ng=ng, window=window, d=d, seq=seq)


def _dil_attn(x, batch, seq):
    T = x.shape[0]
    HW = DIL_HEADS * HEAD_DIM
    ncol = 3 * HW // LANES
    kern = functools.partial(_dil_kernel, seq=seq)
    col_spec = lambda c: pl.BlockSpec((seq, LANES), lambda b, g: (b, g * ncol + c))
    return pl.pallas_call(
        kern, grid=(batch, len(DIL_GROUPS)),
        in_specs=[col_spec(c) for c in range(ncol)],
        out_specs=pl.BlockSpec((seq, HW), lambda b, g: (b, 0)),
        out_shape=jax.ShapeDtypeStruct((T, HW), BF16),
        scratch_shapes=[pltpu.VMEM((HW // LANES, seq, LANES), F32)] * 3,
        compiler_params=_params("parallel", "arbitrary"), name="dil_attn")(*([x] * ncol))


def _dsa_kernel(q_ref, iq_ref, kv_ref, ikw_ref, o_ref, key_s, bias_s, *, tq, ck, top, seq):
    qi = pl.program_id(1)
    lo = qi * tq
    nch = (lo + tq + ck - 1) // ck
    rowg = lo + lax.broadcasted_iota(I32, (tq, ck), 0)
    coli = lax.broadcasted_iota(I32, (tq, ck), 1)

    def chunk(c):
        return pl.ds(pl.multiple_of(c * ck, ck), ck)

    @pl.when(lo + tq <= top)
    def _():
        def body(c, _):
            bias_s[:, chunk(c)] = jnp.where(c * ck + coli <= rowg, 0.0, NEG_INF).astype(F32)
            return 0
        lax.fori_loop(0, nch, body, 0)

    @pl.when(lo + tq > top)
    def _():
        iw = ikw_ref[pl.ds(pl.multiple_of(lo, tq), tq), IDX_DIM:IDX_DIM + IDX_HEADS]
        iw = iw * (IDX_HEADS ** -0.5 * IDX_DIM ** -0.5)
        iq = iq_ref[...]

        def score_body(c, _):
            ik = ikw_ref[chunk(c), 0:IDX_DIM].astype(BF16)
            sc = jnp.zeros((tq, ck), F32)
            for h in range(IDX_HEADS):
                rel = jnp.maximum(_dot_t(iq[:, h * IDX_DIM:(h + 1) * IDX_DIM], ik), 0.0)
                sc = sc + iw[:, h:h + 1] * rel
            sc = jnp.where(c * ck + coli <= rowg, sc, NEG_INF)
            bits = lax.bitcast_convert_type(sc, I32)
            key_s[:, chunk(c)] = bits ^ ((bits >> 31) & 0x7FFFFFFF)
            return 0
        lax.fori_loop(0, nch, score_body, 0)

        def count(pred):
            def body(c, cnt):
                return cnt + jnp.sum(pred(key_s[:, chunk(c)], c), axis=1, keepdims=True)
            return lax.fori_loop(0, nch, body, jnp.zeros((tq, 1), F32))

        def count_ge(cand):
            return count(lambda kc, c: jnp.where(kc >= cand, 1.0, 0.0))

        topf = float(top)
        tau = jnp.where(count_ge(jnp.zeros((tq, 1), I32)) >= topf, 0, INT_MIN).astype(I32)

        def search(i, tau):
            cand = tau | jnp.left_shift(jnp.int32(1), 30 - i)
            return jnp.where(count_ge(cand) >= topf, cand, tau)
        tau = lax.fori_loop(0, 31, search, tau)

        need = topf - count_ge(tau + 1)

        def count_eq_lt(J):
            return count(lambda kc, c: jnp.where(kc == tau, jnp.where(c * ck + coli < J, 1.0, 0.0), 0.0))

        nbits = seq.bit_length()

        def tie(i, J):
            cand = J + jnp.left_shift(jnp.int32(1), nbits - 1 - i)
            return jnp.where(count_eq_lt(cand) <= need, cand, J)
        J = lax.fori_loop(0, nbits, tie, jnp.zeros((tq, 1), I32))

        def bias_body(c, _):
            kc = key_s[:, chunk(c)]
            sel = jnp.where(kc > tau, 0.0, jnp.where(kc == tau, jnp.where(c * ck + coli < J, 0.0, NEG_INF), NEG_INF))
            bias_s[:, chunk(c)] = sel.astype(F32)
            return 0
        lax.fori_loop(0, nch, bias_body, 0)

    outs = []
    for h in range(DSA_HEADS):
        q = q_ref[:, h * HEAD_DIM:(h + 1) * HEAD_DIM]

        def step(c, carry, q=q):
            m, l, acc = carry
            k = kv_ref[chunk(c), 0:HEAD_DIM]
            v = kv_ref[chunk(c), HEAD_DIM:2 * HEAD_DIM]
            s = _dot_t(q, k) + bias_s[:, chunk(c)]
            m_new = jnp.maximum(m, jnp.max(s, axis=1, keepdims=True))
            alpha = jnp.exp(m - m_new)
            p = jnp.exp(s - m_new)
            l = alpha * l + jnp.sum(p, axis=1, keepdims=True)
            acc = alpha * acc + _dot(p.astype(BF16), v)
            return m_new, l, acc

        init = (jnp.full((tq, 1), M_FLOOR, F32), jnp.zeros((tq, 1), F32), jnp.zeros((tq, HEAD_DIM), F32))
        _, l, acc = lax.fori_loop(0, nch, step, init)
        outs.append(acc / l)
    o_ref[...] = jnp.concatenate(outs, axis=1).astype(o_ref.dtype)


def _dsa_attn(q, kv, iq, ikw, batch, seq, tq=128, ck=512):
    T = q.shape[0]
    nq = seq // tq
    top = min(TOPK_MAX, seq // 4)
    assert top % tq == 0 and seq % ck == 0
    kern = functools.partial(_dsa_kernel, tq=tq, ck=ck, top=top, seq=seq)
    W = DSA_HEADS * HEAD_DIM
    return pl.pallas_call(
        kern, grid=(batch, nq),
        in_specs=[pl.BlockSpec((tq, W), lambda b, i: (b * nq + i, 0)),
                  pl.BlockSpec((tq, IDX_HEADS * IDX_DIM), lambda b, i: (b * nq + i, 0)),
                  pl.BlockSpec((seq, LANES), lambda b, i: (b, 0)),
                  pl.BlockSpec((seq, LANES), lambda b, i: (b, 0))],
        out_specs=pl.BlockSpec((tq, W), lambda b, i: (b * nq + i, 0)),
        out_shape=jax.ShapeDtypeStruct((T, W), BF16),
        scratch_shapes=[pltpu.VMEM((tq, seq), I32), pltpu.VMEM((tq, seq), F32)],
        compiler_params=_params("parallel", "arbitrary"), name="dsa_attn")(q, iq, kv, ikw)


def _mix_out_kernel(x_ref, oa_ref, ob_ref, oc_ref, wg_ref, bg_ref, wa_ref, wb_ref, wc_ref, wo_ref,
                    lg_ref, lb_ref, o_ref, *, alpha, d_model):
    x = x_ref[...]
    xb = x.astype(BF16)
    merged = None
    for br, (o_r, w_r) in enumerate(((oa_ref, wa_ref), (ob_ref, wb_ref), (oc_ref, wc_ref))):
        sl = slice(br * d_model, (br + 1) * d_model)
        gate = jax.nn.sigmoid(_dot(xb, wg_ref[:, sl]) + bg_ref[:, sl])
        term = gate * _dot(o_r[...], w_r[...])
        merged = term if merged is None else merged + term
    mix = _dot(merged.astype(BF16), wo_ref[...])
    o_ref[...] = _layer_norm(alpha * x + mix, lg_ref[...], lb_ref[...])


def _mix_out(x, oa, ob, oc, wg, bg, wa, wb, wc, wo, lg, lb, alpha, tm=256):
    T, D = x.shape
    const = lambda i: (0, 0)
    full = lambda a: pl.BlockSpec(a.shape, const)
    row = lambda a: pl.BlockSpec((tm, a.shape[1]), lambda i: (i, 0))
    kern = functools.partial(_mix_out_kernel, alpha=alpha, d_model=D)
    return pl.pallas_call(
        kern, grid=(T // tm,),
        in_specs=[row(x), row(oa), row(ob), row(oc), full(wg), full(bg), full(wa), full(wb), full(wc),
                  full(wo), full(lg), full(lb)],
        out_specs=pl.BlockSpec((tm, D), lambda i: (i, 0)),
        out_shape=jax.ShapeDtypeStruct((T, D), F32),
        compiler_params=_params("parallel"), name="mix_out")(x, oa, ob, oc, wg, bg, wa, wb, wc, wo, lg, lb)


def _route(x, wr_hi_ref, wr_lo_ref, br_ref):
    xh = x.astype(BF16)
    xl = (x - xh.astype(F32)).astype(BF16)
    logits = _dot(xh, wr_hi_ref[...]) + _dot(xl, wr_hi_ref[...]) + _dot(xh, wr_lo_ref[...]) + br_ref[...]
    lane = lax.broadcasted_iota(I32, logits.shape, 1).astype(F32)
    none = float(LANES)
    glog = jnp.where(lane < N_GROUPS, logits, NEG_INF)
    gmax = jnp.max(glog, axis=1, keepdims=True)
    g_p = 1.0 / jnp.sum(jnp.exp(glog - gmax), axis=1, keepdims=True)
    g_idx = jnp.min(jnp.where(glog == gmax, lane, none), axis=1, keepdims=True)
    first = N_GROUPS + g_idx * EXPERTS_PER_GROUP
    sub = jnp.where(lane >= first, jnp.where(lane < first + EXPERTS_PER_GROUP, logits, NEG_INF), NEG_INF)
    v1 = jnp.max(sub, axis=1, keepdims=True)
    i1 = jnp.min(jnp.where(sub == v1, lane, none), axis=1, keepdims=True)
    sub2 = jnp.where(lane == i1, NEG_INF, sub)
    v2 = jnp.max(sub2, axis=1, keepdims=True)
    i2 = jnp.min(jnp.where(sub2 == v2, lane, none), axis=1, keepdims=True)
    e2 = jnp.exp(v2 - v1)
    w1 = g_p / (1.0 + e2)
    w2 = g_p * e2 / (1.0 + e2)
    return jnp.where(lane == i1, w1, jnp.where(lane == i2, w2, 0.0))


def _moe_kernel(x_ref, wrh_ref, wrl_ref, br_ref, w1_ref, w3_ref, w2_ref, lg_ref, lb_ref, o_ref,
                xb_s, comb_s, acc_s, *, alpha):
    e = pl.program_id(1)

    @pl.when(e == 0)
    def _():
        x = x_ref[...]
        xb_s[...] = x.astype(BF16)
        comb_s[...] = _route(x, wrh_ref, wrl_ref, br_ref)
        acc_s[...] = jnp.zeros_like(acc_s)

    comb = comb_s[...]
    lane = lax.broadcasted_iota(I32, comb.shape, 1)
    we = jnp.sum(jnp.where(lane == N_GROUPS + e, comb, 0.0), axis=1, keepdims=True)
    xb = xb_s[...]
    a = _dot(xb, w1_ref[...])
    b = _dot(xb, w3_ref[...])
    hid = (a * jax.nn.sigmoid(a) * b).astype(BF16)
    acc_s[...] += we * _dot(hid, w2_ref[...])

    @pl.when(e == N_EXPERTS - 1)
    def _():
        o_ref[...] = _layer_norm(alpha * x_ref[...] + acc_s[...], lg_ref[...], lb_ref[...])


def _moe(x, wrh, wrl, br, w1, w3, w2, lg, lb, alpha, tm=512):
    T, D = x.shape
    const = lambda i, e: (0, 0)
    full = lambda a: pl.BlockSpec(a.shape, const)
    kern = functools.partial(_moe_kernel, alpha=alpha)
    return pl.pallas_call(
        kern, grid=(T // tm, N_EXPERTS),
        in_specs=[pl.BlockSpec((tm, D), lambda i, e: (i, 0)), full(wrh), full(wrl), full(br),
                  pl.BlockSpec((None, D, D_EXPERT), lambda i, e: (e, 0, 0)),
                  pl.BlockSpec((None, D, D_EXPERT), lambda i, e: (e, 0, 0)),
                  pl.BlockSpec((None, D_EXPERT, D), lambda i, e: (e, 0, 0)),
                  full(lg), full(lb)],
        out_specs=pl.BlockSpec((tm, D), lambda i, e: (i, 0)),
        out_shape=jax.ShapeDtypeStruct((T, D), F32),
        scratch_shapes=[pltpu.VMEM((tm, D), BF16), pltpu.VMEM((tm, LANES), F32), pltpu.VMEM((tm, D), F32)],
        compiler_params=_params("parallel", "arbitrary"), name="moe")(x, wrh, wrl, br, w1, w3, w2, lg, lb)


def _rope_tables(seq):
    pos = jnp.arange(seq, dtype=F32)[:, None]
    one = lambda n: jnp.ones((seq, n), F32)
    zero = lambda n: jnp.zeros((seq, n), F32)

    inv_p = ROPE_THETA ** (-jnp.arange(0, ROT_DIM, 2, dtype=F32) / ROT_DIM)
    cp, sp = jnp.cos(pos * inv_p), jnp.sin(pos * inv_p)
    hp = ROT_DIM // 2
    rest = HEAD_DIM - ROT_DIM
    head = (jnp.concatenate([cp, cp, one(rest)], 1),
            jnp.concatenate([zero(hp), sp, zero(rest)], 1),
            jnp.concatenate([-sp, zero(hp), zero(rest)], 1))
    ident = (one(HEAD_DIM), zero(HEAD_DIM), zero(HEAD_DIM))
    both = jnp.stack([jnp.concatenate([a, a], 1) for a in head])
    first = jnp.stack([jnp.concatenate([a, b], 1) for a, b in zip(head, ident)])
    tab_p = jnp.stack([both, first])

    inv_m = ROPE_THETA ** (-jnp.arange(0, MLA_ROPE, 2, dtype=F32) / MLA_ROPE)
    cm, sm = jnp.cos(pos * inv_m), jnp.sin(pos * inv_m)
    hm = MLA_ROPE // 2
    pad = LANES - MLA_NOPE - MLA_ROPE
    tab_m = jnp.stack([jnp.concatenate([one(MLA_NOPE), cm, cm, one(pad)], 1),
                       jnp.concatenate([zero(MLA_NOPE), zero(hm), sm, zero(pad)], 1),
                       jnp.concatenate([zero(MLA_NOPE), -sm, zero(hm), zero(pad)], 1)])[None]
    return tab_p, tab_m


def _layer_weights(p, l):
    w_in = p["w_in"][l]
    D = w_in.shape[0]
    z = lambda n: jnp.zeros((D, n), F32)
    o = 0
    cuts = []
    for n in (MLA_Q_RANK, MLA_KV_RANK, MLA_ROPE, 3 * 3 * DIL_HEADS * HEAD_DIM, DSA_HEADS * HEAD_DIM,
              HEAD_DIM, HEAD_DIM, IDX_HEADS * IDX_DIM, IDX_DIM, IDX_HEADS):
        cuts.append(w_in[:, o:o + n])
        o += n
    w_cq, w_ckv, w_kr, w_dil, w_q, w_k, w_v, w_iq, w_ik, w_iw = cuts
    qs = HEAD_DIM ** -0.5
    w_mla = jnp.concatenate([w_cq, w_ckv, z(MLA_NOPE), w_kr, z(LANES - MLA_NOPE - MLA_ROPE)], 1)
    HW = DIL_HEADS * HEAD_DIM
    dil_scale = jnp.tile(jnp.concatenate([jnp.full((HW,), qs, F32), jnp.ones((2 * HW,), F32)]), 3)
    w_dil = w_dil * dil_scale[None, :]
    w_dsa = jnp.concatenate([w_q * qs, w_k, w_v, w_iq, w_ik, w_iw, z(LANES - IDX_DIM - IDX_HEADS)], 1)

    def per_head(w, n_in, n_keep_lo, n_keep_hi):
        r = w.shape[0]
        w = w.reshape(r, MLA_HEADS, n_in)[:, :, n_keep_lo:n_keep_hi]
        w = jnp.pad(w, ((0, 0), (0, 0), (0, LANES - (n_keep_hi - n_keep_lo))))
        return w.reshape(r, MLA_HEADS * LANES)

    w_uq = per_head(p["w_uq"][l], MLA_NOPE + MLA_ROPE, 0, MLA_NOPE + MLA_ROPE)
    w_uk = per_head(p["w_ukv"][l], MLA_NOPE + MLA_V, 0, MLA_NOPE)
    w_uv = p["w_ukv"][l].reshape(MLA_KV_RANK, MLA_HEADS, MLA_NOPE + MLA_V)[:, :, MLA_NOPE:]
    w_uv = w_uv.reshape(MLA_KV_RANK, MLA_HEADS * MLA_V)

    w_r = jnp.concatenate([p["w_group"][l], p["w_sub"][l], z(LANES - N_GROUPS - N_EXPERTS)], 1)
    w_r_hi = w_r.astype(BF16)
    w_r_lo = (w_r - w_r_hi.astype(F32)).astype(BF16)
    b_r = jnp.concatenate([p["b_group"][l], p["b_sub"][l], jnp.zeros((LANES - N_GROUPS - N_EXPERTS,), F32)])

    bf = lambda a: a.astype(BF16)
    return dict(
        w_mla=bf(w_mla), w_dil=bf(w_dil), w_dsa=bf(w_dsa),
        q_g=p["q_norm_g"][l][None], kv_g=p["kv_norm_g"][l][None],
        w_uq=bf(w_uq), w_uk=bf(w_uk), w_uv=bf(w_uv),
        w_gate=bf(p["w_gate"][l]), b_gate=p["b_gate"][l][None],
        w_a=bf(p["w_a"][l]), w_b=bf(p["w_b"][l]), w_c=bf(p["w_c"][l]), w_o=bf(p["w_o"][l]),
        ln1_g=p["ln1_g"][l][None], ln1_b=p["ln1_b"][l][None],
        w_r_hi=w_r_hi, w_r_lo=w_r_lo, b_r=b_r[None],
        w1=bf(p["w1"][l]), w3=bf(p["w3"][l]), w2=bf(p["w2"][l]),
        ln2_g=p["ln2_g"][l][None], ln2_b=p["ln2_b"][l][None])


def _layer(xt, w, tab_p, tab_m, batch, seq, alpha):
    HW = DIL_HEADS * HEAD_DIM
    (grp,) = _proj(xt, w["w_mla"], seq, tn=w["w_mla"].shape[1], segs=((0, w["w_mla"].shape[1]),),
                   dtypes=(F32,), kinds=(-1,) * (w["w_mla"].shape[1] // LANES))
    (dil,) = _proj(xt, w["w_dil"], seq, tn=3 * HW, segs=((0, 3 * HW),), dtypes=(F32,),
                   kinds=(0, 0, 0, 0, -1, -1), tab=tab_p)
    qw = DSA_HEADS * HEAD_DIM
    iw_ = IDX_HEADS * IDX_DIM
    q_c, kv_c, iq_c, ikw_c = _proj(
        xt, w["w_dsa"], seq, tn=w["w_dsa"].shape[1],
        segs=((0, qw), (qw, LANES), (qw + LANES, iw_), (qw + LANES + iw_, LANES)),
        dtypes=(BF16, BF16, BF16, F32),
        kinds=(0,) * (qw // LANES) + (1,) + (0,) * (iw_ // LANES) + (1,), tab=tab_p)

    q_a, k_a, v_a = _mla_up(grp, w["q_g"], w["kv_g"], w["w_uq"], w["w_uk"], w["w_uv"], tab_m, seq)
    o_a = _mla_attn(q_a, k_a, v_a, batch, seq)
    o_b = _dil_attn(dil, batch, seq)
    o_c = _dsa_attn(q_c, kv_c, iq_c, ikw_c, batch, seq)

    x1 = _mix_out(xt, o_a, o_b, o_c, w["w_gate"], w["b_gate"], w["w_a"], w["w_b"], w["w_c"], w["w_o"],
                  w["ln1_g"], w["ln1_b"], alpha)
    return _moe(x1, w["w_r_hi"], w["w_r_lo"], w["b_r"], w["w1"], w["w3"], w["w2"],
                w["ln2_g"], w["ln2_b"], alpha)


def kernel(x, w_in, q_norm_g, w_uq, kv_norm_g, w_ukv, w_gate, b_gate, w_a, w_b, w_c, w_o, ln1_g, ln1_b,
           w_group, b_group, w_sub, b_sub, w1, w3, w2, ln2_g, ln2_b):
    batch, seq, d_model = x.shape
    depth = w_in.shape[0]
    alpha = (2 * depth) ** 0.25
    p = dict(w_in=w_in, q_norm_g=q_norm_g, w_uq=w_uq, kv_norm_g=kv_norm_g, w_ukv=w_ukv, w_gate=w_gate,
             b_gate=b_gate, w_a=w_a, w_b=w_b, w_c=w_c, w_o=w_o, ln1_g=ln1_g, ln1_b=ln1_b, w_group=w_group,
             b_group=b_group, w_sub=w_sub, b_sub=b_sub, w1=w1, w3=w3, w2=w2, ln2_g=ln2_g, ln2_b=ln2_b)
    tab_p, tab_m = _rope_tables(seq)
    xt = x.reshape(batch * seq, d_model)
    for l in range(depth):
        xt = _layer(xt, _layer_weights(p, l), tab_p, tab_m, batch, seq, alpha)
    return xt.reshape(batch, seq, d_model)
```

```python
import functools

import jax
import jax.numpy as jnp
from jax import lax
from jax.experimental import pallas as pl
from jax.experimental.pallas import tpu as pltpu

F32 = jnp.float32
BF16 = jnp.bfloat16
I32 = jnp.int32

HEAD_DIM = 64
ROT_DIM = HEAD_DIM // 4
ROPE_THETA = 500000.0
EPS = 1e-6
MLA_HEADS = 8
MLA_Q_RANK = 256
MLA_KV_RANK = 128
MLA_NOPE = 64
MLA_ROPE = 32
MLA_V = 64
DIL_GROUPS = ((128, 1), (512, 4), (2048, 16))
DIL_HEADS = 4
DSA_HEADS = 8
IDX_HEADS = 8
IDX_DIM = 64
TOPK_MAX = 256
N_GROUPS = 4
EXPERTS_PER_GROUP = 8
N_EXPERTS = N_GROUPS * EXPERTS_PER_GROUP
D_EXPERT = 256
N_BRANCH = 3

LANES = 128
VMEM_LIMIT = 56 * 1024 * 1024
NEG_INF = float("-inf")
INT_MIN = -2 ** 31
M_FLOOR = -1e30
LOG2E = 1.4426950408889634


def _params(*sem):
    return pltpu.CompilerParams(dimension_semantics=sem, vmem_limit_bytes=VMEM_LIMIT)


def _dot(a, b):
    return jnp.dot(a, b, preferred_element_type=F32)


def _dot_t(a, b):
    return lax.dot_general(a, b, (((1,), (1,)), ((), ())), preferred_element_type=F32)


def _rope_chunk(v, tab_ref, kind, shift):
    c = tab_ref[kind, 0]
    s1 = tab_ref[kind, 1]
    s2 = tab_ref[kind, 2]
    return v * c + pltpu.roll(v, shift, 1) * s1 + pltpu.roll(v, LANES - shift, 1) * s2


def _layer_norm(z, g, b):
    mu = jnp.mean(z, axis=-1, keepdims=True)
    zc = z - mu
    var = jnp.mean(zc * zc, axis=-1, keepdims=True)
    return zc * lax.rsqrt(var + EPS) * g + b


def _proj_kernel(*refs, segs, kinds, shift, has_tab):
    x_ref, w_ref = refs[0], refs[1]
    tab_ref = refs[2] if has_tab else None
    outs = refs[3:] if has_tab else refs[2:]
    acc = _dot(x_ref[...].astype(BF16), w_ref[...])
    for (c0, width), o_ref in zip(segs, outs):
        for c in range(width // LANES):
            cc = c0 // LANES + c
            v = acc[:, cc * LANES:(cc + 1) * LANES]
            if kinds[cc] >= 0:
                v = _rope_chunk(v, tab_ref, kinds[cc], shift)
            o_ref[:, c * LANES:(c + 1) * LANES] = v.astype(o_ref.dtype)


def _proj(x, w, seq, *, tn, segs, dtypes, kinds, tab=None, shift=ROT_DIM // 2, tm=512):
    T, K = x.shape
    N = w.shape[1]
    nj = N // tn
    spt = seq // tm
    in_specs = [pl.BlockSpec((tm, K), lambda i, j: (i, 0)),
                pl.BlockSpec((K, tn), lambda i, j: (0, j))]
    args = [x, w]
    if tab is not None:
        in_specs.append(pl.BlockSpec((tab.shape[0], 3, tm, LANES), lambda i, j: (0, 0, i % spt, 0)))
        args.append(tab)
    out_specs = [pl.BlockSpec((tm, wd), lambda i, j: (i, j)) for (_, wd) in segs]
    out_shape = [jax.ShapeDtypeStruct((T, wd * nj), dt) for (_, wd), dt in zip(segs, dtypes)]
    kern = functools.partial(_proj_kernel, segs=segs, kinds=kinds, shift=shift, has_tab=tab is not None)
    return pl.pallas_call(kern, grid=(T // tm, nj), in_specs=in_specs, out_specs=out_specs,
                          out_shape=out_shape, compiler_params=_params("parallel", "arbitrary"),
                          name="in_proj")(*args)


def _mla_up_kernel(g_ref, qg_ref, kvg_ref, wq_ref, wk_ref, wv_ref, tab_ref, q_out, k_out, v_out):
    g = g_ref[...]
    cq = g[:, :MLA_Q_RANK]
    ckv = g[:, MLA_Q_RANK:MLA_Q_RANK + MLA_KV_RANK]
    kr = g[:, MLA_Q_RANK + MLA_KV_RANK:]
    cqn = (cq * lax.rsqrt(jnp.mean(cq * cq, axis=-1, keepdims=True) + EPS) * qg_ref[...]).astype(BF16)
    ckvn = (ckv * lax.rsqrt(jnp.mean(ckv * ckv, axis=-1, keepdims=True) + EPS) * kvg_ref[...]).astype(BF16)
    q = _dot(cqn, wq_ref[...])
    k = _dot(ckvn, wk_ref[...])
    v_out[...] = _dot_t(wv_ref[...], ckvn).astype(v_out.dtype)
    for h in range(MLA_HEADS):
        sl = slice(h * LANES, (h + 1) * LANES)
        q_out[:, sl] = _rope_chunk(q[:, sl], tab_ref, 0, MLA_ROPE // 2).astype(q_out.dtype)
        k_out[:, sl] = _rope_chunk(k[:, sl] + kr, tab_ref, 0, MLA_ROPE // 2).astype(k_out.dtype)


def _mla_up(grp, qg, kvg, wq, wk, wv, tab, seq, tm=512):
    T = grp.shape[0]
    spt = seq // tm
    const = lambda i: (0, 0)
    return pl.pallas_call(
        _mla_up_kernel, grid=(T // tm,),
        in_specs=[pl.BlockSpec((tm, grp.shape[1]), lambda i: (i, 0)),
                  pl.BlockSpec(qg.shape, const), pl.BlockSpec(kvg.shape, const),
                  pl.BlockSpec(wq.shape, const), pl.BlockSpec(wk.shape, const), pl.BlockSpec(wv.shape, const),
                  pl.BlockSpec((1, 3, tm, LANES), lambda i: (0, 0, i % spt, 0))],
        out_specs=[pl.BlockSpec((tm, MLA_HEADS * LANES), lambda i: (i, 0)),
                   pl.BlockSpec((tm, MLA_HEADS * LANES), lambda i: (i, 0)),
                   pl.BlockSpec((MLA_HEADS * MLA_V, tm), lambda i: (0, i))],
        out_shape=[jax.ShapeDtypeStruct((T, MLA_HEADS * LANES), BF16),
                   jax.ShapeDtypeStruct((T, MLA_HEADS * LANES), BF16),
                   jax.ShapeDtypeStruct((MLA_HEADS * MLA_V, T), BF16)],
        compiler_params=_params("parallel"), name="mla_up")(grp, qg, kvg, wq, wk, wv, tab)


def _mla_attn_kernel(q_ref, k_ref, vt_ref, o_ref, acc_s, s_s, p_s, *, tq, c_exp):
    qi = pl.program_id(1)
    krow = lax.broadcasted_iota(I32, (tq, tq), 0)
    qcol = lax.broadcasted_iota(I32, (tq, tq), 1)
    diag_bias = jnp.where(krow <= qcol, 0.0, NEG_INF).astype(F32)
    acc_s[...] = jnp.zeros_like(acc_s)

    def step(j, carry, masked):
        ms, ls = carry
        ks = pl.multiple_of(j * tq, tq)
        for h in range(MLA_HEADS):
            kb = k_ref[pl.ds(ks, tq), h * LANES:(h + 1) * LANES]
            s_s[h] = _dot_t(kb, q_ref[:, h * LANES:(h + 1) * LANES])
        new_ms, new_ls, alphas = [], [], []
        for h in range(MLA_HEADS):
            s = s_s[h]
            if masked:
                s = s + diag_bias
            m_new = jnp.maximum(ms[h], jnp.max(s, axis=0, keepdims=True))
            alpha = jnp.exp2((ms[h] - m_new) * c_exp)
            p = jnp.exp2((s - m_new) * c_exp)
            new_ls.append(alpha * ls[h] + jnp.sum(p, axis=0, keepdims=True))
            new_ms.append(m_new)
            alphas.append(alpha)
            p_s[h] = p.astype(BF16)
        for h in range(MLA_HEADS):
            sl = slice(h * MLA_V, (h + 1) * MLA_V)
            acc_s[sl, :] = alphas[h] * acc_s[sl, :] + _dot(vt_ref[sl, pl.ds(ks, tq)], p_s[h])
        return tuple(new_ms), tuple(new_ls)

    init = (tuple(jnp.full((1, tq), NEG_INF, F32) for _ in range(MLA_HEADS)),
            tuple(jnp.zeros((1, tq), F32) for _ in range(MLA_HEADS)))
    carry = lax.fori_loop(0, qi, functools.partial(step, masked=False), init)
    _, ls = step(qi, carry, True)
    for h in range(MLA_HEADS):
        sl = slice(h * MLA_V, (h + 1) * MLA_V)
        acc_s[sl, :] = acc_s[sl, :] / ls[h]
    o_ref[...] = acc_s[...].T.astype(o_ref.dtype)


def _mla_attn(q, k, vt, batch, seq, tq=256):
    T = q.shape[0]
    nq = seq // tq
    c_exp = (MLA_NOPE + MLA_ROPE) ** -0.5 * LOG2E
    kern = functools.partial(_mla_attn_kernel, tq=tq, c_exp=c_exp)
    W = MLA_HEADS * LANES
    return pl.pallas_call(
        kern, grid=(batch, nq),
        in_specs=[pl.BlockSpec((tq, W), lambda b, i: (b * nq + i, 0)),
                  pl.BlockSpec((seq, W), lambda b, i: (b, 0)),
                  pl.BlockSpec((MLA_HEADS * MLA_V, seq), lambda b, i: (0, b))],
        out_specs=pl.BlockSpec((tq, MLA_HEADS * MLA_V), lambda b, i: (b * nq + i, 0)),
        out_shape=jax.ShapeDtypeStruct((T, MLA_HEADS * MLA_V), BF16),
        scratch_shapes=[pltpu.VMEM((MLA_HEADS * MLA_V, tq), F32),
                        pltpu.VMEM((MLA_HEADS, tq, tq), F32), pltpu.VMEM((MLA_HEADS, tq, tq), BF16)],
        compiler_params=_params("parallel", "arbitrary"), name="mla_attn")(q, k, vt)


def _dil_group(x_refs, o_ref, m_s, l_s, a_s, *, gi, ng, window, d, seq):
    band = window // d
    M = seq // d
    assert M % band == 0 and band % 8 == 0
    nblk = M // band
    NCH = DIL_HEADS * HEAD_DIM // LANES
    q_refs, k_refs, v_refs = x_refs[:NCH], x_refs[NCH:2 * NCH], x_refs[2 * NCH:]
    ri = lax.broadcasted_iota(I32, (band, band), 0)
    ci = lax.broadcasted_iota(I32, (band, band), 1)
    bias0 = jnp.where(ci <= ri, 0.0, NEG_INF).astype(F32)
    ri2 = lax.broadcasted_iota(I32, (band, 2 * band), 0)
    ci2 = lax.broadcasted_iota(I32, (band, 2 * band), 1)
    bias1 = jnp.where((ci2 >= ri2), jnp.where(ci2 <= ri2 + band, 0.0, NEG_INF), NEG_INF).astype(F32)

    def rows(start, size):
        if d == 1:
            return pl.ds(start, size)
        return pl.ds(start, size, stride=d)

    def tile(r, n, first):
        q_rows = rows(n * (band * d) + r, band)
        k_rows = q_rows if first else rows((n - 1) * (band * d) + r, 2 * band)
        bias = bias0 if first else bias1
        for c in range(NCH):
            q = q_refs[c][q_rows, :].astype(BF16)
            k = k_refs[c][k_rows, :].astype(BF16)
            v = v_refs[c][k_rows, :].astype(BF16)
            accs, ms, ls = [], [], []
            for h in range(LANES // HEAD_DIM):
                sl = slice(h * HEAD_DIM, (h + 1) * HEAD_DIM)
                s = _dot_t(q[:, sl], k[:, sl]) + bias
                m = jnp.max(s, axis=1, keepdims=True)
                p = jnp.exp(s - m)
                l = jnp.sum(p, axis=1, keepdims=True)
                accs.append(_dot(p.astype(BF16), v[:, sl]))
                ms.append(jnp.broadcast_to(m, (band, HEAD_DIM)))
                ls.append(jnp.broadcast_to(l, (band, HEAD_DIM)))
            A = jnp.concatenate(accs, axis=1)
            Mb = jnp.concatenate(ms, axis=1)
            Lb = jnp.concatenate(ls, axis=1)
            if gi > 0:
                m_old = m_s[c, q_rows, :]
                m_new = jnp.maximum(m_old, Mb)
                w_old = jnp.exp(m_old - m_new)
                w_cur = jnp.exp(Mb - m_new)
                A = a_s[c, q_rows, :] * w_old + A * w_cur
                Lb = l_s[c, q_rows, :] * w_old + Lb * w_cur
                Mb = m_new
            if gi == ng - 1:
                a_s[c, q_rows, :] = A / Lb
            else:
                m_s[c, q_rows, :] = Mb
                l_s[c, q_rows, :] = Lb
                a_s[c, q_rows, :] = A

    def r_body(r, _):
        tile(r, 0, True)
        if nblk > 1:
            def n_body(n, __):
                tile(r, n, False)
                return 0
            lax.fori_loop(1, nblk, n_body, 0)
        return 0

    lax.fori_loop(0, d, r_body, 0)
    if gi == ng - 1:
        for c in range(NCH):
            o_ref[:, c * LANES:(c + 1) * LANES] = a_s[c].astype(o_ref.dtype)


def _dil_kernel(*refs, seq):
    ncol = 3 * DIL_HEADS * HEAD_DIM // LANES
    x_refs = refs[:ncol]
    o_ref, m_s, l_s, a_s = refs[ncol:]
    g = pl.program_id(1)
    ng = len(DIL_GROUPS)
    for gi, (window, d) in enumerate(DIL_GROUPS):
        @pl.when(g == gi)
        def _(gi=gi, window=window, d=d):
            _dil_group(x_refs, o_ref, m_s, l_s, a_s, gi=gi, ng=ng, window=window, d=d, seq=seq)


def _dil_attn(x, batch, seq):
    T = x.shape[0]
    HW = DIL_HEADS * HEAD_DIM
    ncol = 3 * HW // LANES
    kern = functools.partial(_dil_kernel, seq=seq)
    col_spec = lambda c: pl.BlockSpec((seq, LANES), lambda b, g: (b, g * ncol + c))
    return pl.pallas_call(
        kern, grid=(batch, len(DIL_GROUPS)),
        in_specs=[col_spec(c) for c in range(ncol)],
        out_specs=pl.BlockSpec((seq, HW), lambda b, g: (b, 0)),
        out_shape=jax.ShapeDtypeStruct((T, HW), BF16),
        scratch_shapes=[pltpu.VMEM((HW // LANES, seq, LANES), F32)] * 3,
        compiler_params=_params("parallel", "arbitrary"), name="dil_attn")(*([x] * ncol))


def _dsa_kernel(q_ref, iq_ref, kv_ref, ikw_ref, o_ref, qall_s, iqall_s, kvt_s, key_s, bias_s, j_s, acc_s, p_s,
                *, tq, ck, top, seq):
    qi = pl.program_id(1)
    lo = qi * tq
    nch = (lo + tq + ck - 1) // ck
    krow = lax.broadcasted_iota(I32, (ck, tq), 0)
    qcol = lo + lax.broadcasted_iota(I32, (ck, tq), 1)

    def chunk(c):
        return pl.ds(pl.multiple_of(c * ck, ck), ck)

    @pl.when(qi == 0)
    def _():
        def body(c, _):
            kvt_s[:, chunk(c)] = kv_ref[chunk(c), :].astype(F32).T.astype(BF16)
            return 0
        lax.fori_loop(0, seq // ck, body, 0)

    for h in range(DSA_HEADS):
        qall_s[h * tq:(h + 1) * tq, :] = q_ref[:, h * HEAD_DIM:(h + 1) * HEAD_DIM]
    for h in range(IDX_HEADS):
        iqall_s[h * tq:(h + 1) * tq, :] = iq_ref[:, h * IDX_DIM:(h + 1) * IDX_DIM]

    @pl.when(lo + tq <= top)
    def _():
        def body(c, _):
            bias_s[chunk(c), :] = jnp.where(c * ck + krow <= qcol, 0.0, NEG_INF).astype(F32)
            return 0
        lax.fori_loop(0, nch, body, 0)

    @pl.when(lo + tq > top)
    def _():
        iw_t = ikw_ref[pl.ds(pl.multiple_of(lo, tq), tq), :].T[IDX_DIM:IDX_DIM + IDX_HEADS, :]
        iw_t = iw_t * (IDX_HEADS ** -0.5 * IDX_DIM ** -0.5)

        def score_body(c, _):
            ik = ikw_ref[chunk(c), 0:IDX_DIM].astype(BF16)
            r = _dot_t(ik, iqall_s[...])
            sc = jnp.zeros((ck, tq), F32)
            for h in range(IDX_HEADS):
                sc = sc + iw_t[h:h + 1, :] * jnp.maximum(r[:, h * tq:(h + 1) * tq], 0.0)
            sc = jnp.where(c * ck + krow <= qcol, sc, NEG_INF)
            bits = lax.bitcast_convert_type(sc, I32)
            key_s[chunk(c), :] = bits ^ ((bits >> 31) & 0x7FFFFFFF)
            return 0
        lax.fori_loop(0, nch, score_body, 0)

        def count(pred):
            def body(c, acc):
                ind = pred(key_s[chunk(c), :], c)
                return acc + jnp.sum(ind.reshape(ck // 64, 64, tq), axis=0)
            acc = lax.fori_loop(0, nch, body, jnp.zeros((64, tq), F32))
            return jnp.sum(acc, axis=0, keepdims=True)

        def count_ge(cand):
            return count(lambda kc, c: jnp.where(kc >= cand, 1.0, 0.0))

        topf = float(top)
        tau = jnp.where(count_ge(jnp.zeros((1, tq), I32)) >= topf, 0, INT_MIN).astype(I32)

        def search(i, tau):
            cand = tau | jnp.left_shift(jnp.int32(1), 30 - i)
            return jnp.where(count_ge(cand) >= topf, cand, tau)
        tau = lax.fori_loop(0, 31, search, tau)

        need = topf - count_ge(tau + 1)
        j_s[...] = jnp.full(j_s.shape, seq, I32)

        @pl.when(jnp.max(count_ge(tau)) > topf)
        def _():
            def count_eq_lt(J):
                return count(lambda kc, c: jnp.where(kc == tau, jnp.where(c * ck + krow < J, 1.0, 0.0), 0.0))

            nbits = seq.bit_length()

            def tie(i, J):
                cand = J + jnp.left_shift(jnp.int32(1), nbits - 1 - i)
                return jnp.where(count_eq_lt(cand) <= need, cand, J)
            J = lax.fori_loop(0, nbits, tie, jnp.zeros((1, tq), I32))
            j_s[...] = jnp.broadcast_to(J, j_s.shape)

        J = j_s[0:1, :]

        def bias_body(c, _):
            kc = key_s[chunk(c), :]
            sel = jnp.where(kc > tau, 0.0,
                            jnp.where(kc == tau, jnp.where(c * ck + krow < J, 0.0, NEG_INF), NEG_INF))
            bias_s[chunk(c), :] = sel.astype(F32)
            return 0
        lax.fori_loop(0, nch, bias_body, 0)

    acc_s[...] = jnp.zeros_like(acc_s)

    def step(c, carry):
        ms, ls = carry
        kc = kv_ref[chunk(c), 0:HEAD_DIM]
        bias = bias_s[chunk(c), :]
        vt = kvt_s[HEAD_DIM:2 * HEAD_DIM, chunk(c)]
        new_ms, new_ls, alphas = [], [], []
        for h in range(DSA_HEADS):
            if h % 2 == 0:
                s_pair = _dot_t(kc, qall_s[h * tq:(h + 2) * tq, :])
            s = s_pair[:, (h % 2) * tq:(h % 2 + 1) * tq] + bias
            m_new = jnp.maximum(ms[h], jnp.max(s, axis=0, keepdims=True))
            alpha = jnp.exp(ms[h] - m_new)
            p = jnp.exp(s - m_new)
            new_ls.append(alpha * ls[h] + jnp.sum(p, axis=0, keepdims=True))
            new_ms.append(m_new)
            alphas.append(alpha)
            p_s[h] = p.astype(BF16)
        for h in range(DSA_HEADS):
            sl = slice(h * HEAD_DIM, (h + 1) * HEAD_DIM)
            acc_s[sl, :] = alphas[h] * acc_s[sl, :] + _dot(vt, p_s[h])
        return tuple(new_ms), tuple(new_ls)

    init = (tuple(jnp.full((1, tq), M_FLOOR, F32) for _ in range(DSA_HEADS)),
            tuple(jnp.zeros((1, tq), F32) for _ in range(DSA_HEADS)))
    _, ls = lax.fori_loop(0, nch, step, init)
    for h in range(DSA_HEADS):
        sl = slice(h * HEAD_DIM, (h + 1) * HEAD_DIM)
        acc_s[sl, :] = acc_s[sl, :] / ls[h]
    o_ref[...] = acc_s[...].T.astype(o_ref.dtype)


def _dsa_attn(q, kv, iq, ikw, batch, seq, tq=LANES, ck=256):
    T = q.shape[0]
    nq = seq // tq
    top = min(TOPK_MAX, seq // 4)
    assert tq == LANES and top % tq == 0 and seq % ck == 0
    kern = functools.partial(_dsa_kernel, tq=tq, ck=ck, top=top, seq=seq)
    W = DSA_HEADS * HEAD_DIM
    return pl.pallas_call(
        kern, grid=(batch, nq),
        in_specs=[pl.BlockSpec((tq, W), lambda b, i: (b * nq + i, 0)),
                  pl.BlockSpec((tq, IDX_HEADS * IDX_DIM), lambda b, i: (b * nq + i, 0)),
                  pl.BlockSpec((seq, LANES), lambda b, i: (b, 0)),
                  pl.BlockSpec((seq, LANES), lambda b, i: (b, 0))],
        out_specs=pl.BlockSpec((tq, W), lambda b, i: (b * nq + i, 0)),
        out_shape=jax.ShapeDtypeStruct((T, W), BF16),
        scratch_shapes=[pltpu.VMEM((DSA_HEADS * tq, HEAD_DIM), BF16),
                        pltpu.VMEM((IDX_HEADS * tq, IDX_DIM), BF16),
                        pltpu.VMEM((LANES, seq), BF16),
                        pltpu.VMEM((seq, tq), I32),
                        pltpu.VMEM((seq, tq), F32),
                        pltpu.VMEM((8, tq), I32),
                        pltpu.VMEM((W, tq), F32),
                        pltpu.VMEM((DSA_HEADS, ck, tq), BF16)],
        compiler_params=_params("arbitrary", "arbitrary"), name="dsa_attn")(q, iq, kv, ikw)


def _mix_out_kernel(x_ref, oa_ref, ob_ref, oc_ref, wg_ref, bg_ref, wa_ref, wb_ref, wc_ref, wo_ref,
                    lg_ref, lb_ref, o_ref, *, alpha, d_model):
    x = x_ref[...]
    xb = x.astype(BF16)
    merged = None
    for br, (o_r, w_r) in enumerate(((oa_ref, wa_ref), (ob_ref, wb_ref), (oc_ref, wc_ref))):
        sl = slice(br * d_model, (br + 1) * d_model)
        gate = jax.nn.sigmoid(_dot(xb, wg_ref[:, sl]) + bg_ref[:, sl])
        term = gate * _dot(o_r[...], w_r[...])
        merged = term if merged is None else merged + term
    mix = _dot(merged.astype(BF16), wo_ref[...])
    o_ref[...] = _layer_norm(alpha * x + mix, lg_ref[...], lb_ref[...])


def _mix_out(x, oa, ob, oc, wg, bg, wa, wb, wc, wo, lg, lb, alpha, tm=256):
    T, D = x.shape
    const = lambda i: (0, 0)
    full = lambda a: pl.BlockSpec(a.shape, const)
    row = lambda a: pl.BlockSpec((tm, a.shape[1]), lambda i: (i, 0))
    kern = functools.partial(_mix_out_kernel, alpha=alpha, d_model=D)
    return pl.pallas_call(
        kern, grid=(T // tm,),
        in_specs=[row(x), row(oa), row(ob), row(oc), full(wg), full(bg), full(wa), full(wb), full(wc),
                  full(wo), full(lg), full(lb)],
        out_specs=pl.BlockSpec((tm, D), lambda i: (i, 0)),
        out_shape=jax.ShapeDtypeStruct((T, D), F32),
        compiler_params=_params("parallel"), name="mix_out")(x, oa, ob, oc, wg, bg, wa, wb, wc, wo, lg, lb)


def _route(x, wr_hi_ref, wr_lo_ref, br_ref):
    xh = x.astype(BF16)
    xl = (x - xh.astype(F32)).astype(BF16)
    logits = _dot(xh, wr_hi_ref[...]) + _dot(xl, wr_hi_ref[...]) + _dot(xh, wr_lo_ref[...]) + br_ref[...]
    lane = lax.broadcasted_iota(I32, logits.shape, 1).astype(F32)
    none = float(LANES)
    glog = jnp.where(lane < N_GROUPS, logits, NEG_INF)
    gmax = jnp.max(glog, axis=1, keepdims=True)
    g_p = 1.0 / jnp.sum(jnp.exp(glog - gmax), axis=1, keepdims=True)
    g_idx = jnp.min(jnp.where(glog == gmax, lane, none), axis=1, keepdims=True)
    first = N_GROUPS + g_idx * EXPERTS_PER_GROUP
    sub = jnp.where(lane >= first, jnp.where(lane < first + EXPERTS_PER_GROUP, logits, NEG_INF), NEG_INF)
    v1 = jnp.max(sub, axis=1, keepdims=True)
    i1 = jnp.min(jnp.where(sub == v1, lane, none), axis=1, keepdims=True)
    sub2 = jnp.where(lane == i1, NEG_INF, sub)
    v2 = jnp.max(sub2, axis=1, keepdims=True)
    i2 = jnp.min(jnp.where(sub2 == v2, lane, none), axis=1, keepdims=True)
    e2 = jnp.exp(v2 - v1)
    w1 = g_p / (1.0 + e2)
    w2 = g_p * e2 / (1.0 + e2)
    return jnp.where(lane == i1, w1, jnp.where(lane == i2, w2, 0.0))


def _moe_kernel(x_ref, wrh_ref, wrl_ref, br_ref, w1_ref, w3_ref, w2_ref, lg_ref, lb_ref, o_ref,
                xb_s, comb_s, acc_s, *, alpha):
    e = pl.program_id(1)

    @pl.when(e == 0)
    def _():
        x = x_ref[...]
        xb_s[...] = x.astype(BF16)
        comb_s[...] = _route(x, wrh_ref, wrl_ref, br_ref)
        acc_s[...] = jnp.zeros_like(acc_s)

    comb = comb_s[...]
    lane = lax.broadcasted_iota(I32, comb.shape, 1)
    we = jnp.sum(jnp.where(lane == N_GROUPS + e, comb, 0.0), axis=1, keepdims=True)
    xb = xb_s[...]
    a = _dot(xb, w1_ref[...])
    b = _dot(xb, w3_ref[...])
    hid = (a * jax.nn.sigmoid(a) * b).astype(BF16)
    acc_s[...] += we * _dot(hid, w2_ref[...])

    @pl.when(e == N_EXPERTS - 1)
    def _():
        o_ref[...] = _layer_norm(alpha * x_ref[...] + acc_s[...], lg_ref[...], lb_ref[...])


def _moe(x, wrh, wrl, br, w1, w3, w2, lg, lb, alpha, tm=1024):
    T, D = x.shape
    const = lambda i, e: (0, 0)
    full = lambda a: pl.BlockSpec(a.shape, const)
    kern = functools.partial(_moe_kernel, alpha=alpha)
    return pl.pallas_call(
        kern, grid=(T // tm, N_EXPERTS),
        in_specs=[pl.BlockSpec((tm, D), lambda i, e: (i, 0)), full(wrh), full(wrl), full(br),
                  pl.BlockSpec((None, D, D_EXPERT), lambda i, e: (e, 0, 0)),
                  pl.BlockSpec((None, D, D_EXPERT), lambda i, e: (e, 0, 0)),
                  pl.BlockSpec((None, D_EXPERT, D), lambda i, e: (e, 0, 0)),
                  full(lg), full(lb)],
        out_specs=pl.BlockSpec((tm, D), lambda i, e: (i, 0)),
        out_shape=jax.ShapeDtypeStruct((T, D), F32),
        scratch_shapes=[pltpu.VMEM((tm, D), BF16), pltpu.VMEM((tm, LANES), F32), pltpu.VMEM((tm, D), F32)],
        compiler_params=_params("parallel", "arbitrary"), name="moe")(x, wrh, wrl, br, w1, w3, w2, lg, lb)


def _rope_tables(seq):
    pos = jnp.arange(seq, dtype=F32)[:, None]
    one = lambda n: jnp.ones((seq, n), F32)
    zero = lambda n: jnp.zeros((seq, n), F32)

    inv_p = ROPE_THETA ** (-jnp.arange(0, ROT_DIM, 2, dtype=F32) / ROT_DIM)
    cp, sp = jnp.cos(pos * inv_p), jnp.sin(pos * inv_p)
    hp = ROT_DIM // 2
    rest = HEAD_DIM - ROT_DIM
    head = (jnp.concatenate([cp, cp, one(rest)], 1),
            jnp.concatenate([zero(hp), sp, zero(rest)], 1),
            jnp.concatenate([-sp, zero(hp), zero(rest)], 1))
    ident = (one(HEAD_DIM), zero(HEAD_DIM), zero(HEAD_DIM))
    both = jnp.stack([jnp.concatenate([a, a], 1) for a in head])
    first = jnp.stack([jnp.concatenate([a, b], 1) for a, b in zip(head, ident)])
    tab_p = jnp.stack([both, first])

    inv_m = ROPE_THETA ** (-jnp.arange(0, MLA_ROPE, 2, dtype=F32) / MLA_ROPE)
    cm, sm = jnp.cos(pos * inv_m), jnp.sin(pos * inv_m)
    hm = MLA_ROPE // 2
    pad = LANES - MLA_NOPE - MLA_ROPE
    tab_m = jnp.stack([jnp.concatenate([one(MLA_NOPE), cm, cm, one(pad)], 1),
                       jnp.concatenate([zero(MLA_NOPE), zero(hm), sm, zero(pad)], 1),
                       jnp.concatenate([zero(MLA_NOPE), -sm, zero(hm), zero(pad)], 1)])[None]
    return tab_p, tab_m


def _layer_weights(p, l):
    w_in = p["w_in"][l]
    D = w_in.shape[0]
    z = lambda n: jnp.zeros((D, n), F32)
    o = 0
    cuts = []
    for n in (MLA_Q_RANK, MLA_KV_RANK, MLA_ROPE, 3 * 3 * DIL_HEADS * HEAD_DIM, DSA_HEADS * HEAD_DIM,
              HEAD_DIM, HEAD_DIM, IDX_HEADS * IDX_DIM, IDX_DIM, IDX_HEADS):
        cuts.append(w_in[:, o:o + n])
        o += n
    w_cq, w_ckv, w_kr, w_dil, w_q, w_k, w_v, w_iq, w_ik, w_iw = cuts
    qs = HEAD_DIM ** -0.5
    w_mla = jnp.concatenate([w_cq, w_ckv, z(MLA_NOPE), w_kr, z(LANES - MLA_NOPE - MLA_ROPE)], 1)
    HW = DIL_HEADS * HEAD_DIM
    dil_scale = jnp.tile(jnp.concatenate([jnp.full((HW,), qs, F32), jnp.ones((2 * HW,), F32)]), 3)
    w_dil = w_dil * dil_scale[None, :]
    w_dsa = jnp.concatenate([w_q * qs, w_k, w_v, w_iq, w_ik, w_iw, z(LANES - IDX_DIM - IDX_HEADS)], 1)

    def per_head(w, n_in, n_keep_lo, n_keep_hi):
        r = w.shape[0]
        w = w.reshape(r, MLA_HEADS, n_in)[:, :, n_keep_lo:n_keep_hi]
        w = jnp.pad(w, ((0, 0), (0, 0), (0, LANES - (n_keep_hi - n_keep_lo))))
        return w.reshape(r, MLA_HEADS * LANES)

    w_uq = per_head(p["w_uq"][l], MLA_NOPE + MLA_ROPE, 0, MLA_NOPE + MLA_ROPE)
    w_uk = per_head(p["w_ukv"][l], MLA_NOPE + MLA_V, 0, MLA_NOPE)
    w_uv = p["w_ukv"][l].reshape(MLA_KV_RANK, MLA_HEADS, MLA_NOPE + MLA_V)[:, :, MLA_NOPE:]
    w_uv = w_uv.reshape(MLA_KV_RANK, MLA_HEADS * MLA_V).T

    w_r = jnp.concatenate([p["w_group"][l], p["w_sub"][l], z(LANES - N_GROUPS - N_EXPERTS)], 1)
    w_r_hi = w_r.astype(BF16)
    w_r_lo = (w_r - w_r_hi.astype(F32)).astype(BF16)
    b_r = jnp.concatenate([p["b_group"][l], p["b_sub"][l], jnp.zeros((LANES - N_GROUPS - N_EXPERTS,), F32)])

    bf = lambda a: a.astype(BF16)
    return dict(
        w_mla=bf(w_mla), w_dil=bf(w_dil), w_dsa=bf(w_dsa),
        q_g=p["q_norm_g"][l][None], kv_g=p["kv_norm_g"][l][None],
        w_uq=bf(w_uq), w_uk=bf(w_uk), w_uv=bf(w_uv),
        w_gate=bf(p["w_gate"][l]), b_gate=p["b_gate"][l][None],
        w_a=bf(p["w_a"][l]), w_b=bf(p["w_b"][l]), w_c=bf(p["w_c"][l]), w_o=bf(p["w_o"][l]),
        ln1_g=p["ln1_g"][l][None], ln1_b=p["ln1_b"][l][None],
        w_r_hi=w_r_hi, w_r_lo=w_r_lo, b_r=b_r[None],
        w1=bf(p["w1"][l]), w3=bf(p["w3"][l]), w2=bf(p["w2"][l]),
        ln2_g=p["ln2_g"][l][None], ln2_b=p["ln2_b"][l][None])


def _layer(xt, w, tab_p, tab_m, batch, seq, alpha):
    HW = DIL_HEADS * HEAD_DIM
    (grp,) = _proj(xt, w["w_mla"], seq, tn=w["w_mla"].shape[1], segs=((0, w["w_mla"].shape[1]),),
                   dtypes=(F32,), kinds=(-1,) * (w["w_mla"].shape[1] // LANES))
    (dil,) = _proj(xt, w["w_dil"], seq, tn=3 * HW, segs=((0, 3 * HW),), dtypes=(F32,),
                   kinds=(0, 0, 0, 0, -1, -1), tab=tab_p)
    qw = DSA_HEADS * HEAD_DIM
    iw_ = IDX_HEADS * IDX_DIM
    q_c, kv_c, iq_c, ikw_c = _proj(
        xt, w["w_dsa"], seq, tn=w["w_dsa"].shape[1],
        segs=((0, qw), (qw, LANES), (qw + LANES, iw_), (qw + LANES + iw_, LANES)),
        dtypes=(BF16, BF16, BF16, F32),
        kinds=(0,) * (qw // LANES) + (1,) + (0,) * (iw_ // LANES) + (1,), tab=tab_p)

    q_a, k_a, v_a = _mla_up(grp, w["q_g"], w["kv_g"], w["w_uq"], w["w_uk"], w["w_uv"], tab_m, seq)
    o_a = _mla_attn(q_a, k_a, v_a, batch, seq)
    o_b = _dil_attn(dil, batch, seq)
    o_c = _dsa_attn(q_c, kv_c, iq_c, ikw_c, batch, seq)

    x1 = _mix_out(xt, o_a, o_b, o_c, w["w_gate"], w["b_gate"], w["w_a"], w["w_b"], w["w_c"], w["w_o"],
                  w["ln1_g"], w["ln1_b"], alpha)
    return _moe(x1, w["w_r_hi"], w["w_r_lo"], w["b_r"], w["w1"], w["w3"], w["w2"],
                w["ln2_g"], w["ln2_b"], alpha)


def kernel(x, w_in, q_norm_g, w_uq, kv_norm_g, w_ukv, w_gate, b_gate, w_a, w_b, w_c, w_o, ln1_g, ln1_b,
           w_group, b_group, w_sub, b_sub, w1, w3, w2, ln2_g, ln2_b):
    batch, seq, d_model = x.shape
    depth = w_in.shape[0]
    alpha = (2 * depth) ** 0.25
    p = dict(w_in=w_in, q_norm_g=q_norm_g, w_uq=w_uq, kv_norm_g=kv_norm_g, w_ukv=w_ukv, w_gate=w_gate,
             b_gate=b_gate, w_a=w_a, w_b=w_b, w_c=w_c, w_o=w_o, ln1_g=ln1_g, ln1_b=ln1_b, w_group=w_group,
             b_group=b_group, w_sub=w_sub, b_sub=b_sub, w1=w1, w3=w3, w2=w2, ln2_g=ln2_g, ln2_b=ln2_b)
    tab_p, tab_m = _rope_tables(seq)
    xt = x.reshape(batch * seq, d_model)
    for l in range(depth):
        xt = _layer(xt, _layer_weights(p, l), tab_p, tab_m, batch, seq, alpha)
    return xt.reshape(batch, seq, d_model)
```

```python
import functools

import jax
import jax.numpy as jnp
import numpy as np
from jax import lax
from jax.experimental import pallas as pl
from jax.experimental.pallas import tpu as pltpu

F32 = jnp.float32
BF16 = jnp.bfloat16
I32 = jnp.int32

HEAD_DIM = 64
ROT_DIM = HEAD_DIM // 4
ROPE_THETA = 500000.0
EPS = 1e-6
MLA_HEADS = 8
MLA_Q_RANK = 256
MLA_KV_RANK = 128
MLA_NOPE = 64
MLA_ROPE = 32
MLA_V = 64
DIL_GROUPS = ((128, 1), (512, 4), (2048, 16))
DIL_HEADS = 4
DSA_HEADS = 8
IDX_HEADS = 8
IDX_DIM = 64
TOPK_MAX = 256
N_GROUPS = 4
EXPERTS_PER_GROUP = 8
N_EXPERTS = N_GROUPS * EXPERTS_PER_GROUP
D_EXPERT = 256
N_BRANCH = 3

LANES = 128
VMEM_LIMIT = 56 * 1024 * 1024
NEG_INF = float("-inf")
INT_MIN = -2 ** 31
M_FLOOR = -1e30
LOG2E = 1.4426950408889634


def _params(*sem):
    return pltpu.CompilerParams(dimension_semantics=sem, vmem_limit_bytes=VMEM_LIMIT)


def _dot(a, b):
    return jnp.dot(a, b, preferred_element_type=F32)


def _dot_t(a, b):
    return lax.dot_general(a, b, (((1,), (1,)), ((), ())), preferred_element_type=F32)


def _rope_chunk(v, tab_ref, kind, shift):
    c = tab_ref[kind, 0]
    s1 = tab_ref[kind, 1]
    s2 = tab_ref[kind, 2]
    return v * c + pltpu.roll(v, shift, 1) * s1 + pltpu.roll(v, LANES - shift, 1) * s2


def _layer_norm(z, g, b):
    mu = jnp.mean(z, axis=-1, keepdims=True)
    zc = z - mu
    var = jnp.mean(zc * zc, axis=-1, keepdims=True)
    return zc * lax.rsqrt(var + EPS) * g + b


def _proj_kernel(*refs, segs, kinds, shift, has_tab):
    x_ref, w_ref = refs[0], refs[1]
    tab_ref = refs[2] if has_tab else None
    outs = refs[3:] if has_tab else refs[2:]
    acc = _dot(x_ref[...].astype(BF16), w_ref[...])
    for (c0, width), o_ref in zip(segs, outs):
        for c in range(width // LANES):
            cc = c0 // LANES + c
            v = acc[:, cc * LANES:(cc + 1) * LANES]
            if kinds[cc] >= 0:
                v = _rope_chunk(v, tab_ref, kinds[cc], shift)
            o_ref[:, c * LANES:(c + 1) * LANES] = v.astype(o_ref.dtype)


def _proj(x, w, seq, *, tn, segs, dtypes, kinds, tab=None, shift=ROT_DIM // 2, tm=512):
    T, K = x.shape
    N = w.shape[1]
    nj = N // tn
    spt = seq // tm
    in_specs = [pl.BlockSpec((tm, K), lambda i, j: (i, 0)),
                pl.BlockSpec((K, tn), lambda i, j: (0, j))]
    args = [x, w]
    if tab is not None:
        in_specs.append(pl.BlockSpec((tab.shape[0], 3, tm, LANES), lambda i, j: (0, 0, i % spt, 0)))
        args.append(tab)
    out_specs = [pl.BlockSpec((tm, wd), lambda i, j: (i, j)) for (_, wd) in segs]
    out_shape = [jax.ShapeDtypeStruct((T, wd * nj), dt) for (_, wd), dt in zip(segs, dtypes)]
    kern = functools.partial(_proj_kernel, segs=segs, kinds=kinds, shift=shift, has_tab=tab is not None)
    return pl.pallas_call(kern, grid=(T // tm, nj), in_specs=in_specs, out_specs=out_specs,
                          out_shape=out_shape, compiler_params=_params("parallel", "arbitrary"),
                          name="in_proj")(*args)


def _mla_up_kernel(g_ref, qg_ref, kvg_ref, wq_ref, wk_ref, wv_ref, tab_ref, q_out, k_out, v_out):
    g = g_ref[...]
    cq = g[:, :MLA_Q_RANK]
    ckv = g[:, MLA_Q_RANK:MLA_Q_RANK + MLA_KV_RANK]
    kr = g[:, MLA_Q_RANK + MLA_KV_RANK:]
    cqn = (cq * lax.rsqrt(jnp.mean(cq * cq, axis=-1, keepdims=True) + EPS) * qg_ref[...]).astype(BF16)
    ckvn = (ckv * lax.rsqrt(jnp.mean(ckv * ckv, axis=-1, keepdims=True) + EPS) * kvg_ref[...]).astype(BF16)
    q = _dot(cqn, wq_ref[...])
    k = _dot(ckvn, wk_ref[...])
    v_out[...] = _dot_t(wv_ref[...], ckvn).astype(v_out.dtype)
    for h in range(MLA_HEADS):
        sl = slice(h * LANES, (h + 1) * LANES)
        q_out[:, sl] = _rope_chunk(q[:, sl], tab_ref, 0, MLA_ROPE // 2).astype(q_out.dtype)
        k_out[:, sl] = _rope_chunk(k[:, sl] + kr, tab_ref, 0, MLA_ROPE // 2).astype(k_out.dtype)


def _mla_up(grp, qg, kvg, wq, wk, wv, tab, seq, tm=512):
    T = grp.shape[0]
    spt = seq // tm
    const = lambda i: (0, 0)
    return pl.pallas_call(
        _mla_up_kernel, grid=(T // tm,),
        in_specs=[pl.BlockSpec((tm, grp.shape[1]), lambda i: (i, 0)),
                  pl.BlockSpec(qg.shape, const), pl.BlockSpec(kvg.shape, const),
                  pl.BlockSpec(wq.shape, const), pl.BlockSpec(wk.shape, const), pl.BlockSpec(wv.shape, const),
                  pl.BlockSpec((1, 3, tm, LANES), lambda i: (0, 0, i % spt, 0))],
        out_specs=[pl.BlockSpec((tm, MLA_HEADS * LANES), lambda i: (i, 0)),
                   pl.BlockSpec((tm, MLA_HEADS * LANES), lambda i: (i, 0)),
                   pl.BlockSpec((MLA_HEADS * MLA_V, tm), lambda i: (0, i))],
        out_shape=[jax.ShapeDtypeStruct((T, MLA_HEADS * LANES), BF16),
                   jax.ShapeDtypeStruct((T, MLA_HEADS * LANES), BF16),
                   jax.ShapeDtypeStruct((MLA_HEADS * MLA_V, T), BF16)],
        compiler_params=_params("parallel"), name="mla_up")(grp, qg, kvg, wq, wk, wv, tab)


def _mla_attn_kernel(q_ref, k_ref, vt_ref, o_ref, acc_s, s_s, p_s, *, tq, c_exp):
    qi = pl.program_id(1)
    krow = lax.broadcasted_iota(I32, (tq, tq), 0)
    qcol = lax.broadcasted_iota(I32, (tq, tq), 1)
    diag_bias = jnp.where(krow <= qcol, 0.0, NEG_INF).astype(F32)
    acc_s[...] = jnp.zeros_like(acc_s)

    def step(j, carry, masked):
        ms, ls = carry
        ks = pl.multiple_of(j * tq, tq)
        for h in range(MLA_HEADS):
            kb = k_ref[pl.ds(ks, tq), h * LANES:(h + 1) * LANES]
            s_s[h] = _dot_t(kb, q_ref[:, h * LANES:(h + 1) * LANES])
        new_ms, new_ls, alphas = [], [], []
        for h in range(MLA_HEADS):
            s = s_s[h]
            if masked:
                s = s + diag_bias
            m_new = jnp.maximum(ms[h], jnp.max(s, axis=0, keepdims=True))
            alpha = jnp.exp2((ms[h] - m_new) * c_exp)
            p = jnp.exp2((s - m_new) * c_exp)
            new_ls.append(alpha * ls[h] + jnp.sum(p, axis=0, keepdims=True))
            new_ms.append(m_new)
            alphas.append(alpha)
            p_s[h] = p.astype(BF16)
        for h in range(MLA_HEADS):
            sl = slice(h * MLA_V, (h + 1) * MLA_V)
            acc_s[sl, :] = alphas[h] * acc_s[sl, :] + _dot(vt_ref[sl, pl.ds(ks, tq)], p_s[h])
        return tuple(new_ms), tuple(new_ls)

    init = (tuple(jnp.full((1, tq), NEG_INF, F32) for _ in range(MLA_HEADS)),
            tuple(jnp.zeros((1, tq), F32) for _ in range(MLA_HEADS)))
    carry = lax.fori_loop(0, qi, functools.partial(step, masked=False), init)
    _, ls = step(qi, carry, True)
    for h in range(MLA_HEADS):
        sl = slice(h * MLA_V, (h + 1) * MLA_V)
        acc_s[sl, :] = acc_s[sl, :] / ls[h]
    o_ref[...] = acc_s[...].T.astype(o_ref.dtype)


def _mla_attn(q, k, vt, batch, seq, tq=256):
    T = q.shape[0]
    nq = seq // tq
    c_exp = (MLA_NOPE + MLA_ROPE) ** -0.5 * LOG2E
    kern = functools.partial(_mla_attn_kernel, tq=tq, c_exp=c_exp)
    W = MLA_HEADS * LANES
    return pl.pallas_call(
        kern, grid=(batch, nq),
        in_specs=[pl.BlockSpec((tq, W), lambda b, i: (b * nq + i, 0)),
                  pl.BlockSpec((seq, W), lambda b, i: (b, 0)),
                  pl.BlockSpec((MLA_HEADS * MLA_V, seq), lambda b, i: (0, b))],
        out_specs=pl.BlockSpec((tq, MLA_HEADS * MLA_V), lambda b, i: (b * nq + i, 0)),
        out_shape=jax.ShapeDtypeStruct((T, MLA_HEADS * MLA_V), BF16),
        scratch_shapes=[pltpu.VMEM((MLA_HEADS * MLA_V, tq), F32),
                        pltpu.VMEM((MLA_HEADS, tq, tq), F32), pltpu.VMEM((MLA_HEADS, tq, tq), BF16)],
        compiler_params=_params("parallel", "arbitrary"), name="mla_attn")(q, k, vt)


def _dil_bias(tq, nq):
    i = np.arange(tq)[None, :]
    j = np.arange(tq)[:, None]
    tiles, ndelta = [], []
    for window, d in DIL_GROUPS:
        assert tq % d == 0
        nd = min((window + tq - 1) // tq + 1, nq)
        for dl in range(nd):
            dist = tq * dl + i - j
            ok = (dist >= 0) & (dist <= window) & (dist % d == 0)
            tiles.append(np.where(ok, 0.0, -np.inf))
        ndelta.append(nd)
    return jnp.asarray(np.stack(tiles), F32), tuple(ndelta)


def _dil_kernel(q_ref, k_ref, vt_ref, bias_ref, o_ref, acc_s, s_s, p_s, *, tq, ndelta):
    qi = pl.program_id(1)
    HW = DIL_HEADS * HEAD_DIM
    acc_s[...] = jnp.zeros_like(acc_s)
    carry = (tuple(jnp.full((1, tq), NEG_INF, F32) for _ in range(DIL_HEADS)),
             tuple(jnp.zeros((1, tq), F32) for _ in range(DIL_HEADS)))
    base = 0
    for g, nd in enumerate(ndelta):
        qc = g * 2 * HW

        def step(dl, carry, g=g, qc=qc, base=base):
            ms, ls = carry
            ks = pl.multiple_of((qi - dl) * tq, tq)
            bias = bias_ref[base + dl]
            for h in range(DIL_HEADS):
                kh = k_ref[pl.ds(ks, tq), qc + HW + h * HEAD_DIM:qc + HW + (h + 1) * HEAD_DIM]
                s_s[h] = _dot_t(kh, q_ref[:, qc + h * HEAD_DIM:qc + (h + 1) * HEAD_DIM])
            new_ms, new_ls, alphas = [], [], []
            for h in range(DIL_HEADS):
                s = s_s[h] + bias
                m_new = jnp.maximum(ms[h], jnp.max(s, axis=0, keepdims=True))
                alpha = jnp.exp(ms[h] - m_new)
                p = jnp.exp(s - m_new)
                new_ls.append(alpha * ls[h] + jnp.sum(p, axis=0, keepdims=True))
                new_ms.append(m_new)
                alphas.append(alpha)
                p_s[h] = p.astype(BF16)
            for h in range(DIL_HEADS):
                sl = slice(h * HEAD_DIM, (h + 1) * HEAD_DIM)
                vt = vt_ref[g * HW + h * HEAD_DIM:g * HW + (h + 1) * HEAD_DIM, pl.ds(ks, tq)]
                acc_s[sl, :] = alphas[h] * acc_s[sl, :] + _dot(vt, p_s[h])
            return tuple(new_ms), tuple(new_ls)

        carry = lax.fori_loop(0, jnp.minimum(nd, qi + 1), step, carry)
        base += nd
    _, ls = carry
    for h in range(DIL_HEADS):
        sl = slice(h * HEAD_DIM, (h + 1) * HEAD_DIM)
        acc_s[sl, :] = acc_s[sl, :] / ls[h]
    o_ref[...] = acc_s[...].T.astype(o_ref.dtype)


def _dil_attn(qk, vt, batch, seq, tq=256):
    T = qk.shape[0]
    nq = seq // tq
    HW = DIL_HEADS * HEAD_DIM
    bias, ndelta = _dil_bias(tq, nq)
    kern = functools.partial(_dil_kernel, tq=tq, ndelta=ndelta)
    return pl.pallas_call(
        kern, grid=(batch, nq),
        in_specs=[pl.BlockSpec((tq, qk.shape[1]), lambda b, i: (b * nq + i, 0)),
                  pl.BlockSpec((seq, qk.shape[1]), lambda b, i: (b, 0)),
                  pl.BlockSpec((vt.shape[0], seq), lambda b, i: (0, b)),
                  pl.BlockSpec(bias.shape, lambda b, i: (0, 0, 0))],
        out_specs=pl.BlockSpec((tq, HW), lambda b, i: (b * nq + i, 0)),
        out_shape=jax.ShapeDtypeStruct((T, HW), BF16),
        scratch_shapes=[pltpu.VMEM((HW, tq), F32),
                        pltpu.VMEM((DIL_HEADS, tq, tq), F32), pltpu.VMEM((DIL_HEADS, tq, tq), BF16)],
        compiler_params=_params("parallel", "arbitrary"), name="dil_attn")(qk, qk, vt, bias)


def _proj_t_kernel(x_ref, wt_ref, o_ref):
    o_ref[...] = _dot_t(wt_ref[...], x_ref[...].astype(BF16)).astype(o_ref.dtype)


def _proj_t(x, wt, tm=512):
    T, K = x.shape
    N = wt.shape[0]
    return pl.pallas_call(
        _proj_t_kernel, grid=(T // tm,),
        in_specs=[pl.BlockSpec((tm, K), lambda i: (i, 0)), pl.BlockSpec((N, K), lambda i: (0, 0))],
        out_specs=pl.BlockSpec((N, tm), lambda i: (0, i)),
        out_shape=jax.ShapeDtypeStruct((N, T), BF16),
        compiler_params=_params("parallel"), name="in_proj_t")(x, wt)


def _dsa_kernel(q_ref, iq_ref, kv_ref, ikw_ref, o_ref, qall_s, iqall_s, kvt_s, key_s, bias_s, j_s, acc_s, p_s,
                *, tq, ck, top, seq):
    qi = pl.program_id(1)
    lo = qi * tq
    nch = (lo + tq + ck - 1) // ck
    krow = lax.broadcasted_iota(I32, (ck, tq), 0)
    qcol = lo + lax.broadcasted_iota(I32, (ck, tq), 1)

    def chunk(c):
        return pl.ds(pl.multiple_of(c * ck, ck), ck)

    @pl.when(qi == 0)
    def _():
        def body(c, _):
            kvt_s[:, chunk(c)] = kv_ref[chunk(c), :].astype(F32).T.astype(BF16)
            return 0
        lax.fori_loop(0, seq // ck, body, 0)

    for h in range(DSA_HEADS):
        qall_s[h * tq:(h + 1) * tq, :] = q_ref[:, h * HEAD_DIM:(h + 1) * HEAD_DIM]
    for h in range(IDX_HEADS):
        iqall_s[h * tq:(h + 1) * tq, :] = iq_ref[:, h * IDX_DIM:(h + 1) * IDX_DIM]

    @pl.when(lo + tq <= top)
    def _():
        def body(c, _):
            bias_s[chunk(c), :] = jnp.where(c * ck + krow <= qcol, 0.0, NEG_INF).astype(F32)
            return 0
        lax.fori_loop(0, nch, body, 0)

    @pl.when(lo + tq > top)
    def _():
        iw_t = ikw_ref[pl.ds(pl.multiple_of(lo, tq), tq), :].T[IDX_DIM:IDX_DIM + IDX_HEADS, :]
        iw_t = iw_t * (IDX_HEADS ** -0.5 * IDX_DIM ** -0.5)

        def score_body(c, _):
            ik = ikw_ref[chunk(c), 0:IDX_DIM].astype(BF16)
            r = _dot_t(ik, iqall_s[...])
            sc = jnp.zeros((ck, tq), F32)
            for h in range(IDX_HEADS):
                sc = sc + iw_t[h:h + 1, :] * jnp.maximum(r[:, h * tq:(h + 1) * tq], 0.0)
            sc = jnp.where(c * ck + krow <= qcol, sc, NEG_INF)
            bits = lax.bitcast_convert_type(sc, I32)
            key_s[chunk(c), :] = bits ^ ((bits >> 31) & 0x7FFFFFFF)
            return 0
        lax.fori_loop(0, nch, score_body, 0)

        def count(pred):
            def body(c, acc):
                ind = pred(key_s[chunk(c), :], c)
                return acc + jnp.sum(ind.reshape(ck // 64, 64, tq), axis=0)
            acc = lax.fori_loop(0, nch, body, jnp.zeros((64, tq), F32))
            return jnp.sum(acc, axis=0, keepdims=True)

        def count_ge(cand):
            return count(lambda kc, c: jnp.where(kc >= cand, 1.0, 0.0))

        topf = float(top)
        tau = jnp.where(count_ge(jnp.zeros((1, tq), I32)) >= topf, 0, INT_MIN).astype(I32)

        def search(i, tau):
            cand = tau | jnp.left_shift(jnp.int32(1), 30 - i)
            return jnp.where(count_ge(cand) >= topf, cand, tau)
        tau = lax.fori_loop(0, 31, search, tau)

        need = topf - count_ge(tau + 1)
        j_s[...] = jnp.full(j_s.shape, seq, I32)

        @pl.when(jnp.max(count_ge(tau)) > topf)
        def _():
            def count_eq_lt(J):
                return count(lambda kc, c: jnp.where(kc == tau, jnp.where(c * ck + krow < J, 1.0, 0.0), 0.0))

            nbits = seq.bit_length()

            def tie(i, J):
                cand = J + jnp.left_shift(jnp.int32(1), nbits - 1 - i)
                return jnp.where(count_eq_lt(cand) <= need, cand, J)
            J = lax.fori_loop(0, nbits, tie, jnp.zeros((1, tq), I32))
            j_s[...] = jnp.broadcast_to(J, j_s.shape)

        J = j_s[0:1, :]

        def bias_body(c, _):
            kc = key_s[chunk(c), :]
            sel = jnp.where(kc > tau, 0.0,
                            jnp.where(kc == tau, jnp.where(c * ck + krow < J, 0.0, NEG_INF), NEG_INF))
            bias_s[chunk(c), :] = sel.astype(F32)
            return 0
        lax.fori_loop(0, nch, bias_body, 0)

    acc_s[...] = jnp.zeros_like(acc_s)

    def step(c, carry):
        ms, ls = carry
        kc = kv_ref[chunk(c), 0:HEAD_DIM]
        bias = bias_s[chunk(c), :]
        vt = kvt_s[HEAD_DIM:2 * HEAD_DIM, chunk(c)]
        new_ms, new_ls, alphas = [], [], []
        for h in range(DSA_HEADS):
            if h % 2 == 0:
                s_pair = _dot_t(kc, qall_s[h * tq:(h + 2) * tq, :])
            s = s_pair[:, (h % 2) * tq:(h % 2 + 1) * tq] + bias
            m_new = jnp.maximum(ms[h], jnp.max(s, axis=0, keepdims=True))
            alpha = jnp.exp(ms[h] - m_new)
            p = jnp.exp(s - m_new)
            new_ls.append(alpha * ls[h] + jnp.sum(p, axis=0, keepdims=True))
            new_ms.append(m_new)
            alphas.append(alpha)
            p_s[h] = p.astype(BF16)
        for h in range(DSA_HEADS):
            sl = slice(h * HEAD_DIM, (h + 1) * HEAD_DIM)
            acc_s[sl, :] = alphas[h] * acc_s[sl, :] + _dot(vt, p_s[h])
        return tuple(new_ms), tuple(new_ls)

    init = (tuple(jnp.full((1, tq), M_FLOOR, F32) for _ in range(DSA_HEADS)),
            tuple(jnp.zeros((1, tq), F32) for _ in range(DSA_HEADS)))
    _, ls = lax.fori_loop(0, nch, step, init)
    for h in range(DSA_HEADS):
        sl = slice(h * HEAD_DIM, (h + 1) * HEAD_DIM)
        acc_s[sl, :] = acc_s[sl, :] / ls[h]
    o_ref[...] = acc_s[...].T.astype(o_ref.dtype)


def _dsa_attn(q, kv, iq, ikw, batch, seq, tq=LANES, ck=256):
    T = q.shape[0]
    nq = seq // tq
    top = min(TOPK_MAX, seq // 4)
    assert tq == LANES and top % tq == 0 and seq % ck == 0
    kern = functools.partial(_dsa_kernel, tq=tq, ck=ck, top=top, seq=seq)
    W = DSA_HEADS * HEAD_DIM
    return pl.pallas_call(
        kern, grid=(batch, nq),
        in_specs=[pl.BlockSpec((tq, W), lambda b, i: (b * nq + i, 0)),
                  pl.BlockSpec((tq, IDX_HEADS * IDX_DIM), lambda b, i: (b * nq + i, 0)),
                  pl.BlockSpec((seq, LANES), lambda b, i: (b, 0)),
                  pl.BlockSpec((seq, LANES), lambda b, i: (b, 0))],
        out_specs=pl.BlockSpec((tq, W), lambda b, i: (b * nq + i, 0)),
        out_shape=jax.ShapeDtypeStruct((T, W), BF16),
        scratch_shapes=[pltpu.VMEM((DSA_HEADS * tq, HEAD_DIM), BF16),
                        pltpu.VMEM((IDX_HEADS * tq, IDX_DIM), BF16),
                        pltpu.VMEM((LANES, seq), BF16),
                        pltpu.VMEM((seq, tq), I32),
                        pltpu.VMEM((seq, tq), F32),
                        pltpu.VMEM((8, tq), I32),
                        pltpu.VMEM((W, tq), F32),
                        pltpu.VMEM((DSA_HEADS, ck, tq), BF16)],
        compiler_params=_params("arbitrary", "arbitrary"), name="dsa_attn")(q, iq, kv, ikw)


def _mix_out_kernel(x_ref, oa_ref, ob_ref, oc_ref, wg_ref, bg_ref, wa_ref, wb_ref, wc_ref, wo_ref,
                    lg_ref, lb_ref, o_ref, *, alpha, d_model):
    x = x_ref[...]
    xb = x.astype(BF16)
    merged = None
    for br, (o_r, w_r) in enumerate(((oa_ref, wa_ref), (ob_ref, wb_ref), (oc_ref, wc_ref))):
        sl = slice(br * d_model, (br + 1) * d_model)
        gate = jax.nn.sigmoid(_dot(xb, wg_ref[:, sl]) + bg_ref[:, sl])
        term = gate * _dot(o_r[...], w_r[...])
        merged = term if merged is None else merged + term
    mix = _dot(merged.astype(BF16), wo_ref[...])
    o_ref[...] = _layer_norm(alpha * x + mix, lg_ref[...], lb_ref[...])


def _mix_out(x, oa, ob, oc, wg, bg, wa, wb, wc, wo, lg, lb, alpha, tm=256):
    T, D = x.shape
    const = lambda i: (0, 0)
    full = lambda a: pl.BlockSpec(a.shape, const)
    row = lambda a: pl.BlockSpec((tm, a.shape[1]), lambda i: (i, 0))
    kern = functools.partial(_mix_out_kernel, alpha=alpha, d_model=D)
    return pl.pallas_call(
        kern, grid=(T // tm,),
        in_specs=[row(x), row(oa), row(ob), row(oc), full(wg), full(bg), full(wa), full(wb), full(wc),
                  full(wo), full(lg), full(lb)],
        out_specs=pl.BlockSpec((tm, D), lambda i: (i, 0)),
        out_shape=jax.ShapeDtypeStruct((T, D), F32),
        compiler_params=_params("parallel"), name="mix_out")(x, oa, ob, oc, wg, bg, wa, wb, wc, wo, lg, lb)


def _route(x, wr_hi_ref, wr_lo_ref, br_ref):
    xh = x.astype(BF16)
    xl = (x - xh.astype(F32)).astype(BF16)
    logits = _dot(xh, wr_hi_ref[...]) + _dot(xl, wr_hi_ref[...]) + _dot(xh, wr_lo_ref[...]) + br_ref[...]
    lane = lax.broadcasted_iota(I32, logits.shape, 1).astype(F32)
    none = float(LANES)
    glog = jnp.where(lane < N_GROUPS, logits, NEG_INF)
    gmax = jnp.max(glog, axis=1, keepdims=True)
    g_p = 1.0 / jnp.sum(jnp.exp(glog - gmax), axis=1, keepdims=True)
    g_idx = jnp.min(jnp.where(glog == gmax, lane, none), axis=1, keepdims=True)
    first = N_GROUPS + g_idx * EXPERTS_PER_GROUP
    sub = jnp.where(lane >= first, jnp.where(lane < first + EXPERTS_PER_GROUP, logits, NEG_INF), NEG_INF)
    v1 = jnp.max(sub, axis=1, keepdims=True)
    i1 = jnp.min(jnp.where(sub == v1, lane, none), axis=1, keepdims=True)
    sub2 = jnp.where(lane == i1, NEG_INF, sub)
    v2 = jnp.max(sub2, axis=1, keepdims=True)
    i2 = jnp.min(jnp.where(sub2 == v2, lane, none), axis=1, keepdims=True)
    e2 = jnp.exp(v2 - v1)
    w1 = g_p / (1.0 + e2)
    w2 = g_p * e2 / (1.0 + e2)
    return jnp.where(lane == i1, w1, jnp.where(lane == i2, w2, 0.0)), g_idx


def _split3(v):
    hi = v.astype(BF16)
    r = v - hi.astype(F32)
    mid = r.astype(BF16)
    return hi, mid, (r - mid.astype(F32)).astype(BF16)


def _moe_kernel(x_ref, wrh_ref, wrl_ref, br_ref, w1_ref, w3_ref, w2_ref, lg_ref, lb_ref, o_ref,
                xs_s, combs_s, pt_s, acc_s, rng_s, *, alpha, tm, rb):
    e = pl.program_id(1)

    @pl.when(e == 0)
    def _():
        x = x_ref[...]
        comb, g_idx = _route(x, wrh_ref, wrl_ref, br_ref)
        lane = lax.broadcasted_iota(I32, (tm, LANES), 1).astype(F32)
        onehot_g = jnp.where(lane == g_idx, 1.0, 0.0)
        ti = lax.broadcasted_iota(I32, (tm, tm), 0)
        tj = lax.broadcasted_iota(I32, (tm, tm), 1)
        earlier = jnp.where(tj < ti, 1.0, 0.0).astype(BF16)
        rank = _dot(earlier, onehot_g.astype(BF16))
        counts = jnp.sum(onehot_g, axis=0, keepdims=True)
        lane1 = lax.broadcasted_iota(I32, (1, LANES), 1)
        off = jnp.zeros((1, LANES), F32)
        for k in range(1, N_GROUPS):
            off = off + jnp.where(lane1 >= k, pltpu.roll(counts, k, 1), 0.0)
        pos = jnp.sum(onehot_g * (off + rank), axis=1, keepdims=True)
        pos_row = jnp.broadcast_to(pos, (tm, LANES)).T[0:1, :]
        pt_s[...] = jnp.where(tj.astype(F32) == pos, 1.0, 0.0).astype(BF16)
        perm = jnp.where(ti.astype(F32) == pos_row, 1.0, 0.0).astype(BF16)
        xs_s[...] = _dot(perm, x.astype(BF16)).astype(BF16)
        c_hi, c_mid, c_lo = _split3(comb)
        combs_s[...] = _dot(perm, c_hi) + _dot(perm, c_mid) + _dot(perm, c_lo)
        acc_s[...] = jnp.zeros_like(acc_s)
        for g in range(N_GROUPS):
            start = jnp.sum(jnp.where(lane1 == g, off, 0.0)).astype(I32)
            cnt = jnp.sum(jnp.where(lane1 == g, counts, 0.0)).astype(I32)
            rng_s[g] = start // rb
            rng_s[N_GROUPS + g] = (start + cnt + rb - 1) // rb

    g = e // EXPERTS_PER_GROUP
    lane_b = lax.broadcasted_iota(I32, (rb, LANES), 1)

    def block(b, _):
        rows = pl.ds(pl.multiple_of(b * rb, rb), rb)
        xb = xs_s[rows, :]
        we = jnp.sum(jnp.where(lane_b == N_GROUPS + e, combs_s[rows, :], 0.0), axis=1, keepdims=True)
        a = _dot(xb, w1_ref[...])
        b3 = _dot(xb, w3_ref[...])
        hid = (a * jax.nn.sigmoid(a) * b3).astype(BF16)
        acc_s[rows, :] += we * _dot(hid, w2_ref[...])
        return 0
    lax.fori_loop(rng_s[g], rng_s[N_GROUPS + g], block, 0)

    @pl.when(e == N_EXPERTS - 1)
    def _():
        acc = acc_s[...]
        a_hi = acc.astype(BF16)
        a_lo = (acc - a_hi.astype(F32)).astype(BF16)
        y = _dot(pt_s[...], a_hi) + _dot(pt_s[...], a_lo)
        o_ref[...] = _layer_norm(alpha * x_ref[...] + y, lg_ref[...], lb_ref[...])


def _moe(x, wrh, wrl, br, w1, w3, w2, lg, lb, alpha, tm=1024, rb=256):
    T, D = x.shape
    const = lambda i, e: (0, 0)
    full = lambda a: pl.BlockSpec(a.shape, const)
    kern = functools.partial(_moe_kernel, alpha=alpha, tm=tm, rb=rb)
    return pl.pallas_call(
        kern, grid=(T // tm, N_EXPERTS),
        in_specs=[pl.BlockSpec((tm, D), lambda i, e: (i, 0)), full(wrh), full(wrl), full(br),
                  pl.BlockSpec((None, D, D_EXPERT), lambda i, e: (e, 0, 0)),
                  pl.BlockSpec((None, D, D_EXPERT), lambda i, e: (e, 0, 0)),
                  pl.BlockSpec((None, D_EXPERT, D), lambda i, e: (e, 0, 0)),
                  full(lg), full(lb)],
        out_specs=pl.BlockSpec((tm, D), lambda i, e: (i, 0)),
        out_shape=jax.ShapeDtypeStruct((T, D), F32),
        scratch_shapes=[pltpu.VMEM((tm, D), BF16),
                        pltpu.VMEM((tm, LANES), F32),
                        pltpu.VMEM((tm, tm), BF16),
                        pltpu.VMEM((tm, D), F32),
                        pltpu.SMEM((2 * N_GROUPS,), I32)],
        compiler_params=_params("parallel", "arbitrary"), name="moe")(x, wrh, wrl, br, w1, w3, w2, lg, lb)


def _rope_tables(seq):
    pos = jnp.arange(seq, dtype=F32)[:, None]
    one = lambda n: jnp.ones((seq, n), F32)
    zero = lambda n: jnp.zeros((seq, n), F32)

    inv_p = ROPE_THETA ** (-jnp.arange(0, ROT_DIM, 2, dtype=F32) / ROT_DIM)
    cp, sp = jnp.cos(pos * inv_p), jnp.sin(pos * inv_p)
    hp = ROT_DIM // 2
    rest = HEAD_DIM - ROT_DIM
    head = (jnp.concatenate([cp, cp, one(rest)], 1),
            jnp.concatenate([zero(hp), sp, zero(rest)], 1),
            jnp.concatenate([-sp, zero(hp), zero(rest)], 1))
    ident = (one(HEAD_DIM), zero(HEAD_DIM), zero(HEAD_DIM))
    both = jnp.stack([jnp.concatenate([a, a], 1) for a in head])
    first = jnp.stack([jnp.concatenate([a, b], 1) for a, b in zip(head, ident)])
    tab_p = jnp.stack([both, first])

    inv_m = ROPE_THETA ** (-jnp.arange(0, MLA_ROPE, 2, dtype=F32) / MLA_ROPE)
    cm, sm = jnp.cos(pos * inv_m), jnp.sin(pos * inv_m)
    hm = MLA_ROPE // 2
    pad = LANES - MLA_NOPE - MLA_ROPE
    tab_m = jnp.stack([jnp.concatenate([one(MLA_NOPE), cm, cm, one(pad)], 1),
                       jnp.concatenate([zero(MLA_NOPE), zero(hm), sm, zero(pad)], 1),
                       jnp.concatenate([zero(MLA_NOPE), -sm, zero(hm), zero(pad)], 1)])[None]
    return tab_p, tab_m


def _layer_weights(p, l):
    w_in = p["w_in"][l]
    D = w_in.shape[0]
    z = lambda n: jnp.zeros((D, n), F32)
    o = 0
    cuts = []
    for n in (MLA_Q_RANK, MLA_KV_RANK, MLA_ROPE, 3 * 3 * DIL_HEADS * HEAD_DIM, DSA_HEADS * HEAD_DIM,
              HEAD_DIM, HEAD_DIM, IDX_HEADS * IDX_DIM, IDX_DIM, IDX_HEADS):
        cuts.append(w_in[:, o:o + n])
        o += n
    w_cq, w_ckv, w_kr, w_dil, w_q, w_k, w_v, w_iq, w_ik, w_iw = cuts
    qs = HEAD_DIM ** -0.5
    w_mla = jnp.concatenate([w_cq, w_ckv, z(MLA_NOPE), w_kr, z(LANES - MLA_NOPE - MLA_ROPE)], 1)
    HW = DIL_HEADS * HEAD_DIM
    w_dil = w_dil.reshape(D, len(DIL_GROUPS), 3, HW)
    w_dqk = jnp.concatenate([w_dil[:, :, 0] * qs, w_dil[:, :, 1]], axis=-1).reshape(D, -1)
    w_dvt = w_dil[:, :, 2].reshape(D, -1).T
    w_dsa = jnp.concatenate([w_q * qs, w_k, w_v, w_iq, w_ik, w_iw, z(LANES - IDX_DIM - IDX_HEADS)], 1)

    def per_head(w, n_in, n_keep_lo, n_keep_hi):
        r = w.shape[0]
        w = w.reshape(r, MLA_HEADS, n_in)[:, :, n_keep_lo:n_keep_hi]
        w = jnp.pad(w, ((0, 0), (0, 0), (0, LANES - (n_keep_hi - n_keep_lo))))
        return w.reshape(r, MLA_HEADS * LANES)

    w_uq = per_head(p["w_uq"][l], MLA_NOPE + MLA_ROPE, 0, MLA_NOPE + MLA_ROPE)
    w_uk = per_head(p["w_ukv"][l], MLA_NOPE + MLA_V, 0, MLA_NOPE)
    w_uv = p["w_ukv"][l].reshape(MLA_KV_RANK, MLA_HEADS, MLA_NOPE + MLA_V)[:, :, MLA_NOPE:]
    w_uv = w_uv.reshape(MLA_KV_RANK, MLA_HEADS * MLA_V).T

    w_r = jnp.concatenate([p["w_group"][l], p["w_sub"][l], z(LANES - N_GROUPS - N_EXPERTS)], 1)
    w_r_hi = w_r.astype(BF16)
    w_r_lo = (w_r - w_r_hi.astype(F32)).astype(BF16)
    b_r = jnp.concatenate([p["b_group"][l], p["b_sub"][l], jnp.zeros((LANES - N_GROUPS - N_EXPERTS,), F32)])

    bf = lambda a: a.astype(BF16)
    return dict(
        w_mla=bf(w_mla), w_dqk=bf(w_dqk), w_dvt=bf(w_dvt), w_dsa=bf(w_dsa),
        q_g=p["q_norm_g"][l][None], kv_g=p["kv_norm_g"][l][None],
        w_uq=bf(w_uq), w_uk=bf(w_uk), w_uv=bf(w_uv),
        w_gate=bf(p["w_gate"][l]), b_gate=p["b_gate"][l][None],
        w_a=bf(p["w_a"][l]), w_b=bf(p["w_b"][l]), w_c=bf(p["w_c"][l]), w_o=bf(p["w_o"][l]),
        ln1_g=p["ln1_g"][l][None], ln1_b=p["ln1_b"][l][None],
        w_r_hi=w_r_hi, w_r_lo=w_r_lo, b_r=b_r[None],
        w1=bf(p["w1"][l]), w3=bf(p["w3"][l]), w2=bf(p["w2"][l]),
        ln2_g=p["ln2_g"][l][None], ln2_b=p["ln2_b"][l][None])


def _layer(xt, w, tab_p, tab_m, batch, seq, alpha):
    HW = DIL_HEADS * HEAD_DIM
    (grp,) = _proj(xt, w["w_mla"], seq, tn=w["w_mla"].shape[1], segs=((0, w["w_mla"].shape[1]),),
                   dtypes=(F32,), kinds=(-1,) * (w["w_mla"].shape[1] // LANES))
    (dil_qk,) = _proj(xt, w["w_dqk"], seq, tn=w["w_dqk"].shape[1], segs=((0, w["w_dqk"].shape[1]),),
                      dtypes=(BF16,), kinds=(0,) * (w["w_dqk"].shape[1] // LANES), tab=tab_p)
    dil_vt = _proj_t(xt, w["w_dvt"])
    qw = DSA_HEADS * HEAD_DIM
    iw_ = IDX_HEADS * IDX_DIM
    q_c, kv_c, iq_c, ikw_c = _proj(
        xt, w["w_dsa"], seq, tn=w["w_dsa"].shape[1],
        segs=((0, qw), (qw, LANES), (qw + LANES, iw_), (qw + LANES + iw_, LANES)),
        dtypes=(BF16, BF16, BF16, F32),
        kinds=(0,) * (qw // LANES) + (1,) + (0,) * (iw_ // LANES) + (1,), tab=tab_p)

    q_a, k_a, v_a = _mla_up(grp, w["q_g"], w["kv_g"], w["w_uq"], w["w_uk"], w["w_uv"], tab_m, seq)
    o_a = _mla_attn(q_a, k_a, v_a, batch, seq)
    o_b = _dil_attn(dil_qk, dil_vt, batch, seq)
    o_c = _dsa_attn(q_c, kv_c, iq_c, ikw_c, batch, seq)

    x1 = _mix_out(xt, o_a, o_b, o_c, w["w_gate"], w["b_gate"], w["w_a"], w["w_b"], w["w_c"], w["w_o"],
                  w["ln1_g"], w["ln1_b"], alpha)
    return _moe(x1, w["w_r_hi"], w["w_r_lo"], w["b_r"], w["w1"], w["w3"], w["w2"],
                w["ln2_g"], w["ln2_b"], alpha)


def kernel(x, w_in, q_norm_g, w_uq, kv_norm_g, w_ukv, w_gate, b_gate, w_a, w_b, w_c, w_o, ln1_g, ln1_b,
           w_group, b_group, w_sub, b_sub, w1, w3, w2, ln2_g, ln2_b):
    batch, seq, d_model = x.shape
    depth = w_in.shape[0]
    alpha = (2 * depth) ** 0.25
    p = dict(w_in=w_in, q_norm_g=q_norm_g, w_uq=w_uq, kv_norm_g=kv_norm_g, w_ukv=w_ukv, w_gate=w_gate,
             b_gate=b_gate, w_a=w_a, w_b=w_b, w_c=w_c, w_o=w_o, ln1_g=ln1_g, ln1_b=ln1_b, w_group=w_group,
             b_group=b_group, w_sub=w_sub, b_sub=b_sub, w1=w1, w3=w3, w2=w2, ln2_g=ln2_g, ln2_b=ln2_b)
    tab_p, tab_m = _rope_tables(seq)
    xt = x.reshape(batch * seq, d_model)
    for l in range(depth):
        xt = _layer(xt, _layer_weights(p, l), tab_p, tab_m, batch, seq, alpha)
    return xt.reshape(batch, seq, d_model)
```

```python
import functools

import jax
import jax.numpy as jnp
from jax import lax
from jax.experimental import pallas as pl
from jax.experimental.pallas import tpu as pltpu

F32 = jnp.float32
BF16 = jnp.bfloat16
I32 = jnp.int32

HEAD_DIM = 64
ROT_DIM = HEAD_DIM // 4
ROPE_THETA = 500000.0
EPS = 1e-6
MLA_HEADS = 8
MLA_Q_RANK = 256
MLA_KV_RANK = 128
MLA_NOPE = 64
MLA_ROPE = 32
MLA_V = 64
DIL_GROUPS = ((128, 1), (512, 4), (2048, 16))
DIL_HEADS = 4
DSA_HEADS = 8
IDX_HEADS = 8
IDX_DIM = 64
TOPK_MAX = 256
N_GROUPS = 4
EXPERTS_PER_GROUP = 8
N_EXPERTS = N_GROUPS * EXPERTS_PER_GROUP
D_EXPERT = 256
N_BRANCH = 3

LANES = 128
VMEM_LIMIT = 56 * 1024 * 1024
NEG_INF = float("-inf")
INT_MIN = -2 ** 31
M_FLOOR = -1e30
LOG2E = 1.4426950408889634
ROW_ALIGN = 16


def _params(*sem):
    return pltpu.CompilerParams(dimension_semantics=sem, vmem_limit_bytes=VMEM_LIMIT)


def _dot(a, b):
    return jnp.dot(a, b, preferred_element_type=F32)


def _dot_t(a, b):
    return lax.dot_general(a, b, (((1,), (1,)), ((), ())), preferred_element_type=F32)


def _rope_chunk(v, tab_ref, kind, shift):
    c = tab_ref[kind, 0]
    s1 = tab_ref[kind, 1]
    s2 = tab_ref[kind, 2]
    return v * c + pltpu.roll(v, shift, 1) * s1 + pltpu.roll(v, LANES - shift, 1) * s2


def _layer_norm(z, g, b):
    mu = jnp.mean(z, axis=-1, keepdims=True)
    zc = z - mu
    var = jnp.mean(zc * zc, axis=-1, keepdims=True)
    return zc * lax.rsqrt(var + EPS) * g + b


def _proj_kernel(*refs, segs, kinds, shift, has_tab):
    x_ref, w_ref = refs[0], refs[1]
    tab_ref = refs[2] if has_tab else None
    outs = refs[3:] if has_tab else refs[2:]
    acc = _dot(x_ref[...].astype(BF16), w_ref[...])
    for (c0, width), o_ref in zip(segs, outs):
        for c in range(width // LANES):
            cc = c0 // LANES + c
            v = acc[:, cc * LANES:(cc + 1) * LANES]
            if kinds[cc] >= 0:
                v = _rope_chunk(v, tab_ref, kinds[cc], shift)
            o_ref[:, c * LANES:(c + 1) * LANES] = v.astype(o_ref.dtype)


def _proj(x, w, seq, *, tn, segs, dtypes, kinds, tab=None, shift=ROT_DIM // 2, tm=512):
    T, K = x.shape
    N = w.shape[1]
    nj = N // tn
    spt = seq // tm
    in_specs = [pl.BlockSpec((tm, K), lambda i, j: (i, 0)),
                pl.BlockSpec((K, tn), lambda i, j: (0, j))]
    args = [x, w]
    if tab is not None:
        in_specs.append(pl.BlockSpec((tab.shape[0], 3, tm, LANES), lambda i, j: (0, 0, i % spt, 0)))
        args.append(tab)
    out_specs = [pl.BlockSpec((tm, wd), lambda i, j: (i, j)) for (_, wd) in segs]
    out_shape = [jax.ShapeDtypeStruct((T, wd * nj), dt) for (_, wd), dt in zip(segs, dtypes)]
    kern = functools.partial(_proj_kernel, segs=segs, kinds=kinds, shift=shift, has_tab=tab is not None)
    return pl.pallas_call(kern, grid=(T // tm, nj), in_specs=in_specs, out_specs=out_specs,
                          out_shape=out_shape, compiler_params=_params("parallel", "arbitrary"),
                          name="in_proj")(*args)


def _mla_up_kernel(g_ref, qg_ref, kvg_ref, wq_ref, wk_ref, wv_ref, tab_ref, q_out, k_out, v_out):
    g = g_ref[...]
    cq = g[:, :MLA_Q_RANK]
    ckv = g[:, MLA_Q_RANK:MLA_Q_RANK + MLA_KV_RANK]
    kr = g[:, MLA_Q_RANK + MLA_KV_RANK:]
    cqn = (cq * lax.rsqrt(jnp.mean(cq * cq, axis=-1, keepdims=True) + EPS) * qg_ref[...]).astype(BF16)
    ckvn = (ckv * lax.rsqrt(jnp.mean(ckv * ckv, axis=-1, keepdims=True) + EPS) * kvg_ref[...]).astype(BF16)
    q = _dot(cqn, wq_ref[...])
    k = _dot(ckvn, wk_ref[...])
    v_out[...] = _dot_t(wv_ref[...], ckvn).astype(v_out.dtype)
    for h in range(MLA_HEADS):
        sl = slice(h * LANES, (h + 1) * LANES)
        q_out[:, sl] = _rope_chunk(q[:, sl], tab_ref, 0, MLA_ROPE // 2).astype(q_out.dtype)
        k_out[:, sl] = _rope_chunk(k[:, sl] + kr, tab_ref, 0, MLA_ROPE // 2).astype(k_out.dtype)


def _mla_up(grp, qg, kvg, wq, wk, wv, tab, seq, tm=512):
    T = grp.shape[0]
    spt = seq // tm
    const = lambda i: (0, 0)
    return pl.pallas_call(
        _mla_up_kernel, grid=(T // tm,),
        in_specs=[pl.BlockSpec((tm, grp.shape[1]), lambda i: (i, 0)),
                  pl.BlockSpec(qg.shape, const), pl.BlockSpec(kvg.shape, const),
                  pl.BlockSpec(wq.shape, const), pl.BlockSpec(wk.shape, const), pl.BlockSpec(wv.shape, const),
                  pl.BlockSpec((1, 3, tm, LANES), lambda i: (0, 0, i % spt, 0))],
        out_specs=[pl.BlockSpec((tm, MLA_HEADS * LANES), lambda i: (i, 0)),
                   pl.BlockSpec((tm, MLA_HEADS * LANES), lambda i: (i, 0)),
                   pl.BlockSpec((MLA_HEADS * MLA_V, tm), lambda i: (0, i))],
        out_shape=[jax.ShapeDtypeStruct((T, MLA_HEADS * LANES), BF16),
                   jax.ShapeDtypeStruct((T, MLA_HEADS * LANES), BF16),
                   jax.ShapeDtypeStruct((MLA_HEADS * MLA_V, T), BF16)],
        compiler_params=_params("parallel"), name="mla_up")(grp, qg, kvg, wq, wk, wv, tab)


def _mla_attn_kernel(q_ref, k_ref, vt_ref, o_ref, acc_s, s_s, p_s, *, tq, c_exp):
    qi = pl.program_id(1)
    krow = lax.broadcasted_iota(I32, (tq, tq), 0)
    qcol = lax.broadcasted_iota(I32, (tq, tq), 1)
    diag_bias = jnp.where(krow <= qcol, 0.0, NEG_INF).astype(F32)
    acc_s[...] = jnp.zeros_like(acc_s)

    def step(j, carry, masked):
        ms, ls = carry
        ks = pl.multiple_of(j * tq, tq)
        for h in range(MLA_HEADS):
            kb = k_ref[pl.ds(ks, tq), h * LANES:(h + 1) * LANES]
            s_s[h] = _dot_t(kb, q_ref[:, h * LANES:(h + 1) * LANES])
        new_ms, new_ls, alphas = [], [], []
        for h in range(MLA_HEADS):
            s = s_s[h]
            if masked:
                s = s + diag_bias
            m_new = jnp.maximum(ms[h], jnp.max(s, axis=0, keepdims=True))
            alpha = jnp.exp2((ms[h] - m_new) * c_exp)
            p = jnp.exp2((s - m_new) * c_exp)
            new_ls.append(alpha * ls[h] + jnp.sum(p, axis=0, keepdims=True))
            new_ms.append(m_new)
            alphas.append(alpha)
            p_s[h] = p.astype(BF16)
        for h in range(MLA_HEADS):
            sl = slice(h * MLA_V, (h + 1) * MLA_V)
            acc_s[sl, :] = alphas[h] * acc_s[sl, :] + _dot(vt_ref[sl, pl.ds(ks, tq)], p_s[h])
        return tuple(new_ms), tuple(new_ls)

    init = (tuple(jnp.full((1, tq), NEG_INF, F32) for _ in range(MLA_HEADS)),
            tuple(jnp.zeros((1, tq), F32) for _ in range(MLA_HEADS)))
    carry = lax.fori_loop(0, qi, functools.partial(step, masked=False), init)
    _, ls = step(qi, carry, True)
    for h in range(MLA_HEADS):
        sl = slice(h * MLA_V, (h + 1) * MLA_V)
        acc_s[sl, :] = acc_s[sl, :] / ls[h]
    o_ref[...] = acc_s[...].T.astype(o_ref.dtype)


def _mla_attn(q, k, vt, batch, seq, tq=256):
    T = q.shape[0]
    nq = seq // tq
    c_exp = (MLA_NOPE + MLA_ROPE) ** -0.5 * LOG2E
    kern = functools.partial(_mla_attn_kernel, tq=tq, c_exp=c_exp)
    W = MLA_HEADS * LANES
    return pl.pallas_call(
        kern, grid=(batch, nq),
        in_specs=[pl.BlockSpec((tq, W), lambda b, i: (b * nq + i, 0)),
                  pl.BlockSpec((seq, W), lambda b, i: (b, 0)),
                  pl.BlockSpec((MLA_HEADS * MLA_V, seq), lambda b, i: (0, b))],
        out_specs=pl.BlockSpec((tq, MLA_HEADS * MLA_V), lambda b, i: (b * nq + i, 0)),
        out_shape=jax.ShapeDtypeStruct((T, MLA_HEADS * MLA_V), BF16),
        scratch_shapes=[pltpu.VMEM((MLA_HEADS * MLA_V, tq), F32),
                        pltpu.VMEM((MLA_HEADS, tq, tq), F32), pltpu.VMEM((MLA_HEADS, tq, tq), BF16)],
        compiler_params=_params("parallel", "arbitrary"), name="mla_attn")(q, k, vt)


def _dil_proj_kernel(x_ref, wqk_ref, wvt_ref, tab_ref, qk_out, vt_out):
    xb = x_ref[...].astype(BF16)
    acc = _dot(xb, wqk_ref[...])
    for c in range(acc.shape[1] // LANES):
        sl = slice(c * LANES, (c + 1) * LANES)
        qk_out[:, sl] = _rope_chunk(acc[:, sl], tab_ref, 0, ROT_DIM // 2).astype(qk_out.dtype)
    vt_out[...] = _dot_t(wvt_ref[...], xb).astype(vt_out.dtype)


def _dil_proj(x, wqk, wvt, tab, seq, d, tm):
    T, K = x.shape
    M = seq // d
    mt = M // tm
    nk = tab.shape[0]
    x3 = x.reshape(T // d, d * K)
    tab5 = tab.reshape(nk, 3, M, d * LANES)
    out_tile = lambda i, r: ((i // mt) * d + r) * mt + i % mt
    nqk, nvt = wqk.shape[1], wvt.shape[0]
    return pl.pallas_call(
        _dil_proj_kernel, grid=(T // d // tm, d),
        in_specs=[pl.BlockSpec((tm, K), lambda i, r: (i, r)),
                  pl.BlockSpec(wqk.shape, lambda i, r: (0, 0)), pl.BlockSpec(wvt.shape, lambda i, r: (0, 0)),
                  pl.BlockSpec((nk, 3, tm, LANES), lambda i, r: (0, 0, i % mt, r))],
        out_specs=[pl.BlockSpec((tm, nqk), lambda i, r: (out_tile(i, r), 0)),
                   pl.BlockSpec((nvt, tm), lambda i, r: (0, out_tile(i, r)))],
        out_shape=[jax.ShapeDtypeStruct((T, nqk), BF16), jax.ShapeDtypeStruct((nvt, T), BF16)],
        compiler_params=_params("parallel", "arbitrary"), name="dil_proj")(x3, wqk, wvt, tab5)


def _dil_lat_kernel(qk_ref, qkh_ref, vt_ref, vth_ref, o_ref, lse_ref, s_s, p_s, *, band, nsub, sub_per_seq):
    i = pl.program_id(0)
    HW = DIL_HEADS * HEAD_DIM
    kj = lax.broadcasted_iota(I32, (band, band), 0)
    qi = lax.broadcasted_iota(I32, (band, band), 1)
    bias_cur = jnp.where(kj <= qi, 0.0, NEG_INF).astype(F32)
    bias_prev = jnp.where(kj >= qi, 0.0, NEG_INF).astype(F32)
    prevs = []
    for u in range(nsub):
        if sub_per_seq == 1 or (u % sub_per_seq == 0 and nsub % sub_per_seq == 0):
            prevs.append(None)
        elif u > 0:
            prevs.append((qk_ref[(u - 1) * band:u * band, HW:2 * HW], vt_ref[:, (u - 1) * band:u * band],
                          bias_prev))
        else:
            has = (i * nsub) % sub_per_seq != 0
            prevs.append((qkh_ref[:, HW:2 * HW], vth_ref[...], jnp.where(has, bias_prev, NEG_INF)))

    for u in range(nsub):
        rows = slice(u * band, (u + 1) * band)
        q = qk_ref[rows, 0:HW]
        k_cur = qk_ref[rows, HW:2 * HW]
        for h in range(DIL_HEADS):
            hs = slice(h * HEAD_DIM, (h + 1) * HEAD_DIM)
            s_s[u, h, 0] = _dot_t(k_cur[:, hs], q[:, hs])
            if prevs[u] is not None:
                s_s[u, h, 1] = _dot_t(prevs[u][0][:, hs], q[:, hs])
    stats = {}
    for u in range(nsub):
        for h in range(DIL_HEADS):
            s_c = s_s[u, h, 0] + bias_cur
            m = jnp.max(s_c, axis=0, keepdims=True)
            if prevs[u] is not None:
                s_p = s_s[u, h, 1] + prevs[u][2]
                m = jnp.maximum(m, jnp.max(s_p, axis=0, keepdims=True))
            p_c = jnp.exp(s_c - m)
            l = jnp.sum(p_c, axis=0, keepdims=True)
            p_s[u, h, 0] = p_c.astype(BF16)
            if prevs[u] is not None:
                p_p = jnp.exp(s_p - m)
                l = l + jnp.sum(p_p, axis=0, keepdims=True)
                p_s[u, h, 1] = p_p.astype(BF16)
            stats[u, h] = (m, l)
    pad = jnp.zeros((band - DIL_HEADS, band), F32)
    for u in range(nsub):
        rows = slice(u * band, (u + 1) * band)
        outs, lses = [], []
        for h in range(DIL_HEADS):
            hs = slice(h * HEAD_DIM, (h + 1) * HEAD_DIM)
            m, l = stats[u, h]
            o = _dot(vt_ref[hs, rows], p_s[u, h, 0])
            if prevs[u] is not None:
                o = o + _dot(prevs[u][1][hs, :], p_s[u, h, 1])
            outs.append(o / l)
            lses.append(m + jnp.log(l))
        o_ref[rows, :] = jnp.concatenate(outs, axis=0).T.astype(o_ref.dtype)
        lse_ref[rows, :] = jnp.concatenate(lses + [pad], axis=0).T


def _dil_lat_attn(qk, vt, seq, d, band, nsub=4):
    T = qk.shape[0]
    HW = DIL_HEADS * HEAD_DIM
    M = seq // d
    assert band == LANES and M % band == 0
    sub_per_seq = M // band
    assert sub_per_seq % nsub == 0 or nsub % sub_per_seq == 0
    R = nsub * band
    halo = lambda i: jnp.maximum(i * nsub - 1, 0)
    kern = functools.partial(_dil_lat_kernel, band=band, nsub=nsub, sub_per_seq=sub_per_seq)
    return pl.pallas_call(
        kern, grid=(T // R,),
        in_specs=[pl.BlockSpec((R, 2 * HW), lambda i: (i, 0)),
                  pl.BlockSpec((band, 2 * HW), lambda i: (halo(i), 0)),
                  pl.BlockSpec((HW, R), lambda i: (0, i)),
                  pl.BlockSpec((HW, band), lambda i: (0, halo(i)))],
        out_specs=[pl.BlockSpec((R, HW), lambda i: (i, 0)), pl.BlockSpec((R, LANES), lambda i: (i, 0))],
        out_shape=[jax.ShapeDtypeStruct((T, HW), BF16), jax.ShapeDtypeStruct((T, LANES), F32)],
        scratch_shapes=[pltpu.VMEM((nsub, DIL_HEADS, 2, band, band), F32),
                        pltpu.VMEM((nsub, DIL_HEADS, 2, band, band), BF16)],
        compiler_params=_params("parallel"), name="dil_lat")(qk, qk, vt, vt)


def _dil_merge_kernel(*refs, dils, tmm):
    ng = len(dils)
    o_refs, l_refs = refs[0:2 * ng:2], refs[1:2 * ng:2]
    out_ref, scr = refs[2 * ng], refs[2 * ng + 1]
    nch = DIL_HEADS * HEAD_DIM // LANES
    o_pos, l_pos = [], []
    for g, d in enumerate(dils):
        if d == 1:
            o_pos.append([o_refs[g][:, c * LANES:(c + 1) * LANES].astype(F32) for c in range(nch)])
            l_pos.append(l_refs[g][...])
            continue
        for r in range(d):
            rows = pl.ds(r, tmm // d, stride=d)
            for c in range(nch):
                scr[g, c, rows, :] = o_refs[g][r, :, c * LANES:(c + 1) * LANES].astype(F32)
            scr[g, nch, rows, :] = l_refs[g][r]
        o_pos.append([scr[g, c] for c in range(nch)])
        l_pos.append(scr[g, nch])
    mx = functools.reduce(jnp.maximum, l_pos)
    es = [jnp.exp(l - mx) for l in l_pos]
    den = functools.reduce(lambda a, b: a + b, es)
    hpc = LANES // HEAD_DIM
    for c in range(nch):
        acc = None
        for g in range(ng):
            w = es[g] / den
            wc = jnp.concatenate([jnp.broadcast_to(w[:, c * hpc + k:c * hpc + k + 1], (tmm, HEAD_DIM))
                                  for k in range(hpc)], axis=1)
            term = wc * o_pos[g][c]
            acc = term if acc is None else acc + term
        out_ref[:, c * LANES:(c + 1) * LANES] = acc.astype(out_ref.dtype)


def _dil_merge(os_, ls_, batch, seq, tmm=256):
    dils = tuple(d for _, d in DIL_GROUPS)
    HW = DIL_HEADS * HEAD_DIM
    T = batch * seq
    tps = seq // tmm
    args, in_specs = [], []
    for o, l, d in zip(os_, ls_, dils):
        if d == 1:
            args += [o, l]
            in_specs += [pl.BlockSpec((tmm, HW), lambda i: (i, 0)), pl.BlockSpec((tmm, LANES), lambda i: (i, 0))]
        else:
            M = seq // d
            args += [o.reshape(batch, d, M, HW), l.reshape(batch, d, M, LANES)]
            in_specs += [pl.BlockSpec((None, d, tmm // d, HW), lambda i: (i // tps, 0, i % tps, 0)),
                         pl.BlockSpec((None, d, tmm // d, LANES), lambda i: (i // tps, 0, i % tps, 0))]
    kern = functools.partial(_dil_merge_kernel, dils=dils, tmm=tmm)
    return pl.pallas_call(
        kern, grid=(T // tmm,), in_specs=in_specs,
        out_specs=pl.BlockSpec((tmm, HW), lambda i: (i, 0)),
        out_shape=jax.ShapeDtypeStruct((T, HW), BF16),
        scratch_shapes=[pltpu.VMEM((len(dils), HW // LANES + 1, tmm, LANES), F32)],
        compiler_params=_params("parallel"), name="dil_merge")(*args)


def _dsa_kernel(q_ref, iq_ref, kv_ref, ikw_ref, o_ref, qall_s, iqall_s, kvt_s, key_s, bias_s, j_s, acc_s, p_s,
                *, tq, ck, top, seq):
    qi = pl.program_id(1)
    lo = qi * tq
    nch = (lo + tq + ck - 1) // ck
    krow = lax.broadcasted_iota(I32, (ck, tq), 0)
    qcol = lo + lax.broadcasted_iota(I32, (ck, tq), 1)

    def chunk(c):
        return pl.ds(pl.multiple_of(c * ck, ck), ck)

    @pl.when(qi == 0)
    def _():
        def body(c, _):
            kvt_s[:, chunk(c)] = kv_ref[chunk(c), :].astype(F32).T.astype(BF16)
            return 0
        lax.fori_loop(0, seq // ck, body, 0)

    for h in range(DSA_HEADS):
        qall_s[h * tq:(h + 1) * tq, :] = q_ref[:, h * HEAD_DIM:(h + 1) * HEAD_DIM]
    for h in range(IDX_HEADS):
        iqall_s[h * tq:(h + 1) * tq, :] = iq_ref[:, h * IDX_DIM:(h + 1) * IDX_DIM]

    @pl.when(lo + tq <= top)
    def _():
        def body(c, _):
            bias_s[chunk(c), :] = jnp.where(c * ck + krow <= qcol, 0.0, NEG_INF).astype(F32)
            return 0
        lax.fori_loop(0, nch, body, 0)

    @pl.when(lo + tq > top)
    def _():
        iw_t = ikw_ref[pl.ds(pl.multiple_of(lo, tq), tq), :].T[IDX_DIM:IDX_DIM + IDX_HEADS, :]
        iw_t = iw_t * (IDX_HEADS ** -0.5 * IDX_DIM ** -0.5)

        def score_body(c, _):
            ik = ikw_ref[chunk(c), 0:IDX_DIM].astype(BF16)
            r = _dot_t(ik, iqall_s[...])
            sc = jnp.zeros((ck, tq), F32)
            for h in range(IDX_HEADS):
                sc = sc + iw_t[h:h + 1, :] * jnp.maximum(r[:, h * tq:(h + 1) * tq], 0.0)
            sc = jnp.where(c * ck + krow <= qcol, sc, NEG_INF)
            bits = lax.bitcast_convert_type(sc, I32)
            key_s[chunk(c), :] = bits ^ ((bits >> 31) & 0x7FFFFFFF)
            return 0
        lax.fori_loop(0, nch, score_body, 0)

        def count(pred):
            def body(c, acc):
                ind = pred(key_s[chunk(c), :], c)
                return acc + jnp.sum(ind.reshape(ck // 64, 64, tq), axis=0)
            acc = lax.fori_loop(0, nch, body, jnp.zeros((64, tq), F32))
            return jnp.sum(acc, axis=0, keepdims=True)

        def count_ge(cand):
            return count(lambda kc, c: jnp.where(kc >= cand, 1.0, 0.0))

        topf = float(top)
        tau = jnp.where(count_ge(jnp.zeros((1, tq), I32)) >= topf, 0, INT_MIN).astype(I32)

        def search(i, tau):
            cand = tau | jnp.left_shift(jnp.int32(1), 30 - i)
            return jnp.where(count_ge(cand) >= topf, cand, tau)
        tau = lax.fori_loop(0, 31, search, tau)

        need = topf - count_ge(tau + 1)
        j_s[...] = jnp.full(j_s.shape, seq, I32)

        @pl.when(jnp.max(count_ge(tau)) > topf)
        def _():
            def count_eq_lt(J):
                return count(lambda kc, c: jnp.where(kc == tau, jnp.where(c * ck + krow < J, 1.0, 0.0), 0.0))

            nbits = seq.bit_length()

            def tie(i, J):
                cand = J + jnp.left_shift(jnp.int32(1), nbits - 1 - i)
                return jnp.where(count_eq_lt(cand) <= need, cand, J)
            J = lax.fori_loop(0, nbits, tie, jnp.zeros((1, tq), I32))
            j_s[...] = jnp.broadcast_to(J, j_s.shape)

        J = j_s[0:1, :]

        def bias_body(c, _):
            kc = key_s[chunk(c), :]
            sel = jnp.where(kc > tau, 0.0,
                            jnp.where(kc == tau, jnp.where(c * ck + krow < J, 0.0, NEG_INF), NEG_INF))
            bias_s[chunk(c), :] = sel.astype(F32)
            return 0
        lax.fori_loop(0, nch, bias_body, 0)

    acc_s[...] = jnp.zeros_like(acc_s)

    def step(c, carry):
        ms, ls = carry
        kc = kv_ref[chunk(c), 0:HEAD_DIM]
        bias = bias_s[chunk(c), :]
        vt = kvt_s[HEAD_DIM:2 * HEAD_DIM, chunk(c)]
        new_ms, new_ls, alphas = [], [], []
        for h in range(DSA_HEADS):
            if h % 2 == 0:
                s_pair = _dot_t(kc, qall_s[h * tq:(h + 2) * tq, :])
            s = s_pair[:, (h % 2) * tq:(h % 2 + 1) * tq] + bias
            m_new = jnp.maximum(ms[h], jnp.max(s, axis=0, keepdims=True))
            alpha = jnp.exp(ms[h] - m_new)
            p = jnp.exp(s - m_new)
            new_ls.append(alpha * ls[h] + jnp.sum(p, axis=0, keepdims=True))
            new_ms.append(m_new)
            alphas.append(alpha)
            p_s[h] = p.astype(BF16)
        for h in range(DSA_HEADS):
            sl = slice(h * HEAD_DIM, (h + 1) * HEAD_DIM)
            acc_s[sl, :] = alphas[h] * acc_s[sl, :] + _dot(vt, p_s[h])
        return tuple(new_ms), tuple(new_ls)

    init = (tuple(jnp.full((1, tq), M_FLOOR, F32) for _ in range(DSA_HEADS)),
            tuple(jnp.zeros((1, tq), F32) for _ in range(DSA_HEADS)))
    _, ls = lax.fori_loop(0, nch, step, init)
    for h in range(DSA_HEADS):
        sl = slice(h * HEAD_DIM, (h + 1) * HEAD_DIM)
        acc_s[sl, :] = acc_s[sl, :] / ls[h]
    o_ref[...] = acc_s[...].T.astype(o_ref.dtype)


def _dsa_attn(q, kv, iq, ikw, batch, seq, tq=LANES, ck=256):
    T = q.shape[0]
    nq = seq // tq
    top = min(TOPK_MAX, seq // 4)
    assert tq == LANES and top % tq == 0 and seq % ck == 0
    kern = functools.partial(_dsa_kernel, tq=tq, ck=ck, top=top, seq=seq)
    W = DSA_HEADS * HEAD_DIM
    return pl.pallas_call(
        kern, grid=(batch, nq),
        in_specs=[pl.BlockSpec((tq, W), lambda b, i: (b * nq + i, 0)),
                  pl.BlockSpec((tq, IDX_HEADS * IDX_DIM), lambda b, i: (b * nq + i, 0)),
                  pl.BlockSpec((seq, LANES), lambda b, i: (b, 0)),
                  pl.BlockSpec((seq, LANES), lambda b, i: (b, 0))],
        out_specs=pl.BlockSpec((tq, W), lambda b, i: (b * nq + i, 0)),
        out_shape=jax.ShapeDtypeStruct((T, W), BF16),
        scratch_shapes=[pltpu.VMEM((DSA_HEADS * tq, HEAD_DIM), BF16),
                        pltpu.VMEM((IDX_HEADS * tq, IDX_DIM), BF16),
                        pltpu.VMEM((LANES, seq), BF16),
                        pltpu.VMEM((seq, tq), I32),
                        pltpu.VMEM((seq, tq), F32),
                        pltpu.VMEM((8, tq), I32),
                        pltpu.VMEM((W, tq), F32),
                        pltpu.VMEM((DSA_HEADS, ck, tq), BF16)],
        compiler_params=_params("arbitrary", "arbitrary"), name="dsa_attn")(q, iq, kv, ikw)


def _mix_out_kernel(x_ref, oa_ref, ob_ref, oc_ref, wg_ref, bg_ref, wa_ref, wb_ref, wc_ref, wo_ref,
                    lg_ref, lb_ref, o_ref, *, alpha, d_model):
    x = x_ref[...]
    xb = x.astype(BF16)
    merged = None
    for br, (o_r, w_r) in enumerate(((oa_ref, wa_ref), (ob_ref, wb_ref), (oc_ref, wc_ref))):
        sl = slice(br * d_model, (br + 1) * d_model)
        gate = jax.nn.sigmoid(_dot(xb, wg_ref[:, sl]) + bg_ref[:, sl])
        term = gate * _dot(o_r[...], w_r[...])
        merged = term if merged is None else merged + term
    mix = _dot(merged.astype(BF16), wo_ref[...])
    o_ref[...] = _layer_norm(alpha * x + mix, lg_ref[...], lb_ref[...])


def _mix_out(x, oa, ob, oc, wg, bg, wa, wb, wc, wo, lg, lb, alpha, tm=256):
    T, D = x.shape
    const = lambda i: (0, 0)
    full = lambda a: pl.BlockSpec(a.shape, const)
    row = lambda a: pl.BlockSpec((tm, a.shape[1]), lambda i: (i, 0))
    kern = functools.partial(_mix_out_kernel, alpha=alpha, d_model=D)
    return pl.pallas_call(
        kern, grid=(T // tm,),
        in_specs=[row(x), row(oa), row(ob), row(oc), full(wg), full(bg), full(wa), full(wb), full(wc),
                  full(wo), full(lg), full(lb)],
        out_specs=pl.BlockSpec((tm, D), lambda i: (i, 0)),
        out_shape=jax.ShapeDtypeStruct((T, D), F32),
        compiler_params=_params("parallel"), name="mix_out")(x, oa, ob, oc, wg, bg, wa, wb, wc, wo, lg, lb)


def _route(x, wr_hi_ref, wr_lo_ref, br_ref):
    xh = x.astype(BF16)
    xl = (x - xh.astype(F32)).astype(BF16)
    logits = _dot(xh, wr_hi_ref[...]) + _dot(xl, wr_hi_ref[...]) + _dot(xh, wr_lo_ref[...]) + br_ref[...]
    lane = lax.broadcasted_iota(I32, logits.shape, 1).astype(F32)
    none = float(LANES)
    glog = jnp.where(lane < N_GROUPS, logits, NEG_INF)
    gmax = jnp.max(glog, axis=1, keepdims=True)
    g_p = 1.0 / jnp.sum(jnp.exp(glog - gmax), axis=1, keepdims=True)
    g_idx = jnp.min(jnp.where(glog == gmax, lane, none), axis=1, keepdims=True)
    first = N_GROUPS + g_idx * EXPERTS_PER_GROUP
    sub = jnp.where(lane >= first, jnp.where(lane < first + EXPERTS_PER_GROUP, logits, NEG_INF), NEG_INF)
    v1 = jnp.max(sub, axis=1, keepdims=True)
    i1 = jnp.min(jnp.where(sub == v1, lane, none), axis=1, keepdims=True)
    sub2 = jnp.where(lane == i1, NEG_INF, sub)
    v2 = jnp.max(sub2, axis=1, keepdims=True)
    i2 = jnp.min(jnp.where(sub2 == v2, lane, none), axis=1, keepdims=True)
    e2 = jnp.exp(v2 - v1)
    w1 = g_p / (1.0 + e2)
    w2 = g_p * e2 / (1.0 + e2)
    return jnp.where(lane == i1, w1, jnp.where(lane == i2, w2, 0.0)), g_idx


def _split3(v):
    hi = v.astype(BF16)
    r = v - hi.astype(F32)
    mid = r.astype(BF16)
    return hi, mid, (r - mid.astype(F32)).astype(BF16)


def _moe_kernel(x_ref, wrh_ref, wrl_ref, br_ref, w1_ref, w3_ref, w2_ref, lg_ref, lb_ref, o_ref,
                xs_s, combs_s, pt_s, acc_s, rng_s, *, alpha, tm, rb, ts):
    e = pl.program_id(1)

    @pl.when(e == 0)
    def _():
        x = x_ref[...]
        comb, g_idx = _route(x, wrh_ref, wrl_ref, br_ref)
        lane = lax.broadcasted_iota(I32, (tm, LANES), 1).astype(F32)
        onehot_g = jnp.where(lane == g_idx, 1.0, 0.0)
        ti = lax.broadcasted_iota(I32, (tm, tm), 0)
        tj = lax.broadcasted_iota(I32, (tm, tm), 1)
        earlier = jnp.where(tj < ti, 1.0, 0.0).astype(BF16)
        rank = _dot(earlier, onehot_g.astype(BF16))
        counts = jnp.sum(onehot_g, axis=0, keepdims=True)
        padded = jnp.ceil(counts * (1.0 / ROW_ALIGN)) * ROW_ALIGN
        lane1 = lax.broadcasted_iota(I32, (1, LANES), 1)
        off = jnp.zeros((1, LANES), F32)
        for k in range(1, N_GROUPS):
            off = off + jnp.where(lane1 >= k, pltpu.roll(padded, k, 1), 0.0)
        pos = jnp.sum(onehot_g * (off + rank), axis=1, keepdims=True)
        pos_row = jnp.broadcast_to(pos, (tm, LANES)).T[0:1, :]
        sj = lax.broadcasted_iota(I32, (tm, ts), 1)
        si = lax.broadcasted_iota(I32, (ts, tm), 0)
        pt_s[...] = jnp.where(sj.astype(F32) == pos, 1.0, 0.0).astype(BF16)
        perm = jnp.where(si.astype(F32) == pos_row, 1.0, 0.0).astype(BF16)
        xs_s[...] = _dot(perm, x.astype(BF16)).astype(BF16)
        c_hi, c_mid, c_lo = _split3(comb)
        combs_s[...] = _dot(perm, c_hi) + _dot(perm, c_mid) + _dot(perm, c_lo)
        acc_s[...] = jnp.zeros_like(acc_s)
        for g in range(N_GROUPS):
            start = jnp.sum(jnp.where(lane1 == g, off, 0.0)).astype(I32)
            cnt = jnp.sum(jnp.where(lane1 == g, counts, 0.0)).astype(I32)
            rng_s[g] = start
            rng_s[N_GROUPS + g] = (cnt + rb - 1) // rb

    g = e // EXPERTS_PER_GROUP
    lane_b = lax.broadcasted_iota(I32, (rb, LANES), 1)
    start = rng_s[g]

    def block(b, _):
        rows = pl.ds(pl.multiple_of(start + b * rb, ROW_ALIGN), rb)
        xb = xs_s[rows, :]
        we = jnp.sum(jnp.where(lane_b == N_GROUPS + e, combs_s[rows, :], 0.0), axis=1, keepdims=True)
        a = _dot(xb, w1_ref[...])
        b3 = _dot(xb, w3_ref[...])
        hid = (a * jax.nn.sigmoid(a) * b3).astype(BF16)
        acc_s[rows, :] += we * _dot(hid, w2_ref[...])
        return 0
    lax.fori_loop(0, rng_s[N_GROUPS + g], block, 0)

    @pl.when(e == N_EXPERTS - 1)
    def _():
        y = _dot(pt_s[...], acc_s[...].astype(BF16))
        o_ref[...] = _layer_norm(alpha * x_ref[...] + y, lg_ref[...], lb_ref[...])


def _moe(x, wrh, wrl, br, w1, w3, w2, lg, lb, alpha, tm=1024, rb=320):
    T, D = x.shape
    const = lambda i, e: (0, 0)
    full = lambda a: pl.BlockSpec(a.shape, const)
    ts = -(-(tm + N_GROUPS * ROW_ALIGN + rb) // LANES) * LANES
    kern = functools.partial(_moe_kernel, alpha=alpha, tm=tm, rb=rb, ts=ts)
    return pl.pallas_call(
        kern, grid=(T // tm, N_EXPERTS),
        in_specs=[pl.BlockSpec((tm, D), lambda i, e: (i, 0)), full(wrh), full(wrl), full(br),
                  pl.BlockSpec((None, D, D_EXPERT), lambda i, e: (e, 0, 0)),
                  pl.BlockSpec((None, D, D_EXPERT), lambda i, e: (e, 0, 0)),
                  pl.BlockSpec((None, D_EXPERT, D), lambda i, e: (e, 0, 0)),
                  full(lg), full(lb)],
        out_specs=pl.BlockSpec((tm, D), lambda i, e: (i, 0)),
        out_shape=jax.ShapeDtypeStruct((T, D), F32),
        scratch_shapes=[pltpu.VMEM((ts, D), BF16),
                        pltpu.VMEM((ts, LANES), F32),
                        pltpu.VMEM((tm, ts), BF16),
                        pltpu.VMEM((ts, D), F32),
                        pltpu.SMEM((2 * N_GROUPS,), I32)],
        compiler_params=_params("parallel", "arbitrary"), name="moe")(x, wrh, wrl, br, w1, w3, w2, lg, lb)


def _rope_tables(seq):
    pos = jnp.arange(seq, dtype=F32)[:, None]
    one = lambda n: jnp.ones((seq, n), F32)
    zero = lambda n: jnp.zeros((seq, n), F32)

    inv_p = ROPE_THETA ** (-jnp.arange(0, ROT_DIM, 2, dtype=F32) / ROT_DIM)
    cp, sp = jnp.cos(pos * inv_p), jnp.sin(pos * inv_p)
    hp = ROT_DIM // 2
    rest = HEAD_DIM - ROT_DIM
    head = (jnp.concatenate([cp, cp, one(rest)], 1),
            jnp.concatenate([zero(hp), sp, zero(rest)], 1),
            jnp.concatenate([-sp, zero(hp), zero(rest)], 1))
    ident = (one(HEAD_DIM), zero(HEAD_DIM), zero(HEAD_DIM))
    both = jnp.stack([jnp.concatenate([a, a], 1) for a in head])
    first = jnp.stack([jnp.concatenate([a, b], 1) for a, b in zip(head, ident)])
    tab_p = jnp.stack([both, first])

    inv_m = ROPE_THETA ** (-jnp.arange(0, MLA_ROPE, 2, dtype=F32) / MLA_ROPE)
    cm, sm = jnp.cos(pos * inv_m), jnp.sin(pos * inv_m)
    hm = MLA_ROPE // 2
    pad = LANES - MLA_NOPE - MLA_ROPE
    tab_m = jnp.stack([jnp.concatenate([one(MLA_NOPE), cm, cm, one(pad)], 1),
                       jnp.concatenate([zero(MLA_NOPE), zero(hm), sm, zero(pad)], 1),
                       jnp.concatenate([zero(MLA_NOPE), -sm, zero(hm), zero(pad)], 1)])[None]
    return tab_p, tab_m


def _layer_weights(p, l):
    w_in = p["w_in"][l]
    D = w_in.shape[0]
    z = lambda n: jnp.zeros((D, n), F32)
    o = 0
    cuts = []
    for n in (MLA_Q_RANK, MLA_KV_RANK, MLA_ROPE, 3 * 3 * DIL_HEADS * HEAD_DIM, DSA_HEADS * HEAD_DIM,
              HEAD_DIM, HEAD_DIM, IDX_HEADS * IDX_DIM, IDX_DIM, IDX_HEADS):
        cuts.append(w_in[:, o:o + n])
        o += n
    w_cq, w_ckv, w_kr, w_dil, w_q, w_k, w_v, w_iq, w_ik, w_iw = cuts
    qs = HEAD_DIM ** -0.5
    w_mla = jnp.concatenate([w_cq, w_ckv, z(MLA_NOPE), w_kr, z(LANES - MLA_NOPE - MLA_ROPE)], 1)
    HW = DIL_HEADS * HEAD_DIM
    w_dil = w_dil.reshape(D, len(DIL_GROUPS), 3, HW)
    w_dqk = jnp.concatenate([w_dil[:, :, 0] * qs, w_dil[:, :, 1]], axis=-1).transpose(1, 0, 2)
    w_dvt = w_dil[:, :, 2].transpose(1, 2, 0)
    w_dsa = jnp.concatenate([w_q * qs, w_k, w_v, w_iq, w_ik, w_iw, z(LANES - IDX_DIM - IDX_HEADS)], 1)

    def per_head(w, n_in, n_keep_lo, n_keep_hi):
        r = w.shape[0]
        w = w.reshape(r, MLA_HEADS, n_in)[:, :, n_keep_lo:n_keep_hi]
        w = jnp.pad(w, ((0, 0), (0, 0), (0, LANES - (n_keep_hi - n_keep_lo))))
        return w.reshape(r, MLA_HEADS * LANES)

    w_uq = per_head(p["w_uq"][l], MLA_NOPE + MLA_ROPE, 0, MLA_NOPE + MLA_ROPE)
    w_uk = per_head(p["w_ukv"][l], MLA_NOPE + MLA_V, 0, MLA_NOPE)
    w_uv = p["w_ukv"][l].reshape(MLA_KV_RANK, MLA_HEADS, MLA_NOPE + MLA_V)[:, :, MLA_NOPE:]
    w_uv = w_uv.reshape(MLA_KV_RANK, MLA_HEADS * MLA_V).T

    w_r = jnp.concatenate([p["w_group"][l], p["w_sub"][l], z(LANES - N_GROUPS - N_EXPERTS)], 1)
    w_r_hi = w_r.astype(BF16)
    w_r_lo = (w_r - w_r_hi.astype(F32)).astype(BF16)
    b_r = jnp.concatenate([p["b_group"][l], p["b_sub"][l], jnp.zeros((LANES - N_GROUPS - N_EXPERTS,), F32)])

    bf = lambda a: a.astype(BF16)
    return dict(
        w_mla=bf(w_mla), w_dqk=bf(w_dqk), w_dvt=bf(w_dvt), w_dsa=bf(w_dsa),
        q_g=p["q_norm_g"][l][None], kv_g=p["kv_norm_g"][l][None],
        w_uq=bf(w_uq), w_uk=bf(w_uk), w_uv=bf(w_uv),
        w_gate=bf(p["w_gate"][l]), b_gate=p["b_gate"][l][None],
        w_a=bf(p["w_a"][l]), w_b=bf(p["w_b"][l]), w_c=bf(p["w_c"][l]), w_o=bf(p["w_o"][l]),
        ln1_g=p["ln1_g"][l][None], ln1_b=p["ln1_b"][l][None],
        w_r_hi=w_r_hi, w_r_lo=w_r_lo, b_r=b_r[None],
        w1=bf(p["w1"][l]), w3=bf(p["w3"][l]), w2=bf(p["w2"][l]),
        ln2_g=p["ln2_g"][l][None], ln2_b=p["ln2_b"][l][None])


def _layer(xt, w, tab_p, tab_m, batch, seq, alpha):
    HW = DIL_HEADS * HEAD_DIM
    (grp,) = _proj(xt, w["w_mla"], seq, tn=w["w_mla"].shape[1], segs=((0, w["w_mla"].shape[1]),),
                   dtypes=(F32,), kinds=(-1,) * (w["w_mla"].shape[1] // LANES))
    dil_o, dil_lse = [], []
    for g, (window, d) in enumerate(DIL_GROUPS):
        qk_g, vt_g = _dil_proj(xt, w["w_dqk"][g], w["w_dvt"][g], tab_p[:1], seq, d, tm=min(512, seq // d))
        o_g, lse_g = _dil_lat_attn(qk_g, vt_g, seq, d, band=window // d)
        dil_o.append(o_g)
        dil_lse.append(lse_g)
    qw = DSA_HEADS * HEAD_DIM
    iw_ = IDX_HEADS * IDX_DIM
    q_c, kv_c, iq_c, ikw_c = _proj(
        xt, w["w_dsa"], seq, tn=w["w_dsa"].shape[1],
        segs=((0, qw), (qw, LANES), (qw + LANES, iw_), (qw + LANES + iw_, LANES)),
        dtypes=(BF16, BF16, BF16, F32),
        kinds=(0,) * (qw // LANES) + (1,) + (0,) * (iw_ // LANES) + (1,), tab=tab_p)

    q_a, k_a, v_a = _mla_up(grp, w["q_g"], w["kv_g"], w["w_uq"], w["w_uk"], w["w_uv"], tab_m, seq)
    o_a = _mla_attn(q_a, k_a, v_a, batch, seq)
    o_b = _dil_merge(dil_o, dil_lse, batch, seq)
    o_c = _dsa_attn(q_c, kv_c, iq_c, ikw_c, batch, seq)

    x1 = _mix_out(xt, o_a, o_b, o_c, w["w_gate"], w["b_gate"], w["w_a"], w["w_b"], w["w_c"], w["w_o"],
                  w["ln1_g"], w["ln1_b"], alpha)
    return _moe(x1, w["w_r_hi"], w["w_r_lo"], w["b_r"], w["w1"], w["w3"], w["w2"],
                w["ln2_g"], w["ln2_b"], alpha)


def kernel(x, w_in, q_norm_g, w_uq, kv_norm_g, w_ukv, w_gate, b_gate, w_a, w_b, w_c, w_o, ln1_g, ln1_b,
           w_group, b_group, w_sub, b_sub, w1, w3, w2, ln2_g, ln2_b):
    batch, seq, d_model = x.shape
    depth = w_in.shape[0]
    alpha = (2 * depth) ** 0.25
    p = dict(w_in=w_in, q_norm_g=q_norm_g, w_uq=w_uq, kv_norm_g=kv_norm_g, w_ukv=w_ukv, w_gate=w_gate,
             b_gate=b_gate, w_a=w_a, w_b=w_b, w_c=w_c, w_o=w_o, ln1_g=ln1_g, ln1_b=ln1_b, w_group=w_group,
             b_group=b_group, w_sub=w_sub, b_sub=b_sub, w1=w1, w3=w3, w2=w2, ln2_g=ln2_g, ln2_b=ln2_b)
    tab_p, tab_m = _rope_tables(seq)
    xt = x.reshape(batch * seq, d_model)
    for l in range(depth):
        xt = _layer(xt, _layer_weights(p, l), tab_p, tab_m, batch, seq, alpha)
    return xt.reshape(batch, seq, d_model)
```

```python
import functools

import jax
import jax.numpy as jnp
from jax import lax
from jax.experimental import pallas as pl
from jax.experimental.pallas import tpu as pltpu

F32 = jnp.float32
BF16 = jnp.bfloat16
I32 = jnp.int32

HEAD_DIM = 64
ROT_DIM = HEAD_DIM // 4
ROPE_THETA = 500000.0
EPS = 1e-6
MLA_HEADS = 8
MLA_Q_RANK = 256
MLA_KV_RANK = 128
MLA_NOPE = 64
MLA_ROPE = 32
MLA_V = 64
DIL_GROUPS = ((128, 1), (512, 4), (2048, 16))
DIL_HEADS = 4
DSA_HEADS = 8
IDX_HEADS = 8
IDX_DIM = 64
TOPK_MAX = 256
N_GROUPS = 4
EXPERTS_PER_GROUP = 8
N_EXPERTS = N_GROUPS * EXPERTS_PER_GROUP
D_EXPERT = 256
N_BRANCH = 3

LANES = 128
VMEM_LIMIT = 56 * 1024 * 1024
NEG_INF = float("-inf")
INT_MIN = -2 ** 31
M_FLOOR = -1e30
LOG2E = 1.4426950408889634
ROW_ALIGN = 16


def _params(*sem):
    return pltpu.CompilerParams(dimension_semantics=sem, vmem_limit_bytes=VMEM_LIMIT)


def _dot(a, b):
    return jnp.dot(a, b, preferred_element_type=F32)


def _dot_t(a, b):
    return lax.dot_general(a, b, (((1,), (1,)), ((), ())), preferred_element_type=F32)


def _rope_chunk(v, tab_ref, kind, shift):
    c = tab_ref[kind, 0]
    s1 = tab_ref[kind, 1]
    s2 = tab_ref[kind, 2]
    return v * c + pltpu.roll(v, shift, 1) * s1 + pltpu.roll(v, LANES - shift, 1) * s2


def _layer_norm(z, g, b):
    mu = jnp.mean(z, axis=-1, keepdims=True)
    zc = z - mu
    var = jnp.mean(zc * zc, axis=-1, keepdims=True)
    return zc * lax.rsqrt(var + EPS) * g + b


def _proj_kernel(*refs, segs, kinds, shift, has_tab):
    x_ref, w_ref = refs[0], refs[1]
    tab_ref = refs[2] if has_tab else None
    outs = refs[3:] if has_tab else refs[2:]
    acc = _dot(x_ref[...].astype(BF16), w_ref[...])
    for (c0, width), o_ref in zip(segs, outs):
        for c in range(width // LANES):
            cc = c0 // LANES + c
            v = acc[:, cc * LANES:(cc + 1) * LANES]
            if kinds[cc] >= 0:
                v = _rope_chunk(v, tab_ref, kinds[cc], shift)
            o_ref[:, c * LANES:(c + 1) * LANES] = v.astype(o_ref.dtype)


def _proj(x, w, seq, *, tn, segs, dtypes, kinds, tab=None, shift=ROT_DIM // 2, tm=512):
    T, K = x.shape
    N = w.shape[1]
    nj = N // tn
    spt = seq // tm
    in_specs = [pl.BlockSpec((tm, K), lambda i, j: (i, 0)),
                pl.BlockSpec((K, tn), lambda i, j: (0, j))]
    args = [x, w]
    if tab is not None:
        in_specs.append(pl.BlockSpec((tab.shape[0], 3, tm, LANES), lambda i, j: (0, 0, i % spt, 0)))
        args.append(tab)
    out_specs = [pl.BlockSpec((tm, wd), lambda i, j: (i, j)) for (_, wd) in segs]
    out_shape = [jax.ShapeDtypeStruct((T, wd * nj), dt) for (_, wd), dt in zip(segs, dtypes)]
    kern = functools.partial(_proj_kernel, segs=segs, kinds=kinds, shift=shift, has_tab=tab is not None)
    return pl.pallas_call(kern, grid=(T // tm, nj), in_specs=in_specs, out_specs=out_specs,
                          out_shape=out_shape, compiler_params=_params("parallel", "arbitrary"),
                          name="in_proj")(*args)


def _mla_up_kernel(g_ref, qg_ref, kvg_ref, wq_ref, wk_ref, wv_ref, tab_ref, q_out, k_out, v_out):
    g = g_ref[...]
    cq = g[:, :MLA_Q_RANK]
    ckv = g[:, MLA_Q_RANK:MLA_Q_RANK + MLA_KV_RANK]
    kr = g[:, MLA_Q_RANK + MLA_KV_RANK:]
    cqn = (cq * lax.rsqrt(jnp.mean(cq * cq, axis=-1, keepdims=True) + EPS) * qg_ref[...]).astype(BF16)
    ckvn = (ckv * lax.rsqrt(jnp.mean(ckv * ckv, axis=-1, keepdims=True) + EPS) * kvg_ref[...]).astype(BF16)
    q = _dot(cqn, wq_ref[...])
    k = _dot(ckvn, wk_ref[...])
    v_out[...] = _dot_t(wv_ref[...], ckvn).astype(v_out.dtype)
    for h in range(MLA_HEADS):
        sl = slice(h * LANES, (h + 1) * LANES)
        q_out[:, sl] = _rope_chunk(q[:, sl], tab_ref, 0, MLA_ROPE // 2).astype(q_out.dtype)
        k_out[:, sl] = _rope_chunk(k[:, sl] + kr, tab_ref, 0, MLA_ROPE // 2).astype(k_out.dtype)


def _mla_up(grp, qg, kvg, wq, wk, wv, tab, seq, tm=512):
    T = grp.shape[0]
    spt = seq // tm
    const = lambda i: (0, 0)
    return pl.pallas_call(
        _mla_up_kernel, grid=(T // tm,),
        in_specs=[pl.BlockSpec((tm, grp.shape[1]), lambda i: (i, 0)),
                  pl.BlockSpec(qg.shape, const), pl.BlockSpec(kvg.shape, const),
                  pl.BlockSpec(wq.shape, const), pl.BlockSpec(wk.shape, const), pl.BlockSpec(wv.shape, const),
                  pl.BlockSpec((1, 3, tm, LANES), lambda i: (0, 0, i % spt, 0))],
        out_specs=[pl.BlockSpec((tm, MLA_HEADS * LANES), lambda i: (i, 0)),
                   pl.BlockSpec((tm, MLA_HEADS * LANES), lambda i: (i, 0)),
                   pl.BlockSpec((MLA_HEADS * MLA_V, tm), lambda i: (0, i))],
        out_shape=[jax.ShapeDtypeStruct((T, MLA_HEADS * LANES), BF16),
                   jax.ShapeDtypeStruct((T, MLA_HEADS * LANES), BF16),
                   jax.ShapeDtypeStruct((MLA_HEADS * MLA_V, T), BF16)],
        compiler_params=_params("parallel"), name="mla_up")(grp, qg, kvg, wq, wk, wv, tab)


def _mla_attn_kernel(q_ref, k_ref, vt_ref, o_ref, acc_s, s_s, p_s, *, tq, c_exp):
    qi = pl.program_id(1)
    krow = lax.broadcasted_iota(I32, (tq, tq), 0)
    qcol = lax.broadcasted_iota(I32, (tq, tq), 1)
    diag_bias = jnp.where(krow <= qcol, 0.0, NEG_INF).astype(F32)
    acc_s[...] = jnp.zeros_like(acc_s)

    def step(j, carry, masked):
        ms, ls = carry
        ks = pl.multiple_of(j * tq, tq)
        for h in range(MLA_HEADS):
            kb = k_ref[pl.ds(ks, tq), h * LANES:(h + 1) * LANES]
            s_s[h] = _dot_t(kb, q_ref[:, h * LANES:(h + 1) * LANES])
        new_ms, new_ls, alphas = [], [], []
        for h in range(MLA_HEADS):
            s = s_s[h]
            if masked:
                s = s + diag_bias
            m_new = jnp.maximum(ms[h], jnp.max(s, axis=0, keepdims=True))
            alpha = jnp.exp2((ms[h] - m_new) * c_exp)
            p = jnp.exp2((s - m_new) * c_exp)
            new_ls.append(alpha * ls[h] + jnp.sum(p, axis=0, keepdims=True))
            new_ms.append(m_new)
            alphas.append(alpha)
            p_s[h] = p.astype(BF16)
        for h in range(MLA_HEADS):
            sl = slice(h * MLA_V, (h + 1) * MLA_V)
            acc_s[sl, :] = alphas[h] * acc_s[sl, :] + _dot(vt_ref[sl, pl.ds(ks, tq)], p_s[h])
        return tuple(new_ms), tuple(new_ls)

    init = (tuple(jnp.full((1, tq), NEG_INF, F32) for _ in range(MLA_HEADS)),
            tuple(jnp.zeros((1, tq), F32) for _ in range(MLA_HEADS)))
    carry = lax.fori_loop(0, qi, functools.partial(step, masked=False), init)
    _, ls = step(qi, carry, True)
    for h in range(MLA_HEADS):
        sl = slice(h * MLA_V, (h + 1) * MLA_V)
        acc_s[sl, :] = acc_s[sl, :] / ls[h]
    o_ref[...] = acc_s[...].T.astype(o_ref.dtype)


def _mla_attn(q, k, vt, batch, seq, tq=256):
    T = q.shape[0]
    nq = seq // tq
    c_exp = (MLA_NOPE + MLA_ROPE) ** -0.5 * LOG2E
    kern = functools.partial(_mla_attn_kernel, tq=tq, c_exp=c_exp)
    W = MLA_HEADS * LANES
    return pl.pallas_call(
        kern, grid=(batch, nq),
        in_specs=[pl.BlockSpec((tq, W), lambda b, i: (b * nq + i, 0)),
                  pl.BlockSpec((seq, W), lambda b, i: (b, 0)),
                  pl.BlockSpec((MLA_HEADS * MLA_V, seq), lambda b, i: (0, b))],
        out_specs=pl.BlockSpec((tq, MLA_HEADS * MLA_V), lambda b, i: (b * nq + i, 0)),
        out_shape=jax.ShapeDtypeStruct((T, MLA_HEADS * MLA_V), BF16),
        scratch_shapes=[pltpu.VMEM((MLA_HEADS * MLA_V, tq), F32),
                        pltpu.VMEM((MLA_HEADS, tq, tq), F32), pltpu.VMEM((MLA_HEADS, tq, tq), BF16)],
        compiler_params=_params("parallel", "arbitrary"), name="mla_attn")(q, k, vt)


def _dil_proj_kernel(x_ref, w_ref, tab_ref, out_ref, scr, *, d, nrope):
    acc = _dot(x_ref[...].astype(BF16), w_ref[...])
    tm = acc.shape[0]
    nch = acc.shape[1] // LANES
    for c in range(nch):
        sl = slice(c * LANES, (c + 1) * LANES)
        v = acc[:, sl]
        if c < nrope:
            v = _rope_chunk(v, tab_ref, 0, ROT_DIM // 2)
        if d == 1:
            out_ref[:, sl] = v.astype(out_ref.dtype)
        else:
            scr[c] = v
    if d > 1:
        for r in range(d):
            for c in range(nch):
                out_ref[r, :, c * LANES:(c + 1) * LANES] = (
                    scr[c, pl.ds(r, tm // d, stride=d), :].astype(out_ref.dtype))


def _dil_proj(x, w, tab, batch, seq, d, tm=256):
    T, K = x.shape
    N = w.shape[1]
    spt = seq // tm
    if d == 1:
        out_shape = jax.ShapeDtypeStruct((T, N), BF16)
        out_spec = pl.BlockSpec((tm, N), lambda i: (i, 0))
    else:
        out_shape = jax.ShapeDtypeStruct((batch, d, seq // d, N), BF16)
        out_spec = pl.BlockSpec((None, d, tm // d, N), lambda i: (i // spt, 0, i % spt, 0))
    kern = functools.partial(_dil_proj_kernel, d=d, nrope=2 * DIL_HEADS * HEAD_DIM // LANES)
    out = pl.pallas_call(
        kern, grid=(T // tm,),
        in_specs=[pl.BlockSpec((tm, K), lambda i: (i, 0)), pl.BlockSpec(w.shape, lambda i: (0, 0)),
                  pl.BlockSpec((1, 3, tm, LANES), lambda i: (0, 0, i % spt, 0))],
        out_specs=out_spec, out_shape=out_shape,
        scratch_shapes=[pltpu.VMEM((N // LANES, tm, LANES), F32)],
        compiler_params=_params("parallel"), name="dil_proj")(x, w, tab)
    return out.reshape(T, N)


def _dil_lat_kernel(qkv_ref, halo_ref, o_ref, lse_ref, s_s, p_s, *, band, nsub, sub_per_seq):
    i = pl.program_id(0)
    HW = DIL_HEADS * HEAD_DIM
    v_t = lambda ref, rows: ref[rows, 2 * HW:3 * HW].astype(F32).T.astype(BF16)
    vts = [v_t(qkv_ref, slice(u * band, (u + 1) * band)) for u in range(nsub)]
    qk_ref, qkh_ref = qkv_ref, halo_ref
    kj = lax.broadcasted_iota(I32, (band, band), 0)
    qi = lax.broadcasted_iota(I32, (band, band), 1)
    bias_cur = jnp.where(kj <= qi, 0.0, NEG_INF).astype(F32)
    bias_prev = jnp.where(kj >= qi, 0.0, NEG_INF).astype(F32)
    prevs = []
    for u in range(nsub):
        if sub_per_seq == 1 or (u % sub_per_seq == 0 and nsub % sub_per_seq == 0):
            prevs.append(None)
        elif u > 0:
            prevs.append((qk_ref[(u - 1) * band:u * band, HW:2 * HW], vts[u - 1], bias_prev))
        else:
            has = (i * nsub) % sub_per_seq != 0
            prevs.append((qkh_ref[:, HW:2 * HW], v_t(qkh_ref, slice(0, band)),
                          jnp.where(has, bias_prev, NEG_INF)))

    for u in range(nsub):
        rows = slice(u * band, (u + 1) * band)
        q = qk_ref[rows, 0:HW]
        k_cur = qk_ref[rows, HW:2 * HW]
        for h in range(DIL_HEADS):
            hs = slice(h * HEAD_DIM, (h + 1) * HEAD_DIM)
            s_s[u, h, 0] = _dot_t(k_cur[:, hs], q[:, hs])
            if prevs[u] is not None:
                s_s[u, h, 1] = _dot_t(prevs[u][0][:, hs], q[:, hs])
    stats = {}
    for u in range(nsub):
        for h in range(DIL_HEADS):
            s_c = s_s[u, h, 0] + bias_cur
            m = jnp.max(s_c, axis=0, keepdims=True)
            if prevs[u] is not None:
                s_p = s_s[u, h, 1] + prevs[u][2]
                m = jnp.maximum(m, jnp.max(s_p, axis=0, keepdims=True))
            p_c = jnp.exp(s_c - m)
            l = jnp.sum(p_c, axis=0, keepdims=True)
            p_s[u, h, 0] = p_c.astype(BF16)
            if prevs[u] is not None:
                p_p = jnp.exp(s_p - m)
                l = l + jnp.sum(p_p, axis=0, keepdims=True)
                p_s[u, h, 1] = p_p.astype(BF16)
            stats[u, h] = (m, l)
    pad = jnp.zeros((band - DIL_HEADS, band), F32)
    for u in range(nsub):
        rows = slice(u * band, (u + 1) * band)
        outs, lses = [], []
        for h in range(DIL_HEADS):
            hs = slice(h * HEAD_DIM, (h + 1) * HEAD_DIM)
            m, l = stats[u, h]
            o = _dot(vts[u][hs, :], p_s[u, h, 0])
            if prevs[u] is not None:
                o = o + _dot(prevs[u][1][hs, :], p_s[u, h, 1])
            outs.append(o / l)
            lses.append(m + jnp.log(l))
        o_ref[rows, :] = jnp.concatenate(outs, axis=0).T.astype(o_ref.dtype)
        lse_ref[rows, :] = jnp.concatenate(lses + [pad], axis=0).T


def _dil_lat_attn(qkv, seq, d, band, nsub=4):
    T = qkv.shape[0]
    HW = DIL_HEADS * HEAD_DIM
    M = seq // d
    assert band == LANES and M % band == 0
    sub_per_seq = M // band
    assert sub_per_seq % nsub == 0 or nsub % sub_per_seq == 0
    R = nsub * band
    halo = lambda i: jnp.maximum(i * nsub - 1, 0)
    kern = functools.partial(_dil_lat_kernel, band=band, nsub=nsub, sub_per_seq=sub_per_seq)
    return pl.pallas_call(
        kern, grid=(T // R,),
        in_specs=[pl.BlockSpec((R, 3 * HW), lambda i: (i, 0)),
                  pl.BlockSpec((band, 3 * HW), lambda i: (halo(i), 0))],
        out_specs=[pl.BlockSpec((R, HW), lambda i: (i, 0)), pl.BlockSpec((R, LANES), lambda i: (i, 0))],
        out_shape=[jax.ShapeDtypeStruct((T, HW), BF16), jax.ShapeDtypeStruct((T, LANES), F32)],
        scratch_shapes=[pltpu.VMEM((nsub, DIL_HEADS, 2, band, band), F32),
                        pltpu.VMEM((nsub, DIL_HEADS, 2, band, band), BF16)],
        compiler_params=_params("parallel"), name="dil_lat")(qkv, qkv)


def _dil_merge_kernel(*refs, dils, tmm):
    ng = len(dils)
    o_refs, l_refs = refs[0:2 * ng:2], refs[1:2 * ng:2]
    out_ref, scr = refs[2 * ng], refs[2 * ng + 1]
    nch = DIL_HEADS * HEAD_DIM // LANES
    o_pos, l_pos = [], []
    for g, d in enumerate(dils):
        if d == 1:
            o_pos.append([o_refs[g][:, c * LANES:(c + 1) * LANES].astype(F32) for c in range(nch)])
            l_pos.append(l_refs[g][...])
            continue
        for r in range(d):
            rows = pl.ds(r, tmm // d, stride=d)
            for c in range(nch):
                scr[g, c, rows, :] = o_refs[g][r, :, c * LANES:(c + 1) * LANES].astype(F32)
            scr[g, nch, rows, :] = l_refs[g][r]
        o_pos.append([scr[g, c] for c in range(nch)])
        l_pos.append(scr[g, nch])
    mx = functools.reduce(jnp.maximum, l_pos)
    es = [jnp.exp(l - mx) for l in l_pos]
    den = functools.reduce(lambda a, b: a + b, es)
    hpc = LANES // HEAD_DIM
    for c in range(nch):
        acc = None
        for g in range(ng):
            w = es[g] / den
            wc = jnp.concatenate([jnp.broadcast_to(w[:, c * hpc + k:c * hpc + k + 1], (tmm, HEAD_DIM))
                                  for k in range(hpc)], axis=1)
            term = wc * o_pos[g][c]
            acc = term if acc is None else acc + term
        out_ref[:, c * LANES:(c + 1) * LANES] = acc.astype(out_ref.dtype)


def _dil_merge(os_, ls_, batch, seq, tmm=256):
    dils = tuple(d for _, d in DIL_GROUPS)
    HW = DIL_HEADS * HEAD_DIM
    T = batch * seq
    tps = seq // tmm
    args, in_specs = [], []
    for o, l, d in zip(os_, ls_, dils):
        if d == 1:
            args += [o, l]
            in_specs += [pl.BlockSpec((tmm, HW), lambda i: (i, 0)), pl.BlockSpec((tmm, LANES), lambda i: (i, 0))]
        else:
            M = seq // d
            args += [o.reshape(batch, d, M, HW), l.reshape(batch, d, M, LANES)]
            in_specs += [pl.BlockSpec((None, d, tmm // d, HW), lambda i: (i // tps, 0, i % tps, 0)),
                         pl.BlockSpec((None, d, tmm // d, LANES), lambda i: (i // tps, 0, i % tps, 0))]
    kern = functools.partial(_dil_merge_kernel, dils=dils, tmm=tmm)
    return pl.pallas_call(
        kern, grid=(T // tmm,), in_specs=in_specs,
        out_specs=pl.BlockSpec((tmm, HW), lambda i: (i, 0)),
        out_shape=jax.ShapeDtypeStruct((T, HW), BF16),
        scratch_shapes=[pltpu.VMEM((len(dils), HW // LANES + 1, tmm, LANES), F32)],
        compiler_params=_params("parallel"), name="dil_merge")(*args)


def _dsa_kernel(q_ref, iq_ref, kv_ref, ikw_ref, o_ref, qall_s, iqall_s, kvt_s, key_s, bias_s, j_s, acc_s, p_s,
                *, tq, ck, top, seq):
    qi = pl.program_id(1)
    lo = qi * tq
    nch = (lo + tq + ck - 1) // ck
    krow = lax.broadcasted_iota(I32, (ck, tq), 0)
    qcol = lo + lax.broadcasted_iota(I32, (ck, tq), 1)

    def chunk(c):
        return pl.ds(pl.multiple_of(c * ck, ck), ck)

    @pl.when(qi == 0)
    def _():
        def body(c, _):
            kvt_s[:, chunk(c)] = kv_ref[chunk(c), :].astype(F32).T.astype(BF16)
            return 0
        lax.fori_loop(0, seq // ck, body, 0)

    for h in range(DSA_HEADS):
        qall_s[h * tq:(h + 1) * tq, :] = q_ref[:, h * HEAD_DIM:(h + 1) * HEAD_DIM]
    for h in range(IDX_HEADS):
        iqall_s[h * tq:(h + 1) * tq, :] = iq_ref[:, h * IDX_DIM:(h + 1) * IDX_DIM]

    @pl.when(lo + tq <= top)
    def _():
        def body(c, _):
            bias_s[chunk(c), :] = jnp.where(c * ck + krow <= qcol, 0.0, NEG_INF).astype(F32)
            return 0
        lax.fori_loop(0, nch, body, 0)

    @pl.when(lo + tq > top)
    def _():
        iw_t = ikw_ref[pl.ds(pl.multiple_of(lo, tq), tq), :].T[IDX_DIM:IDX_DIM + IDX_HEADS, :]
        iw_t = iw_t * (IDX_HEADS ** -0.5 * IDX_DIM ** -0.5)

        def score_body(c, _):
            ik = ikw_ref[chunk(c), 0:IDX_DIM].astype(BF16)
            r = _dot_t(ik, iqall_s[...])
            sc = jnp.zeros((ck, tq), F32)
            for h in range(IDX_HEADS):
                sc = sc + iw_t[h:h + 1, :] * jnp.maximum(r[:, h * tq:(h + 1) * tq], 0.0)
            sc = jnp.where(c * ck + krow <= qcol, sc, NEG_INF)
            bits = lax.bitcast_convert_type(sc, I32)
            key_s[chunk(c), :] = bits ^ ((bits >> 31) & 0x7FFFFFFF)
            return 0
        lax.fori_loop(0, nch, score_body, 0)

        def count(pred):
            def body(c, acc):
                ind = pred(key_s[chunk(c), :], c)
                return acc + jnp.sum(ind.reshape(ck // 64, 64, tq), axis=0)
            acc = lax.fori_loop(0, nch, body, jnp.zeros((64, tq), F32))
            return jnp.sum(acc, axis=0, keepdims=True)

        def count_ge(cand):
            return count(lambda kc, c: jnp.where(kc >= cand, 1.0, 0.0))

        topf = float(top)
        tau = jnp.where(count_ge(jnp.zeros((1, tq), I32)) >= topf, 0, INT_MIN).astype(I32)

        def search(i, tau):
            cand = tau | jnp.left_shift(jnp.int32(1), 30 - i)
            return jnp.where(count_ge(cand) >= topf, cand, tau)
        tau = lax.fori_loop(0, 31, search, tau)

        need = topf - count_ge(tau + 1)
        j_s[...] = jnp.full(j_s.shape, seq, I32)

        @pl.when(jnp.max(count_ge(tau)) > topf)
        def _():
            def count_eq_lt(J):
                return count(lambda kc, c: jnp.where(kc == tau, jnp.where(c * ck + krow < J, 1.0, 0.0), 0.0))

            nbits = seq.bit_length()

            def tie(i, J):
                cand = J + jnp.left_shift(jnp.int32(1), nbits - 1 - i)
                return jnp.where(count_eq_lt(cand) <= need, cand, J)
            J = lax.fori_loop(0, nbits, tie, jnp.zeros((1, tq), I32))
            j_s[...] = jnp.broadcast_to(J, j_s.shape)

        J = j_s[0:1, :]

        def bias_body(c, _):
            kc = key_s[chunk(c), :]
            sel = jnp.where(kc > tau, 0.0,
                            jnp.where(kc == tau, jnp.where(c * ck + krow < J, 0.0, NEG_INF), NEG_INF))
            bias_s[chunk(c), :] = sel.astype(F32)
            return 0
        lax.fori_loop(0, nch, bias_body, 0)

    acc_s[...] = jnp.zeros_like(acc_s)

    def step(c, carry):
        ms, ls = carry
        kc = kv_ref[chunk(c), 0:HEAD_DIM]
        bias = bias_s[chunk(c), :]
        vt = kvt_s[HEAD_DIM:2 * HEAD_DIM, chunk(c)]
        new_ms, new_ls, alphas = [], [], []
        for h in range(DSA_HEADS):
            if h % 2 == 0:
                s_pair = _dot_t(kc, qall_s[h * tq:(h + 2) * tq, :])
            s = s_pair[:, (h % 2) * tq:(h % 2 + 1) * tq] + bias
            m_new = jnp.maximum(ms[h], jnp.max(s, axis=0, keepdims=True))
            alpha = jnp.exp(ms[h] - m_new)
            p = jnp.exp(s - m_new)
            new_ls.append(alpha * ls[h] + jnp.sum(p, axis=0, keepdims=True))
            new_ms.append(m_new)
            alphas.append(alpha)
            p_s[h] = p.astype(BF16)
        for h in range(DSA_HEADS):
            sl = slice(h * HEAD_DIM, (h + 1) * HEAD_DIM)
            acc_s[sl, :] = alphas[h] * acc_s[sl, :] + _dot(vt, p_s[h])
        return tuple(new_ms), tuple(new_ls)

    init = (tuple(jnp.full((1, tq), M_FLOOR, F32) for _ in range(DSA_HEADS)),
            tuple(jnp.zeros((1, tq), F32) for _ in range(DSA_HEADS)))
    _, ls = lax.fori_loop(0, nch, step, init)
    for h in range(DSA_HEADS):
        sl = slice(h * HEAD_DIM, (h + 1) * HEAD_DIM)
        acc_s[sl, :] = acc_s[sl, :] / ls[h]
    o_ref[...] = acc_s[...].T.astype(o_ref.dtype)


def _dsa_attn(q, kv, iq, ikw, batch, seq, tq=LANES, ck=256):
    T = q.shape[0]
    nq = seq // tq
    top = min(TOPK_MAX, seq // 4)
    assert tq == LANES and top % tq == 0 and seq % ck == 0
    kern = functools.partial(_dsa_kernel, tq=tq, ck=ck, top=top, seq=seq)
    W = DSA_HEADS * HEAD_DIM
    return pl.pallas_call(
        kern, grid=(batch, nq),
        in_specs=[pl.BlockSpec((tq, W), lambda b, i: (b * nq + i, 0)),
                  pl.BlockSpec((tq, IDX_HEADS * IDX_DIM), lambda b, i: (b * nq + i, 0)),
                  pl.BlockSpec((seq, LANES), lambda b, i: (b, 0)),
                  pl.BlockSpec((seq, LANES), lambda b, i: (b, 0))],
        out_specs=pl.BlockSpec((tq, W), lambda b, i: (b * nq + i, 0)),
        out_shape=jax.ShapeDtypeStruct((T, W), BF16),
        scratch_shapes=[pltpu.VMEM((DSA_HEADS * tq, HEAD_DIM), BF16),
                        pltpu.VMEM((IDX_HEADS * tq, IDX_DIM), BF16),
                        pltpu.VMEM((LANES, seq), BF16),
                        pltpu.VMEM((seq, tq), I32),
                        pltpu.VMEM((seq, tq), F32),
                        pltpu.VMEM((8, tq), I32),
                        pltpu.VMEM((W, tq), F32),
                        pltpu.VMEM((DSA_HEADS, ck, tq), BF16)],
        compiler_params=_params("arbitrary", "arbitrary"), name="dsa_attn")(q, iq, kv, ikw)


def _mix_out_kernel(x_ref, oa_ref, ob_ref, oc_ref, wg_ref, bg_ref, wa_ref, wb_ref, wc_ref, wo_ref,
                    lg_ref, lb_ref, o_ref, *, alpha, d_model):
    x = x_ref[...]
    xb = x.astype(BF16)
    merged = None
    for br, (o_r, w_r) in enumerate(((oa_ref, wa_ref), (ob_ref, wb_ref), (oc_ref, wc_ref))):
        sl = slice(br * d_model, (br + 1) * d_model)
        gate = jax.nn.sigmoid(_dot(xb, wg_ref[:, sl]) + bg_ref[:, sl])
        term = gate * _dot(o_r[...], w_r[...])
        merged = term if merged is None else merged + term
    mix = _dot(merged.astype(BF16), wo_ref[...])
    o_ref[...] = _layer_norm(alpha * x + mix, lg_ref[...], lb_ref[...])


def _mix_out(x, oa, ob, oc, wg, bg, wa, wb, wc, wo, lg, lb, alpha, tm=256):
    T, D = x.shape
    const = lambda i: (0, 0)
    full = lambda a: pl.BlockSpec(a.shape, const)
    row = lambda a: pl.BlockSpec((tm, a.shape[1]), lambda i: (i, 0))
    kern = functools.partial(_mix_out_kernel, alpha=alpha, d_model=D)
    return pl.pallas_call(
        kern, grid=(T // tm,),
        in_specs=[row(x), row(oa), row(ob), row(oc), full(wg), full(bg), full(wa), full(wb), full(wc),
                  full(wo), full(lg), full(lb)],
        out_specs=pl.BlockSpec((tm, D), lambda i: (i, 0)),
        out_shape=jax.ShapeDtypeStruct((T, D), F32),
        compiler_params=_params("parallel"), name="mix_out")(x, oa, ob, oc, wg, bg, wa, wb, wc, wo, lg, lb)


def _route(x, wr_hi_ref, wr_lo_ref, br_ref):
    xh = x.astype(BF16)
    xl = (x - xh.astype(F32)).astype(BF16)
    logits = _dot(xh, wr_hi_ref[...]) + _dot(xl, wr_hi_ref[...]) + _dot(xh, wr_lo_ref[...]) + br_ref[...]
    lane = lax.broadcasted_iota(I32, logits.shape, 1).astype(F32)
    none = float(LANES)
    glog = jnp.where(lane < N_GROUPS, logits, NEG_INF)
    gmax = jnp.max(glog, axis=1, keepdims=True)
    g_p = 1.0 / jnp.sum(jnp.exp(glog - gmax), axis=1, keepdims=True)
    g_idx = jnp.min(jnp.where(glog == gmax, lane, none), axis=1, keepdims=True)
    first = N_GROUPS + g_idx * EXPERTS_PER_GROUP
    sub = jnp.where(lane >= first, jnp.where(lane < first + EXPERTS_PER_GROUP, logits, NEG_INF), NEG_INF)
    v1 = jnp.max(sub, axis=1, keepdims=True)
    i1 = jnp.min(jnp.where(sub == v1, lane, none), axis=1, keepdims=True)
    sub2 = jnp.where(lane == i1, NEG_INF, sub)
    v2 = jnp.max(sub2, axis=1, keepdims=True)
    i2 = jnp.min(jnp.where(sub2 == v2, lane, none), axis=1, keepdims=True)
    e2 = jnp.exp(v2 - v1)
    w1 = g_p / (1.0 + e2)
    w2 = g_p * e2 / (1.0 + e2)
    return jnp.where(lane == i1, w1, jnp.where(lane == i2, w2, 0.0)), g_idx


def _split3(v):
    hi = v.astype(BF16)
    r = v - hi.astype(F32)
    mid = r.astype(BF16)
    return hi, mid, (r - mid.astype(F32)).astype(BF16)


def _moe_kernel(x_ref, wrh_ref, wrl_ref, br_ref, w1_ref, w3_ref, w2_ref, lg_ref, lb_ref, o_ref,
                xs_s, combs_s, pt_s, acc_s, rng_s, *, alpha, tm, rb, ts):
    e = pl.program_id(1)

    @pl.when(e == 0)
    def _():
        x = x_ref[...]
        comb, g_idx = _route(x, wrh_ref, wrl_ref, br_ref)
        lane = lax.broadcasted_iota(I32, (tm, LANES), 1).astype(F32)
        onehot_g = jnp.where(lane == g_idx, 1.0, 0.0)
        ti = lax.broadcasted_iota(I32, (tm, tm), 0)
        tj = lax.broadcasted_iota(I32, (tm, tm), 1)
        earlier = jnp.where(tj < ti, 1.0, 0.0).astype(BF16)
        rank = _dot(earlier, onehot_g.astype(BF16))
        counts = jnp.sum(onehot_g, axis=0, keepdims=True)
        padded = jnp.ceil(counts * (1.0 / ROW_ALIGN)) * ROW_ALIGN
        lane1 = lax.broadcasted_iota(I32, (1, LANES), 1)
        off = jnp.zeros((1, LANES), F32)
        for k in range(1, N_GROUPS):
            off = off + jnp.where(lane1 >= k, pltpu.roll(padded, k, 1), 0.0)
        pos = jnp.sum(onehot_g * (off + rank), axis=1, keepdims=True)
        pos_row = jnp.broadcast_to(pos, (tm, LANES)).T[0:1, :]
        sj = lax.broadcasted_iota(I32, (tm, ts), 1)
        si = lax.broadcasted_iota(I32, (ts, tm), 0)
        pt_s[...] = jnp.where(sj.astype(F32) == pos, 1.0, 0.0).astype(BF16)
        perm = jnp.where(si.astype(F32) == pos_row, 1.0, 0.0).astype(BF16)
        xs_s[...] = _dot(perm, x.astype(BF16)).astype(BF16)
        c_hi, c_mid, c_lo = _split3(comb)
        combs_s[...] = _dot(perm, c_hi) + _dot(perm, c_mid) + _dot(perm, c_lo)
        acc_s[...] = jnp.zeros_like(acc_s)
        for g in range(N_GROUPS):
            start = jnp.sum(jnp.where(lane1 == g, off, 0.0)).astype(I32)
            cnt = jnp.sum(jnp.where(lane1 == g, counts, 0.0)).astype(I32)
            rng_s[g] = start
            rng_s[N_GROUPS + g] = (cnt + rb - 1) // rb

    g = e // EXPERTS_PER_GROUP
    lane_b = lax.broadcasted_iota(I32, (rb, LANES), 1)
    start = rng_s[g]

    def block(b, _):
        rows = pl.ds(pl.multiple_of(start + b * rb, ROW_ALIGN), rb)
        xb = xs_s[rows, :]
        we = jnp.sum(jnp.where(lane_b == N_GROUPS + e, combs_s[rows, :], 0.0), axis=1, keepdims=True)
        a = _dot(xb, w1_ref[...])
        b3 = _dot(xb, w3_ref[...])
        hid = (a * jax.nn.sigmoid(a) * b3).astype(BF16)
        acc_s[rows, :] += we * _dot(hid, w2_ref[...])
        return 0
    lax.fori_loop(0, rng_s[N_GROUPS + g], block, 0)

    @pl.when(e == N_EXPERTS - 1)
    def _():
        y = _dot(pt_s[...], acc_s[...].astype(BF16))
        o_ref[...] = _layer_norm(alpha * x_ref[...] + y, lg_ref[...], lb_ref[...])


def _moe(x, wrh, wrl, br, w1, w3, w2, lg, lb, alpha, tm=1024, rb=320):
    T, D = x.shape
    const = lambda i, e: (0, 0)
    full = lambda a: pl.BlockSpec(a.shape, const)
    ts = -(-(tm + N_GROUPS * ROW_ALIGN + rb) // LANES) * LANES
    kern = functools.partial(_moe_kernel, alpha=alpha, tm=tm, rb=rb, ts=ts)
    return pl.pallas_call(
        kern, grid=(T // tm, N_EXPERTS),
        in_specs=[pl.BlockSpec((tm, D), lambda i, e: (i, 0)), full(wrh), full(wrl), full(br),
                  pl.BlockSpec((None, D, D_EXPERT), lambda i, e: (e, 0, 0)),
                  pl.BlockSpec((None, D, D_EXPERT), lambda i, e: (e, 0, 0)),
                  pl.BlockSpec((None, D_EXPERT, D), lambda i, e: (e, 0, 0)),
                  full(lg), full(lb)],
        out_specs=pl.BlockSpec((tm, D), lambda i, e: (i, 0)),
        out_shape=jax.ShapeDtypeStruct((T, D), F32),
        scratch_shapes=[pltpu.VMEM((ts, D), BF16),
                        pltpu.VMEM((ts, LANES), F32),
                        pltpu.VMEM((tm, ts), BF16),
                        pltpu.VMEM((ts, D), F32),
                        pltpu.SMEM((2 * N_GROUPS,), I32)],
        compiler_params=_params("parallel", "arbitrary"), name="moe")(x, wrh, wrl, br, w1, w3, w2, lg, lb)


def _rope_tables(seq):
    pos = jnp.arange(seq, dtype=F32)[:, None]
    one = lambda n: jnp.ones((seq, n), F32)
    zero = lambda n: jnp.zeros((seq, n), F32)

    inv_p = ROPE_THETA ** (-jnp.arange(0, ROT_DIM, 2, dtype=F32) / ROT_DIM)
    cp, sp = jnp.cos(pos * inv_p), jnp.sin(pos * inv_p)
    hp = ROT_DIM // 2
    rest = HEAD_DIM - ROT_DIM
    head = (jnp.concatenate([cp, cp, one(rest)], 1),
            jnp.concatenate([zero(hp), sp, zero(rest)], 1),
            jnp.concatenate([-sp, zero(hp), zero(rest)], 1))
    ident = (one(HEAD_DIM), zero(HEAD_DIM), zero(HEAD_DIM))
    both = jnp.stack([jnp.concatenate([a, a], 1) for a in head])
    first = jnp.stack([jnp.concatenate([a, b], 1) for a, b in zip(head, ident)])
    tab_p = jnp.stack([both, first])

    inv_m = ROPE_THETA ** (-jnp.arange(0, MLA_ROPE, 2, dtype=F32) / MLA_ROPE)
    cm, sm = jnp.cos(pos * inv_m), jnp.sin(pos * inv_m)
    hm = MLA_ROPE // 2
    pad = LANES - MLA_NOPE - MLA_ROPE
    tab_m = jnp.stack([jnp.concatenate([one(MLA_NOPE), cm, cm, one(pad)], 1),
                       jnp.concatenate([zero(MLA_NOPE), zero(hm), sm, zero(pad)], 1),
                       jnp.concatenate([zero(MLA_NOPE), -sm, zero(hm), zero(pad)], 1)])[None]
    return tab_p, tab_m


def _layer_weights(p, l):
    w_in = p["w_in"][l]
    D = w_in.shape[0]
    z = lambda n: jnp.zeros((D, n), F32)
    o = 0
    cuts = []
    for n in (MLA_Q_RANK, MLA_KV_RANK, MLA_ROPE, 3 * 3 * DIL_HEADS * HEAD_DIM, DSA_HEADS * HEAD_DIM,
              HEAD_DIM, HEAD_DIM, IDX_HEADS * IDX_DIM, IDX_DIM, IDX_HEADS):
        cuts.append(w_in[:, o:o + n])
        o += n
    w_cq, w_ckv, w_kr, w_dil, w_q, w_k, w_v, w_iq, w_ik, w_iw = cuts
    qs = HEAD_DIM ** -0.5
    w_mla = jnp.concatenate([w_cq, w_ckv, z(MLA_NOPE), w_kr, z(LANES - MLA_NOPE - MLA_ROPE)], 1)
    HW = DIL_HEADS * HEAD_DIM
    w_dil = w_dil.reshape(D, len(DIL_GROUPS), 3, HW)
    w_dqkv = jnp.concatenate([w_dil[:, :, 0] * qs, w_dil[:, :, 1], w_dil[:, :, 2]], axis=-1)
    w_dqkv = w_dqkv.transpose(1, 0, 2)
    w_dsa = jnp.concatenate([w_q * qs, w_k, w_v, w_iq, w_ik, w_iw, z(LANES - IDX_DIM - IDX_HEADS)], 1)

    def per_head(w, n_in, n_keep_lo, n_keep_hi):
        r = w.shape[0]
        w = w.reshape(r, MLA_HEADS, n_in)[:, :, n_keep_lo:n_keep_hi]
        w = jnp.pad(w, ((0, 0), (0, 0), (0, LANES - (n_keep_hi - n_keep_lo))))
        return w.reshape(r, MLA_HEADS * LANES)

    w_uq = per_head(p["w_uq"][l], MLA_NOPE + MLA_ROPE, 0, MLA_NOPE + MLA_ROPE)
    w_uk = per_head(p["w_ukv"][l], MLA_NOPE + MLA_V, 0, MLA_NOPE)
    w_uv = p["w_ukv"][l].reshape(MLA_KV_RANK, MLA_HEADS, MLA_NOPE + MLA_V)[:, :, MLA_NOPE:]
    w_uv = w_uv.reshape(MLA_KV_RANK, MLA_HEADS * MLA_V).T

    w_r = jnp.concatenate([p["w_group"][l], p["w_sub"][l], z(LANES - N_GROUPS - N_EXPERTS)], 1)
    w_r_hi = w_r.astype(BF16)
    w_r_lo = (w_r - w_r_hi.astype(F32)).astype(BF16)
    b_r = jnp.concatenate([p["b_group"][l], p["b_sub"][l], jnp.zeros((LANES - N_GROUPS - N_EXPERTS,), F32)])

    bf = lambda a: a.astype(BF16)
    return dict(
        w_mla=bf(w_mla), w_dqkv=bf(w_dqkv), w_dsa=bf(w_dsa),
        q_g=p["q_norm_g"][l][None], kv_g=p["kv_norm_g"][l][None],
        w_uq=bf(w_uq), w_uk=bf(w_uk), w_uv=bf(w_uv),
        w_gate=bf(p["w_gate"][l]), b_gate=p["b_gate"][l][None],
        w_a=bf(p["w_a"][l]), w_b=bf(p["w_b"][l]), w_c=bf(p["w_c"][l]), w_o=bf(p["w_o"][l]),
        ln1_g=p["ln1_g"][l][None], ln1_b=p["ln1_b"][l][None],
        w_r_hi=w_r_hi, w_r_lo=w_r_lo, b_r=b_r[None],
        w1=bf(p["w1"][l]), w3=bf(p["w3"][l]), w2=bf(p["w2"][l]),
        ln2_g=p["ln2_g"][l][None], ln2_b=p["ln2_b"][l][None])


def _layer(xt, w, tab_p, tab_m, batch, seq, alpha):
    HW = DIL_HEADS * HEAD_DIM
    (grp,) = _proj(xt, w["w_mla"], seq, tn=w["w_mla"].shape[1], segs=((0, w["w_mla"].shape[1]),),
                   dtypes=(F32,), kinds=(-1,) * (w["w_mla"].shape[1] // LANES))
    dil_o, dil_lse = [], []
    for g, (window, d) in enumerate(DIL_GROUPS):
        qkv_g = _dil_proj(xt, w["w_dqkv"][g], tab_p[:1], batch, seq, d)
        o_g, lse_g = _dil_lat_attn(qkv_g, seq, d, band=window // d)
        dil_o.append(o_g)
        dil_lse.append(lse_g)
    qw = DSA_HEADS * HEAD_DIM
    iw_ = IDX_HEADS * IDX_DIM
    q_c, kv_c, iq_c, ikw_c = _proj(
        xt, w["w_dsa"], seq, tn=w["w_dsa"].shape[1],
        segs=((0, qw), (qw, LANES), (qw + LANES, iw_), (qw + LANES + iw_, LANES)),
        dtypes=(BF16, BF16, BF16, F32),
        kinds=(0,) * (qw // LANES) + (1,) + (0,) * (iw_ // LANES) + (1,), tab=tab_p)

    q_a, k_a, v_a = _mla_up(grp, w["q_g"], w["kv_g"], w["w_uq"], w["w_uk"], w["w_uv"], tab_m, seq)
    o_a = _mla_attn(q_a, k_a, v_a, batch, seq)
    o_b = _dil_merge(dil_o, dil_lse, batch, seq)
    o_c = _dsa_attn(q_c, kv_c, iq_c, ikw_c, batch, seq)

    x1 = _mix_out(xt, o_a, o_b, o_c, w["w_gate"], w["b_gate"], w["w_a"], w["w_b"], w["w_c"], w["w_o"],
                  w["ln1_g"], w["ln1_b"], alpha)
    return _moe(x1, w["w_r_hi"], w["w_r_lo"], w["b_r"], w["w1"], w["w3"], w["w2"],
                w["ln2_g"], w["ln2_b"], alpha)


def kernel(x, w_in, q_norm_g, w_uq, kv_norm_g, w_ukv, w_gate, b_gate, w_a, w_b, w_c, w_o, ln1_g, ln1_b,
           w_group, b_group, w_sub, b_sub, w1, w3, w2, ln2_g, ln2_b):
    batch, seq, d_model = x.shape
    depth = w_in.shape[0]
    alpha = (2 * depth) ** 0.25
    p = dict(w_in=w_in, q_norm_g=q_norm_g, w_uq=w_uq, kv_norm_g=kv_norm_g, w_ukv=w_ukv, w_gate=w_gate,
             b_gate=b_gate, w_a=w_a, w_b=w_b, w_c=w_c, w_o=w_o, ln1_g=ln1_g, ln1_b=ln1_b, w_group=w_group,
             b_group=b_group, w_sub=w_sub, b_sub=b_sub, w1=w1, w3=w3, w2=w2, ln2_g=ln2_g, ln2_b=ln2_b)
    tab_p, tab_m = _rope_tables(seq)
    xt = x.reshape(batch * seq, d_model)
    for l in range(depth):
        xt = _layer(xt, _layer_weights(p, l), tab_p, tab_m, batch, seq, alpha)
    return xt.reshape(batch, seq, d_model)
```

```python
import functools

import jax
import jax.numpy as jnp
from jax import lax
from jax.experimental import pallas as pl
from jax.experimental.pallas import tpu as pltpu

F32 = jnp.float32
BF16 = jnp.bfloat16
I32 = jnp.int32

HEAD_DIM = 64
ROT_DIM = HEAD_DIM // 4
ROPE_THETA = 500000.0
EPS = 1e-6
MLA_HEADS = 8
MLA_Q_RANK = 256
MLA_KV_RANK = 128
MLA_NOPE = 64
MLA_ROPE = 32
MLA_V = 64
DIL_GROUPS = ((128, 1), (512, 4), (2048, 16))
DIL_HEADS = 4
DSA_HEADS = 8
IDX_HEADS = 8
IDX_DIM = 64
TOPK_MAX = 256
N_GROUPS = 4
EXPERTS_PER_GROUP = 8
N_EXPERTS = N_GROUPS * EXPERTS_PER_GROUP
D_EXPERT = 256
N_BRANCH = 3

LANES = 128
VMEM_LIMIT = 56 * 1024 * 1024
NEG_INF = float("-inf")
INT_MIN = -2 ** 31
M_FLOOR = -1e30
LOG2E = 1.4426950408889634
ROW_ALIGN = 16


def _params(*sem):
    return pltpu.CompilerParams(dimension_semantics=sem, vmem_limit_bytes=VMEM_LIMIT)


def _dot(a, b):
    return jnp.dot(a, b, preferred_element_type=F32)


def _dot_t(a, b):
    return lax.dot_general(a, b, (((1,), (1,)), ((), ())), preferred_element_type=F32)


def _rope_chunk(v, tab_ref, kind, shift):
    c = tab_ref[kind, 0]
    s1 = tab_ref[kind, 1]
    s2 = tab_ref[kind, 2]
    return v * c + pltpu.roll(v, shift, 1) * s1 + pltpu.roll(v, LANES - shift, 1) * s2


def _layer_norm(z, g, b):
    mu = jnp.mean(z, axis=-1, keepdims=True)
    zc = z - mu
    var = jnp.mean(zc * zc, axis=-1, keepdims=True)
    return zc * lax.rsqrt(var + EPS) * g + b


def _proj_kernel(*refs, segs, kinds, dils, shift):
    x_ref, w_ref, tab_ref = refs[:3]
    outs, scr = refs[3:3 + len(segs)], refs[3 + len(segs)]
    xb = x_ref[...].astype(BF16)
    tm = xb.shape[0]
    for (c0, width), d, o_ref in zip(segs, dils, outs):
        acc = _dot(xb, w_ref[:, c0:c0 + width])
        nch = width // LANES
        for c in range(nch):
            sl = slice(c * LANES, (c + 1) * LANES)
            v = acc[:, sl]
            kind = kinds[c0 // LANES + c]
            if kind >= 0:
                v = _rope_chunk(v, tab_ref, kind, shift)
            if d == 1:
                o_ref[:, sl] = v.astype(o_ref.dtype)
            else:
                scr[c] = v
        if d > 1:
            for r in range(d):
                for c in range(nch):
                    o_ref[r, :, c * LANES:(c + 1) * LANES] = (
                        scr[c, pl.ds(r, tm // d, stride=d), :].astype(o_ref.dtype))


def _proj(x, w, tab, batch, seq, *, segs, dtypes, kinds, dils, shift=ROT_DIM // 2, tm=256):
    T, K = x.shape
    spt = seq // tm
    out_specs, out_shape = [], []
    for (_, wd), dt, d in zip(segs, dtypes, dils):
        if d == 1:
            out_specs.append(pl.BlockSpec((tm, wd), lambda i: (i, 0)))
            out_shape.append(jax.ShapeDtypeStruct((T, wd), dt))
        else:
            out_specs.append(pl.BlockSpec((None, d, tm // d, wd), lambda i: (i // spt, 0, i % spt, 0)))
            out_shape.append(jax.ShapeDtypeStruct((batch, d, seq // d, wd), dt))
    kern = functools.partial(_proj_kernel, segs=segs, kinds=kinds, dils=dils, shift=shift)
    outs = pl.pallas_call(
        kern, grid=(T // tm,),
        in_specs=[pl.BlockSpec((tm, K), lambda i: (i, 0)), pl.BlockSpec(w.shape, lambda i: (0, 0)),
                  pl.BlockSpec((tab.shape[0], 3, tm, LANES), lambda i: (0, 0, i % spt, 0))],
        out_specs=out_specs, out_shape=out_shape,
        scratch_shapes=[pltpu.VMEM((max(wd for _, wd in segs) // LANES, tm, LANES), F32)],
        compiler_params=_params("parallel"), name="in_proj")(x, w, tab)
    return [o.reshape(T, o.shape[-1]) for o in outs]


def _mla_up_kernel(g_ref, qg_ref, kvg_ref, wq_ref, wk_ref, wv_ref, tab_ref, q_out, k_out, v_out):
    g = g_ref[...]
    cq = g[:, :MLA_Q_RANK]
    ckv = g[:, MLA_Q_RANK:MLA_Q_RANK + MLA_KV_RANK]
    kr = g[:, MLA_Q_RANK + MLA_KV_RANK:]
    cqn = (cq * lax.rsqrt(jnp.mean(cq * cq, axis=-1, keepdims=True) + EPS) * qg_ref[...]).astype(BF16)
    ckvn = (ckv * lax.rsqrt(jnp.mean(ckv * ckv, axis=-1, keepdims=True) + EPS) * kvg_ref[...]).astype(BF16)
    q = _dot(cqn, wq_ref[...])
    k = _dot(ckvn, wk_ref[...])
    v_out[...] = _dot_t(wv_ref[...], ckvn).astype(v_out.dtype)
    for h in range(MLA_HEADS):
        sl = slice(h * LANES, (h + 1) * LANES)
        q_out[:, sl] = _rope_chunk(q[:, sl], tab_ref, 0, MLA_ROPE // 2).astype(q_out.dtype)
        k_out[:, sl] = _rope_chunk(k[:, sl] + kr, tab_ref, 0, MLA_ROPE // 2).astype(k_out.dtype)


def _mla_up(grp, qg, kvg, wq, wk, wv, tab, seq, tm=512):
    T = grp.shape[0]
    spt = seq // tm
    const = lambda i: (0, 0)
    return pl.pallas_call(
        _mla_up_kernel, grid=(T // tm,),
        in_specs=[pl.BlockSpec((tm, grp.shape[1]), lambda i: (i, 0)),
                  pl.BlockSpec(qg.shape, const), pl.BlockSpec(kvg.shape, const),
                  pl.BlockSpec(wq.shape, const), pl.BlockSpec(wk.shape, const), pl.BlockSpec(wv.shape, const),
                  pl.BlockSpec((1, 3, tm, LANES), lambda i: (0, 0, i % spt, 0))],
        out_specs=[pl.BlockSpec((tm, MLA_HEADS * LANES), lambda i: (i, 0)),
                   pl.BlockSpec((tm, MLA_HEADS * LANES), lambda i: (i, 0)),
                   pl.BlockSpec((MLA_HEADS * MLA_V, tm), lambda i: (0, i))],
        out_shape=[jax.ShapeDtypeStruct((T, MLA_HEADS * LANES), BF16),
                   jax.ShapeDtypeStruct((T, MLA_HEADS * LANES), BF16),
                   jax.ShapeDtypeStruct((MLA_HEADS * MLA_V, T), BF16)],
        compiler_params=_params("parallel"), name="mla_up")(grp, qg, kvg, wq, wk, wv, tab)


def _mla_attn_kernel(q_ref, k_ref, vt_ref, o_ref, acc_s, s_s, p_s, *, tq, c_exp):
    qi = pl.program_id(1)
    krow = lax.broadcasted_iota(I32, (tq, tq), 0)
    qcol = lax.broadcasted_iota(I32, (tq, tq), 1)
    diag_bias = jnp.where(krow <= qcol, 0.0, NEG_INF).astype(F32)
    acc_s[...] = jnp.zeros_like(acc_s)

    def step(j, carry, masked):
        ms, ls = carry
        ks = pl.multiple_of(j * tq, tq)
        for h in range(MLA_HEADS):
            kb = k_ref[pl.ds(ks, tq), h * LANES:(h + 1) * LANES]
            s_s[h] = _dot_t(kb, q_ref[:, h * LANES:(h + 1) * LANES])
        new_ms, new_ls, alphas = [], [], []
        for h in range(MLA_HEADS):
            s = s_s[h]
            if masked:
                s = s + diag_bias
            m_new = jnp.maximum(ms[h], jnp.max(s, axis=0, keepdims=True))
            alpha = jnp.exp2((ms[h] - m_new) * c_exp)
            p = jnp.exp2((s - m_new) * c_exp)
            new_ls.append(alpha * ls[h] + jnp.sum(p, axis=0, keepdims=True))
            new_ms.append(m_new)
            alphas.append(alpha)
            p_s[h] = p.astype(BF16)
        for h in range(MLA_HEADS):
            sl = slice(h * MLA_V, (h + 1) * MLA_V)
            acc_s[sl, :] = alphas[h] * acc_s[sl, :] + _dot(vt_ref[sl, pl.ds(ks, tq)], p_s[h])
        return tuple(new_ms), tuple(new_ls)

    init = (tuple(jnp.full((1, tq), NEG_INF, F32) for _ in range(MLA_HEADS)),
            tuple(jnp.zeros((1, tq), F32) for _ in range(MLA_HEADS)))
    carry = lax.fori_loop(0, qi, functools.partial(step, masked=False), init)
    _, ls = step(qi, carry, True)
    for h in range(MLA_HEADS):
        sl = slice(h * MLA_V, (h + 1) * MLA_V)
        acc_s[sl, :] = acc_s[sl, :] / ls[h]
    o_ref[...] = acc_s[...].T.astype(o_ref.dtype)


def _mla_attn(q, k, vt, batch, seq, tq=256):
    T = q.shape[0]
    nq = seq // tq
    c_exp = (MLA_NOPE + MLA_ROPE) ** -0.5 * LOG2E
    kern = functools.partial(_mla_attn_kernel, tq=tq, c_exp=c_exp)
    W = MLA_HEADS * LANES
    return pl.pallas_call(
        kern, grid=(batch, nq),
        in_specs=[pl.BlockSpec((tq, W), lambda b, i: (b * nq + i, 0)),
                  pl.BlockSpec((seq, W), lambda b, i: (b, 0)),
                  pl.BlockSpec((MLA_HEADS * MLA_V, seq), lambda b, i: (0, b))],
        out_specs=pl.BlockSpec((tq, MLA_HEADS * MLA_V), lambda b, i: (b * nq + i, 0)),
        out_shape=jax.ShapeDtypeStruct((T, MLA_HEADS * MLA_V), BF16),
        scratch_shapes=[pltpu.VMEM((MLA_HEADS * MLA_V, tq), F32),
                        pltpu.VMEM((MLA_HEADS, tq, tq), F32), pltpu.VMEM((MLA_HEADS, tq, tq), BF16)],
        compiler_params=_params("parallel", "arbitrary"), name="mla_attn")(q, k, vt)


def _dil_lat_kernel(qkv_ref, halo_ref, o_ref, lse_ref, s_s, p_s, *, band, nsub, sub_per_seq):
    i = pl.program_id(0)
    HW = DIL_HEADS * HEAD_DIM
    v_t = lambda ref, rows: ref[rows, 2 * HW:3 * HW].astype(F32).T.astype(BF16)
    vts = [v_t(qkv_ref, slice(u * band, (u + 1) * band)) for u in range(nsub)]
    qk_ref, qkh_ref = qkv_ref, halo_ref
    kj = lax.broadcasted_iota(I32, (band, band), 0)
    qi = lax.broadcasted_iota(I32, (band, band), 1)
    bias_cur = jnp.where(kj <= qi, 0.0, NEG_INF).astype(F32)
    bias_prev = jnp.where(kj >= qi, 0.0, NEG_INF).astype(F32)
    prevs = []
    for u in range(nsub):
        if sub_per_seq == 1 or (u % sub_per_seq == 0 and nsub % sub_per_seq == 0):
            prevs.append(None)
        elif u > 0:
            prevs.append((qk_ref[(u - 1) * band:u * band, HW:2 * HW], vts[u - 1], bias_prev))
        else:
            has = (i * nsub) % sub_per_seq != 0
            prevs.append((qkh_ref[:, HW:2 * HW], v_t(qkh_ref, slice(0, band)),
                          jnp.where(has, bias_prev, NEG_INF)))

    for u in range(nsub):
        rows = slice(u * band, (u + 1) * band)
        q = qk_ref[rows, 0:HW]
        k_cur = qk_ref[rows, HW:2 * HW]
        for h in range(DIL_HEADS):
            hs = slice(h * HEAD_DIM, (h + 1) * HEAD_DIM)
            s_s[u, h, 0] = _dot_t(k_cur[:, hs], q[:, hs])
            if prevs[u] is not None:
                s_s[u, h, 1] = _dot_t(prevs[u][0][:, hs], q[:, hs])
    stats = {}
    for u in range(nsub):
        for h in range(DIL_HEADS):
            s_c = s_s[u, h, 0] + bias_cur
            m = jnp.max(s_c, axis=0, keepdims=True)
            if prevs[u] is not None:
                s_p = s_s[u, h, 1] + prevs[u][2]
                m = jnp.maximum(m, jnp.max(s_p, axis=0, keepdims=True))
            p_c = jnp.exp(s_c - m)
            l = jnp.sum(p_c, axis=0, keepdims=True)
            p_s[u, h, 0] = p_c.astype(BF16)
            if prevs[u] is not None:
                p_p = jnp.exp(s_p - m)
                l = l + jnp.sum(p_p, axis=0, keepdims=True)
                p_s[u, h, 1] = p_p.astype(BF16)
            stats[u, h] = (m, l)
    pad = jnp.zeros((band - DIL_HEADS, band), F32)
    for u in range(nsub):
        rows = slice(u * band, (u + 1) * band)
        outs, lses = [], []
        for h in range(DIL_HEADS):
            hs = slice(h * HEAD_DIM, (h + 1) * HEAD_DIM)
            m, l = stats[u, h]
            o = _dot(vts[u][hs, :], p_s[u, h, 0])
            if prevs[u] is not None:
                o = o + _dot(prevs[u][1][hs, :], p_s[u, h, 1])
            outs.append(o / l)
            lses.append(m + jnp.log(l))
        o_ref[rows, :] = jnp.concatenate(outs, axis=0).T.astype(o_ref.dtype)
        lse_ref[rows, :] = jnp.concatenate(lses + [pad], axis=0).T


def _dil_lat_attn(qkv, seq, d, band, nsub=4):
    T = qkv.shape[0]
    HW = DIL_HEADS * HEAD_DIM
    M = seq // d
    assert band == LANES and M % band == 0
    sub_per_seq = M // band
    assert sub_per_seq % nsub == 0 or nsub % sub_per_seq == 0
    R = nsub * band
    halo = lambda i: jnp.maximum(i * nsub - 1, 0)
    kern = functools.partial(_dil_lat_kernel, band=band, nsub=nsub, sub_per_seq=sub_per_seq)
    return pl.pallas_call(
        kern, grid=(T // R,),
        in_specs=[pl.BlockSpec((R, 3 * HW), lambda i: (i, 0)),
                  pl.BlockSpec((band, 3 * HW), lambda i: (halo(i), 0))],
        out_specs=[pl.BlockSpec((R, HW), lambda i: (i, 0)), pl.BlockSpec((R, LANES), lambda i: (i, 0))],
        out_shape=[jax.ShapeDtypeStruct((T, HW), BF16), jax.ShapeDtypeStruct((T, LANES), F32)],
        scratch_shapes=[pltpu.VMEM((nsub, DIL_HEADS, 2, band, band), F32),
                        pltpu.VMEM((nsub, DIL_HEADS, 2, band, band), BF16)],
        compiler_params=_params("parallel"), name="dil_lat")(qkv, qkv)


def _dil_merge_kernel(*refs, dils, tmm):
    ng = len(dils)
    o_refs, l_refs = refs[0:2 * ng:2], refs[1:2 * ng:2]
    out_ref, scr = refs[2 * ng], refs[2 * ng + 1]
    nch = DIL_HEADS * HEAD_DIM // LANES
    o_pos, l_pos = [], []
    for g, d in enumerate(dils):
        if d == 1:
            o_pos.append([o_refs[g][:, c * LANES:(c + 1) * LANES].astype(F32) for c in range(nch)])
            l_pos.append(l_refs[g][...])
            continue
        for r in range(d):
            rows = pl.ds(r, tmm // d, stride=d)
            for c in range(nch):
                scr[g, c, rows, :] = o_refs[g][r, :, c * LANES:(c + 1) * LANES].astype(F32)
            scr[g, nch, rows, :] = l_refs[g][r]
        o_pos.append([scr[g, c] for c in range(nch)])
        l_pos.append(scr[g, nch])
    mx = functools.reduce(jnp.maximum, l_pos)
    es = [jnp.exp(l - mx) for l in l_pos]
    den = functools.reduce(lambda a, b: a + b, es)
    hpc = LANES // HEAD_DIM
    for c in range(nch):
        acc = None
        for g in range(ng):
            w = es[g] / den
            wc = jnp.concatenate([jnp.broadcast_to(w[:, c * hpc + k:c * hpc + k + 1], (tmm, HEAD_DIM))
                                  for k in range(hpc)], axis=1)
            term = wc * o_pos[g][c]
            acc = term if acc is None else acc + term
        out_ref[:, c * LANES:(c + 1) * LANES] = acc.astype(out_ref.dtype)


def _dil_merge(os_, ls_, batch, seq, tmm=256):
    dils = tuple(d for _, d in DIL_GROUPS)
    HW = DIL_HEADS * HEAD_DIM
    T = batch * seq
    tps = seq // tmm
    args, in_specs = [], []
    for o, l, d in zip(os_, ls_, dils):
        if d == 1:
            args += [o, l]
            in_specs += [pl.BlockSpec((tmm, HW), lambda i: (i, 0)), pl.BlockSpec((tmm, LANES), lambda i: (i, 0))]
        else:
            M = seq // d
            args += [o.reshape(batch, d, M, HW), l.reshape(batch, d, M, LANES)]
            in_specs += [pl.BlockSpec((None, d, tmm // d, HW), lambda i: (i // tps, 0, i % tps, 0)),
                         pl.BlockSpec((None, d, tmm // d, LANES), lambda i: (i // tps, 0, i % tps, 0))]
    kern = functools.partial(_dil_merge_kernel, dils=dils, tmm=tmm)
    return pl.pallas_call(
        kern, grid=(T // tmm,), in_specs=in_specs,
        out_specs=pl.BlockSpec((tmm, HW), lambda i: (i, 0)),
        out_shape=jax.ShapeDtypeStruct((T, HW), BF16),
        scratch_shapes=[pltpu.VMEM((len(dils), HW // LANES + 1, tmm, LANES), F32)],
        compiler_params=_params("parallel"), name="dil_merge")(*args)


def _dsa_kernel(q_ref, iq_ref, kv_ref, ikw_ref, o_ref, qall_s, iqall_s, kvt_s, key_s, bias_s, j_s, acc_s, p_s,
                *, tq, ck, top, seq):
    qi = pl.program_id(1)
    lo = qi * tq
    nch = (lo + tq + ck - 1) // ck
    krow = lax.broadcasted_iota(I32, (ck, tq), 0)
    qcol = lo + lax.broadcasted_iota(I32, (ck, tq), 1)

    def chunk(c):
        return pl.ds(pl.multiple_of(c * ck, ck), ck)

    @pl.when(qi == 0)
    def _():
        def body(c, _):
            kvt_s[:, chunk(c)] = kv_ref[chunk(c), :].astype(F32).T.astype(BF16)
            return 0
        lax.fori_loop(0, seq // ck, body, 0)

    for h in range(DSA_HEADS):
        qall_s[h * tq:(h + 1) * tq, :] = q_ref[:, h * HEAD_DIM:(h + 1) * HEAD_DIM]
    for h in range(IDX_HEADS):
        iqall_s[h * tq:(h + 1) * tq, :] = iq_ref[:, h * IDX_DIM:(h + 1) * IDX_DIM]

    @pl.when(lo + tq <= top)
    def _():
        def body(c, _):
            bias_s[chunk(c), :] = jnp.where(c * ck + krow <= qcol, 0.0, NEG_INF).astype(F32)
            return 0
        lax.fori_loop(0, nch, body, 0)

    @pl.when(lo + tq > top)
    def _():
        iw_t = ikw_ref[pl.ds(pl.multiple_of(lo, tq), tq), :].T[IDX_DIM:IDX_DIM + IDX_HEADS, :]
        iw_t = iw_t * (IDX_HEADS ** -0.5 * IDX_DIM ** -0.5)

        def score_body(c, _):
            ik = ikw_ref[chunk(c), 0:IDX_DIM].astype(BF16)
            r = _dot_t(ik, iqall_s[...])
            sc = jnp.zeros((ck, tq), F32)
            for h in range(IDX_HEADS):
                sc = sc + iw_t[h:h + 1, :] * jnp.maximum(r[:, h * tq:(h + 1) * tq], 0.0)
            sc = jnp.where(c * ck + krow <= qcol, sc, NEG_INF)
            bits = lax.bitcast_convert_type(sc, I32)
            key_s[chunk(c), :] = bits ^ ((bits >> 31) & 0x7FFFFFFF)
            return 0
        lax.fori_loop(0, nch, score_body, 0)

        def count(pred):
            def body(c, acc):
                ind = pred(key_s[chunk(c), :], c)
                return acc + jnp.sum(ind.reshape(ck // 64, 64, tq), axis=0)
            acc = lax.fori_loop(0, nch, body, jnp.zeros((64, tq), F32))
            return jnp.sum(acc, axis=0, keepdims=True)

        def count_ge(cand):
            return count(lambda kc, c: jnp.where(kc >= cand, 1.0, 0.0))

        topf = float(top)
        tau = jnp.where(count_ge(jnp.zeros((1, tq), I32)) >= topf, 0, INT_MIN).astype(I32)

        def search(i, tau):
            cand = tau | jnp.left_shift(jnp.int32(1), 30 - i)
            return jnp.where(count_ge(cand) >= topf, cand, tau)
        tau = lax.fori_loop(0, 31, search, tau)

        need = topf - count_ge(tau + 1)
        j_s[...] = jnp.full(j_s.shape, seq, I32)

        @pl.when(jnp.max(count_ge(tau)) > topf)
        def _():
            def count_eq_lt(J):
                return count(lambda kc, c: jnp.where(kc == tau, jnp.where(c * ck + krow < J, 1.0, 0.0), 0.0))

            nbits = seq.bit_length()

            def tie(i, J):
                cand = J + jnp.left_shift(jnp.int32(1), nbits - 1 - i)
                return jnp.where(count_eq_lt(cand) <= need, cand, J)
            J = lax.fori_loop(0, nbits, tie, jnp.zeros((1, tq), I32))
            j_s[...] = jnp.broadcast_to(J, j_s.shape)

        J = j_s[0:1, :]

        def bias_body(c, _):
            kc = key_s[chunk(c), :]
            sel = jnp.where(kc > tau, 0.0,
                            jnp.where(kc == tau, jnp.where(c * ck + krow < J, 0.0, NEG_INF), NEG_INF))
            bias_s[chunk(c), :] = sel.astype(F32)
            return 0
        lax.fori_loop(0, nch, bias_body, 0)

    acc_s[...] = jnp.zeros_like(acc_s)

    def step(c, carry):
        ms, ls = carry
        kc = kv_ref[chunk(c), 0:HEAD_DIM]
        bias = bias_s[chunk(c), :]
        vt = kvt_s[HEAD_DIM:2 * HEAD_DIM, chunk(c)]
        new_ms, new_ls, alphas = [], [], []
        for h in range(DSA_HEADS):
            if h % 2 == 0:
                s_pair = _dot_t(kc, qall_s[h * tq:(h + 2) * tq, :])
            s = s_pair[:, (h % 2) * tq:(h % 2 + 1) * tq] + bias
            m_new = jnp.maximum(ms[h], jnp.max(s, axis=0, keepdims=True))
            alpha = jnp.exp(ms[h] - m_new)
            p = jnp.exp(s - m_new)
            new_ls.append(alpha * ls[h] + jnp.sum(p, axis=0, keepdims=True))
            new_ms.append(m_new)
            alphas.append(alpha)
            p_s[h] = p.astype(BF16)
        for h in range(DSA_HEADS):
            sl = slice(h * HEAD_DIM, (h + 1) * HEAD_DIM)
            acc_s[sl, :] = alphas[h] * acc_s[sl, :] + _dot(vt, p_s[h])
        return tuple(new_ms), tuple(new_ls)

    init = (tuple(jnp.full((1, tq), M_FLOOR, F32) for _ in range(DSA_HEADS)),
            tuple(jnp.zeros((1, tq), F32) for _ in range(DSA_HEADS)))
    _, ls = lax.fori_loop(0, nch, step, init)
    for h in range(DSA_HEADS):
        sl = slice(h * HEAD_DIM, (h + 1) * HEAD_DIM)
        acc_s[sl, :] = acc_s[sl, :] / ls[h]
    o_ref[...] = acc_s[...].T.astype(o_ref.dtype)


def _dsa_attn(q, kv, iq, ikw, batch, seq, tq=LANES, ck=256):
    T = q.shape[0]
    nq = seq // tq
    top = min(TOPK_MAX, seq // 4)
    assert tq == LANES and top % tq == 0 and seq % ck == 0
    kern = functools.partial(_dsa_kernel, tq=tq, ck=ck, top=top, seq=seq)
    W = DSA_HEADS * HEAD_DIM
    return pl.pallas_call(
        kern, grid=(batch, nq),
        in_specs=[pl.BlockSpec((tq, W), lambda b, i: (b * nq + i, 0)),
                  pl.BlockSpec((tq, IDX_HEADS * IDX_DIM), lambda b, i: (b * nq + i, 0)),
                  pl.BlockSpec((seq, LANES), lambda b, i: (b, 0)),
                  pl.BlockSpec((seq, LANES), lambda b, i: (b, 0))],
        out_specs=pl.BlockSpec((tq, W), lambda b, i: (b * nq + i, 0)),
        out_shape=jax.ShapeDtypeStruct((T, W), BF16),
        scratch_shapes=[pltpu.VMEM((DSA_HEADS * tq, HEAD_DIM), BF16),
                        pltpu.VMEM((IDX_HEADS * tq, IDX_DIM), BF16),
                        pltpu.VMEM((LANES, seq), BF16),
                        pltpu.VMEM((seq, tq), I32),
                        pltpu.VMEM((seq, tq), F32),
                        pltpu.VMEM((8, tq), I32),
                        pltpu.VMEM((W, tq), F32),
                        pltpu.VMEM((DSA_HEADS, ck, tq), BF16)],
        compiler_params=_params("arbitrary", "arbitrary"), name="dsa_attn")(q, iq, kv, ikw)


def _mix_out_kernel(x_ref, oa_ref, ob_ref, oc_ref, wg_ref, bg_ref, wa_ref, wb_ref, wc_ref, wo_ref,
                    lg_ref, lb_ref, o_ref, *, alpha, d_model):
    x = x_ref[...]
    xb = x.astype(BF16)
    merged = None
    for br, (o_r, w_r) in enumerate(((oa_ref, wa_ref), (ob_ref, wb_ref), (oc_ref, wc_ref))):
        sl = slice(br * d_model, (br + 1) * d_model)
        gate = jax.nn.sigmoid(_dot(xb, wg_ref[:, sl]) + bg_ref[:, sl])
        term = gate * _dot(o_r[...], w_r[...])
        merged = term if merged is None else merged + term
    mix = _dot(merged.astype(BF16), wo_ref[...])
    o_ref[...] = _layer_norm(alpha * x + mix, lg_ref[...], lb_ref[...])


def _mix_out(x, oa, ob, oc, wg, bg, wa, wb, wc, wo, lg, lb, alpha, tm=256):
    T, D = x.shape
    const = lambda i: (0, 0)
    full = lambda a: pl.BlockSpec(a.shape, const)
    row = lambda a: pl.BlockSpec((tm, a.shape[1]), lambda i: (i, 0))
    kern = functools.partial(_mix_out_kernel, alpha=alpha, d_model=D)
    return pl.pallas_call(
        kern, grid=(T // tm,),
        in_specs=[row(x), row(oa), row(ob), row(oc), full(wg), full(bg), full(wa), full(wb), full(wc),
                  full(wo), full(lg), full(lb)],
        out_specs=pl.BlockSpec((tm, D), lambda i: (i, 0)),
        out_shape=jax.ShapeDtypeStruct((T, D), F32),
        compiler_params=_params("parallel"), name="mix_out")(x, oa, ob, oc, wg, bg, wa, wb, wc, wo, lg, lb)


def _route(x, wr_hi_ref, wr_lo_ref, br_ref):
    xh = x.astype(BF16)
    xl = (x - xh.astype(F32)).astype(BF16)
    logits = _dot(xh, wr_hi_ref[...]) + _dot(xl, wr_hi_ref[...]) + _dot(xh, wr_lo_ref[...]) + br_ref[...]
    lane = lax.broadcasted_iota(I32, logits.shape, 1).astype(F32)
    none = float(LANES)
    glog = jnp.where(lane < N_GROUPS, logits, NEG_INF)
    gmax = jnp.max(glog, axis=1, keepdims=True)
    g_p = 1.0 / jnp.sum(jnp.exp(glog - gmax), axis=1, keepdims=True)
    g_idx = jnp.min(jnp.where(glog == gmax, lane, none), axis=1, keepdims=True)
    first = N_GROUPS + g_idx * EXPERTS_PER_GROUP
    sub = jnp.where(lane >= first, jnp.where(lane < first + EXPERTS_PER_GROUP, logits, NEG_INF), NEG_INF)
    v1 = jnp.max(sub, axis=1, keepdims=True)
    i1 = jnp.min(jnp.where(sub == v1, lane, none), axis=1, keepdims=True)
    sub2 = jnp.where(lane == i1, NEG_INF, sub)
    v2 = jnp.max(sub2, axis=1, keepdims=True)
    i2 = jnp.min(jnp.where(sub2 == v2, lane, none), axis=1, keepdims=True)
    e2 = jnp.exp(v2 - v1)
    w1 = g_p / (1.0 + e2)
    w2 = g_p * e2 / (1.0 + e2)
    return jnp.where(lane == i1, w1, jnp.where(lane == i2, w2, 0.0)), g_idx


def _split3(v):
    hi = v.astype(BF16)
    r = v - hi.astype(F32)
    mid = r.astype(BF16)
    return hi, mid, (r - mid.astype(F32)).astype(BF16)


def _moe_kernel(x_ref, wrh_ref, wrl_ref, br_ref, w1_ref, w3_ref, w2_ref, lg_ref, lb_ref, o_ref,
                xs_s, combs_s, pt_s, acc_s, rng_s, *, alpha, tm, rb, ts):
    e = pl.program_id(1)

    @pl.when(e == 0)
    def _():
        x = x_ref[...]
        comb, g_idx = _route(x, wrh_ref, wrl_ref, br_ref)
        lane = lax.broadcasted_iota(I32, (tm, LANES), 1).astype(F32)
        onehot_g = jnp.where(lane == g_idx, 1.0, 0.0)
        ti = lax.broadcasted_iota(I32, (tm, tm), 0)
        tj = lax.broadcasted_iota(I32, (tm, tm), 1)
        earlier = jnp.where(tj < ti, 1.0, 0.0).astype(BF16)
        rank = _dot(earlier, onehot_g.astype(BF16))
        counts = jnp.sum(onehot_g, axis=0, keepdims=True)
        padded = jnp.ceil(counts * (1.0 / ROW_ALIGN)) * ROW_ALIGN
        lane1 = lax.broadcasted_iota(I32, (1, LANES), 1)
        off = jnp.zeros((1, LANES), F32)
        for k in range(1, N_GROUPS):
            off = off + jnp.where(lane1 >= k, pltpu.roll(padded, k, 1), 0.0)
        pos = jnp.sum(onehot_g * (off + rank), axis=1, keepdims=True)
        pos_row = jnp.broadcast_to(pos, (tm, LANES)).T[0:1, :]
        sj = lax.broadcasted_iota(I32, (tm, ts), 1)
        si = lax.broadcasted_iota(I32, (ts, tm), 0)
        pt_s[...] = jnp.where(sj.astype(F32) == pos, 1.0, 0.0).astype(BF16)
        perm = jnp.where(si.astype(F32) == pos_row, 1.0, 0.0).astype(BF16)
        xs_s[...] = _dot(perm, x.astype(BF16)).astype(BF16)
        c_hi, c_mid, c_lo = _split3(comb)
        combs_s[...] = _dot(perm, c_hi) + _dot(perm, c_mid) + _dot(perm, c_lo)
        acc_s[...] = jnp.zeros_like(acc_s)
        for g in range(N_GROUPS):
            start = jnp.sum(jnp.where(lane1 == g, off, 0.0)).astype(I32)
            cnt = jnp.sum(jnp.where(lane1 == g, counts, 0.0)).astype(I32)
            rng_s[g] = start
            rng_s[N_GROUPS + g] = (cnt + rb - 1) // rb

    g = e // EXPERTS_PER_GROUP
    lane_b = lax.broadcasted_iota(I32, (rb, LANES), 1)
    start = rng_s[g]

    def block(b, _):
        rows = pl.ds(pl.multiple_of(start + b * rb, ROW_ALIGN), rb)
        xb = xs_s[rows, :]
        we = jnp.sum(jnp.where(lane_b == N_GROUPS + e, combs_s[rows, :], 0.0), axis=1, keepdims=True)
        a = _dot(xb, w1_ref[...])
        b3 = _dot(xb, w3_ref[...])
        hid = (a * jax.nn.sigmoid(a) * b3).astype(BF16)
        acc_s[rows, :] += we * _dot(hid, w2_ref[...])
        return 0
    lax.fori_loop(0, rng_s[N_GROUPS + g], block, 0)

    @pl.when(e == N_EXPERTS - 1)
    def _():
        y = _dot(pt_s[...], acc_s[...].astype(BF16))
        o_ref[...] = _layer_norm(alpha * x_ref[...] + y, lg_ref[...], lb_ref[...])


def _moe(x, wrh, wrl, br, w1, w3, w2, lg, lb, alpha, tm=1024, rb=320):
    T, D = x.shape
    const = lambda i, e: (0, 0)
    full = lambda a: pl.BlockSpec(a.shape, const)
    ts = -(-(tm + N_GROUPS * ROW_ALIGN + rb) // LANES) * LANES
    kern = functools.partial(_moe_kernel, alpha=alpha, tm=tm, rb=rb, ts=ts)
    return pl.pallas_call(
        kern, grid=(T // tm, N_EXPERTS),
        in_specs=[pl.BlockSpec((tm, D), lambda i, e: (i, 0)), full(wrh), full(wrl), full(br),
                  pl.BlockSpec((None, D, D_EXPERT), lambda i, e: (e, 0, 0)),
                  pl.BlockSpec((None, D, D_EXPERT), lambda i, e: (e, 0, 0)),
                  pl.BlockSpec((None, D_EXPERT, D), lambda i, e: (e, 0, 0)),
                  full(lg), full(lb)],
        out_specs=pl.BlockSpec((tm, D), lambda i, e: (i, 0)),
        out_shape=jax.ShapeDtypeStruct((T, D), F32),
        scratch_shapes=[pltpu.VMEM((ts, D), BF16),
                        pltpu.VMEM((ts, LANES), F32),
                        pltpu.VMEM((tm, ts), BF16),
                        pltpu.VMEM((ts, D), F32),
                        pltpu.SMEM((2 * N_GROUPS,), I32)],
        compiler_params=_params("parallel", "arbitrary"), name="moe")(x, wrh, wrl, br, w1, w3, w2, lg, lb)


def _rope_tables(seq):
    pos = jnp.arange(seq, dtype=F32)[:, None]
    one = lambda n: jnp.ones((seq, n), F32)
    zero = lambda n: jnp.zeros((seq, n), F32)

    inv_p = ROPE_THETA ** (-jnp.arange(0, ROT_DIM, 2, dtype=F32) / ROT_DIM)
    cp, sp = jnp.cos(pos * inv_p), jnp.sin(pos * inv_p)
    hp = ROT_DIM // 2
    rest = HEAD_DIM - ROT_DIM
    head = (jnp.concatenate([cp, cp, one(rest)], 1),
            jnp.concatenate([zero(hp), sp, zero(rest)], 1),
            jnp.concatenate([-sp, zero(hp), zero(rest)], 1))
    ident = (one(HEAD_DIM), zero(HEAD_DIM), zero(HEAD_DIM))
    both = jnp.stack([jnp.concatenate([a, a], 1) for a in head])
    first = jnp.stack([jnp.concatenate([a, b], 1) for a, b in zip(head, ident)])
    tab_p = jnp.stack([both, first])

    inv_m = ROPE_THETA ** (-jnp.arange(0, MLA_ROPE, 2, dtype=F32) / MLA_ROPE)
    cm, sm = jnp.cos(pos * inv_m), jnp.sin(pos * inv_m)
    hm = MLA_ROPE // 2
    pad = LANES - MLA_NOPE - MLA_ROPE
    tab_m = jnp.stack([jnp.concatenate([one(MLA_NOPE), cm, cm, one(pad)], 1),
                       jnp.concatenate([zero(MLA_NOPE), zero(hm), sm, zero(pad)], 1),
                       jnp.concatenate([zero(MLA_NOPE), -sm, zero(hm), zero(pad)], 1)])[None]
    return tab_p, tab_m


def _layer_weights(p, l):
    w_in = p["w_in"][l]
    D = w_in.shape[0]
    z = lambda n: jnp.zeros((D, n), F32)
    o = 0
    cuts = []
    for n in (MLA_Q_RANK, MLA_KV_RANK, MLA_ROPE, 3 * 3 * DIL_HEADS * HEAD_DIM, DSA_HEADS * HEAD_DIM,
              HEAD_DIM, HEAD_DIM, IDX_HEADS * IDX_DIM, IDX_DIM, IDX_HEADS):
        cuts.append(w_in[:, o:o + n])
        o += n
    w_cq, w_ckv, w_kr, w_dil, w_q, w_k, w_v, w_iq, w_ik, w_iw = cuts
    qs = HEAD_DIM ** -0.5
    w_mla = jnp.concatenate([w_cq, w_ckv, z(MLA_NOPE), w_kr, z(LANES - MLA_NOPE - MLA_ROPE)], 1)
    HW = DIL_HEADS * HEAD_DIM
    w_dil = w_dil.reshape(D, len(DIL_GROUPS), 3, HW)
    w_dqkv = jnp.concatenate([w_dil[:, :, 0] * qs, w_dil[:, :, 1], w_dil[:, :, 2]], axis=-1)
    w_dqkv = w_dqkv.transpose(1, 0, 2)
    w_dsa = jnp.concatenate([w_q * qs, w_k, w_v, w_iq, w_ik, w_iw, z(LANES - IDX_DIM - IDX_HEADS)], 1)

    def per_head(w, n_in, n_keep_lo, n_keep_hi):
        r = w.shape[0]
        w = w.reshape(r, MLA_HEADS, n_in)[:, :, n_keep_lo:n_keep_hi]
        w = jnp.pad(w, ((0, 0), (0, 0), (0, LANES - (n_keep_hi - n_keep_lo))))
        return w.reshape(r, MLA_HEADS * LANES)

    w_uq = per_head(p["w_uq"][l], MLA_NOPE + MLA_ROPE, 0, MLA_NOPE + MLA_ROPE)
    w_uk = per_head(p["w_ukv"][l], MLA_NOPE + MLA_V, 0, MLA_NOPE)
    w_uv = p["w_ukv"][l].reshape(MLA_KV_RANK, MLA_HEADS, MLA_NOPE + MLA_V)[:, :, MLA_NOPE:]
    w_uv = w_uv.reshape(MLA_KV_RANK, MLA_HEADS * MLA_V).T

    w_r = jnp.concatenate([p["w_group"][l], p["w_sub"][l], z(LANES - N_GROUPS - N_EXPERTS)], 1)
    w_r_hi = w_r.astype(BF16)
    w_r_lo = (w_r - w_r_hi.astype(F32)).astype(BF16)
    b_r = jnp.concatenate([p["b_group"][l], p["b_sub"][l], jnp.zeros((LANES - N_GROUPS - N_EXPERTS,), F32)])

    bf = lambda a: a.astype(BF16)
    return dict(
        w_proj=bf(jnp.concatenate([w_mla, w_dsa] + [w_dqkv[g] for g in range(len(DIL_GROUPS))], axis=1)),
        n_mla=w_mla.shape[1],
        q_g=p["q_norm_g"][l][None], kv_g=p["kv_norm_g"][l][None],
        w_uq=bf(w_uq), w_uk=bf(w_uk), w_uv=bf(w_uv),
        w_gate=bf(p["w_gate"][l]), b_gate=p["b_gate"][l][None],
        w_a=bf(p["w_a"][l]), w_b=bf(p["w_b"][l]), w_c=bf(p["w_c"][l]), w_o=bf(p["w_o"][l]),
        ln1_g=p["ln1_g"][l][None], ln1_b=p["ln1_b"][l][None],
        w_r_hi=w_r_hi, w_r_lo=w_r_lo, b_r=b_r[None],
        w1=bf(p["w1"][l]), w3=bf(p["w3"][l]), w2=bf(p["w2"][l]),
        ln2_g=p["ln2_g"][l][None], ln2_b=p["ln2_b"][l][None])


def _project(xt, w, tab_p, batch, seq):
    HW = DIL_HEADS * HEAD_DIM
    qw, iw_ = DSA_HEADS * HEAD_DIM, IDX_HEADS * IDX_DIM
    widths = [w["n_mla"], qw, LANES, iw_, LANES] + [3 * HW] * len(DIL_GROUPS)
    starts = [sum(widths[:k]) for k in range(len(widths))]
    rope2, rope1, none = 0, 1, -1
    kinds = ((none,) * (w["n_mla"] // LANES)
             + (rope2,) * (qw // LANES) + (rope1,) + (rope2,) * (iw_ // LANES) + (rope1,)
             + ((rope2,) * (2 * HW // LANES) + (none,) * (HW // LANES)) * len(DIL_GROUPS))
    return _proj(xt, w["w_proj"], tab_p, batch, seq, segs=tuple(zip(starts, widths)),
                 dtypes=(F32, BF16, BF16, BF16, F32) + (BF16,) * len(DIL_GROUPS), kinds=kinds,
                 dils=(1,) * 5 + tuple(d for _, d in DIL_GROUPS))


def _layer(xt, w, tab_p, tab_m, batch, seq, alpha):
    grp, q_c, kv_c, iq_c, ikw_c, *dil_qkv = _project(xt, w, tab_p, batch, seq)
    dil_o, dil_lse = [], []
    for qkv_g, (window, d) in zip(dil_qkv, DIL_GROUPS):
        o_g, lse_g = _dil_lat_attn(qkv_g, seq, d, band=window // d)
        dil_o.append(o_g)
        dil_lse.append(lse_g)

    q_a, k_a, v_a = _mla_up(grp, w["q_g"], w["kv_g"], w["w_uq"], w["w_uk"], w["w_uv"], tab_m, seq)
    o_a = _mla_attn(q_a, k_a, v_a, batch, seq)
    o_b = _dil_merge(dil_o, dil_lse, batch, seq)
    o_c = _dsa_attn(q_c, kv_c, iq_c, ikw_c, batch, seq)

    x1 = _mix_out(xt, o_a, o_b, o_c, w["w_gate"], w["b_gate"], w["w_a"], w["w_b"], w["w_c"], w["w_o"],
                  w["ln1_g"], w["ln1_b"], alpha)
    return _moe(x1, w["w_r_hi"], w["w_r_lo"], w["b_r"], w["w1"], w["w3"], w["w2"],
                w["ln2_g"], w["ln2_b"], alpha)


def kernel(x, w_in, q_norm_g, w_uq, kv_norm_g, w_ukv, w_gate, b_gate, w_a, w_b, w_c, w_o, ln1_g, ln1_b,
           w_group, b_group, w_sub, b_sub, w1, w3, w2, ln2_g, ln2_b):
    batch, seq, d_model = x.shape
    depth = w_in.shape[0]
    alpha = (2 * depth) ** 0.25
    p = dict(w_in=w_in, q_norm_g=q_norm_g, w_uq=w_uq, kv_norm_g=kv_norm_g, w_ukv=w_ukv, w_gate=w_gate,
             b_gate=b_gate, w_a=w_a, w_b=w_b, w_c=w_c, w_o=w_o, ln1_g=ln1_g, ln1_b=ln1_b, w_group=w_group,
             b_group=b_group, w_sub=w_sub, b_sub=b_sub, w1=w1, w3=w3, w2=w2, ln2_g=ln2_g, ln2_b=ln2_b)
    tab_p, tab_m = _rope_tables(seq)
    xt = x.reshape(batch * seq, d_model)
    for l in range(depth):
        xt = _layer(xt, _layer_weights(p, l), tab_p, tab_m, batch, seq, alpha)
    return xt.reshape(batch, seq, d_model)
```

```python
import functools

import jax
import jax.numpy as jnp
from jax import lax
from jax.experimental import pallas as pl
from jax.experimental.pallas import tpu as pltpu

F32 = jnp.float32
BF16 = jnp.bfloat16
I32 = jnp.int32

HEAD_DIM = 64
ROT_DIM = HEAD_DIM // 4
ROPE_THETA = 500000.0
EPS = 1e-6
MLA_HEADS = 8
MLA_Q_RANK = 256
MLA_KV_RANK = 128
MLA_NOPE = 64
MLA_ROPE = 32
MLA_V = 64
DIL_GROUPS = ((128, 1), (512, 4), (2048, 16))
DIL_HEADS = 4
DSA_HEADS = 8
IDX_HEADS = 8
IDX_DIM = 64
TOPK_MAX = 256
N_GROUPS = 4
EXPERTS_PER_GROUP = 8
N_EXPERTS = N_GROUPS * EXPERTS_PER_GROUP
D_EXPERT = 256
N_BRANCH = 3

LANES = 128
VMEM_LIMIT = 56 * 1024 * 1024
NEG_INF = float("-inf")
INT_MIN = -2 ** 31
M_FLOOR = -1e30
LOG2E = 1.4426950408889634
ROW_ALIGN = 16


def _params(*sem):
    return pltpu.CompilerParams(dimension_semantics=sem, vmem_limit_bytes=VMEM_LIMIT)


def _dot(a, b):
    return jnp.dot(a, b, preferred_element_type=F32)


def _dot_t(a, b):
    return lax.dot_general(a, b, (((1,), (1,)), ((), ())), preferred_element_type=F32)


def _rope_chunk(v, tab_ref, kind, shift):
    c = tab_ref[kind, 0]
    s1 = tab_ref[kind, 1]
    s2 = tab_ref[kind, 2]
    return v * c + pltpu.roll(v, shift, 1) * s1 + pltpu.roll(v, LANES - shift, 1) * s2


def _layer_norm(z, g, b):
    mu = jnp.mean(z, axis=-1, keepdims=True)
    zc = z - mu
    var = jnp.mean(zc * zc, axis=-1, keepdims=True)
    return zc * lax.rsqrt(var + EPS) * g + b


def _proj_kernel(*refs, segs, kinds, dils, shift):
    x_ref, w_ref, tab_ref = refs[:3]
    outs, scr = refs[3:3 + len(segs)], refs[3 + len(segs)]
    xb = x_ref[...].astype(BF16)
    tm = xb.shape[0]
    for (c0, width), d, o_ref in zip(segs, dils, outs):
        acc = _dot(xb, w_ref[:, c0:c0 + width])
        nch = width // LANES
        for c in range(nch):
            sl = slice(c * LANES, (c + 1) * LANES)
            v = acc[:, sl]
            kind = kinds[c0 // LANES + c]
            if kind >= 0:
                v = _rope_chunk(v, tab_ref, kind, shift)
            if d == 1:
                o_ref[:, sl] = v.astype(o_ref.dtype)
            else:
                scr[c] = v
        if d > 1:
            for r in range(d):
                for c in range(nch):
                    o_ref[r, :, c * LANES:(c + 1) * LANES] = (
                        scr[c, pl.ds(r, tm // d, stride=d), :].astype(o_ref.dtype))


def _proj(x, w, tab, batch, seq, *, segs, dtypes, kinds, dils, shift=ROT_DIM // 2, tm=256):
    T, K = x.shape
    spt = seq // tm
    out_specs, out_shape = [], []
    for (_, wd), dt, d in zip(segs, dtypes, dils):
        if d == 1:
            out_specs.append(pl.BlockSpec((tm, wd), lambda i: (i, 0)))
            out_shape.append(jax.ShapeDtypeStruct((T, wd), dt))
        else:
            out_specs.append(pl.BlockSpec((None, d, tm // d, wd), lambda i: (i // spt, 0, i % spt, 0)))
            out_shape.append(jax.ShapeDtypeStruct((batch, d, seq // d, wd), dt))
    kern = functools.partial(_proj_kernel, segs=segs, kinds=kinds, dils=dils, shift=shift)
    outs = pl.pallas_call(
        kern, grid=(T // tm,),
        in_specs=[pl.BlockSpec((tm, K), lambda i: (i, 0)), pl.BlockSpec(w.shape, lambda i: (0, 0)),
                  pl.BlockSpec((tab.shape[0], 3, tm, LANES), lambda i: (0, 0, i % spt, 0))],
        out_specs=out_specs, out_shape=out_shape,
        scratch_shapes=[pltpu.VMEM((max(wd for _, wd in segs) // LANES, tm, LANES), F32)],
        compiler_params=_params("parallel"), name="in_proj")(x, w, tab)
    return [o.reshape(T, o.shape[-1]) for o in outs]


def _mla_up_kernel(g_ref, qg_ref, kvg_ref, wq_ref, wk_ref, wv_ref, tab_ref, q_out, k_out, v_out):
    g = g_ref[...]
    cq = g[:, :MLA_Q_RANK]
    ckv = g[:, MLA_Q_RANK:MLA_Q_RANK + MLA_KV_RANK]
    kr = g[:, MLA_Q_RANK + MLA_KV_RANK:]
    cqn = (cq * lax.rsqrt(jnp.mean(cq * cq, axis=-1, keepdims=True) + EPS) * qg_ref[...]).astype(BF16)
    ckvn = (ckv * lax.rsqrt(jnp.mean(ckv * ckv, axis=-1, keepdims=True) + EPS) * kvg_ref[...]).astype(BF16)
    q3 = _dot(cqn, wq_ref[...])
    k = _dot(ckvn, wk_ref[...])
    v_out[...] = _dot_t(wv_ref[...], ckvn).astype(v_out.dtype)
    W = MLA_HEADS * LANES
    c, s1, s2 = tab_ref[0, 0], tab_ref[0, 1], tab_ref[0, 2]
    kr_rot = _rope_chunk(kr, tab_ref, 0, MLA_ROPE // 2)
    for h in range(MLA_HEADS):
        sl = slice(h * LANES, (h + 1) * LANES)
        q_rot = (q3[:, sl] * c + q3[:, W + h * LANES:W + (h + 1) * LANES] * s1
                 + q3[:, 2 * W + h * LANES:2 * W + (h + 1) * LANES] * s2)
        q_out[:, sl] = q_rot.astype(q_out.dtype)
        k_out[:, sl] = (k[:, sl] + kr_rot).astype(k_out.dtype)


def _mla_up(grp, qg, kvg, wq, wk, wv, tab, seq, tm=512):
    T = grp.shape[0]
    spt = seq // tm
    const = lambda i: (0, 0)
    return pl.pallas_call(
        _mla_up_kernel, grid=(T // tm,),
        in_specs=[pl.BlockSpec((tm, grp.shape[1]), lambda i: (i, 0)),
                  pl.BlockSpec(qg.shape, const), pl.BlockSpec(kvg.shape, const),
                  pl.BlockSpec(wq.shape, const), pl.BlockSpec(wk.shape, const), pl.BlockSpec(wv.shape, const),
                  pl.BlockSpec((1, 3, tm, LANES), lambda i: (0, 0, i % spt, 0))],
        out_specs=[pl.BlockSpec((tm, MLA_HEADS * LANES), lambda i: (i, 0)),
                   pl.BlockSpec((tm, MLA_HEADS * LANES), lambda i: (i, 0)),
                   pl.BlockSpec((MLA_HEADS * MLA_V, tm), lambda i: (0, i))],
        out_shape=[jax.ShapeDtypeStruct((T, MLA_HEADS * LANES), BF16),
                   jax.ShapeDtypeStruct((T, MLA_HEADS * LANES), BF16),
                   jax.ShapeDtypeStruct((MLA_HEADS * MLA_V, T), BF16)],
        compiler_params=_params("parallel"), name="mla_up")(grp, qg, kvg, wq, wk, wv, tab)


def _mla_attn_kernel(q_ref, k_ref, vt_ref, o_ref, acc_s, s_s, p_s, *, tq, c_exp):
    qi = pl.program_id(1)
    krow = lax.broadcasted_iota(I32, (tq, tq), 0)
    qcol = lax.broadcasted_iota(I32, (tq, tq), 1)
    diag_bias = jnp.where(krow <= qcol, 0.0, NEG_INF).astype(F32)
    acc_s[...] = jnp.zeros_like(acc_s)

    def step(j, carry, masked):
        ms, ls = carry
        ks = pl.multiple_of(j * tq, tq)
        for h in range(MLA_HEADS):
            kb = k_ref[pl.ds(ks, tq), h * LANES:(h + 1) * LANES]
            s_s[h] = _dot_t(kb, q_ref[:, h * LANES:(h + 1) * LANES])
        new_ms, new_ls, alphas = [], [], []
        for h in range(MLA_HEADS):
            s = s_s[h]
            if masked:
                s = s + diag_bias
            m_new = jnp.maximum(ms[h], jnp.max(s, axis=0, keepdims=True))
            alpha = jnp.exp2((ms[h] - m_new) * c_exp)
            p = jnp.exp2((s - m_new) * c_exp)
            new_ls.append(alpha * ls[h] + jnp.sum(p, axis=0, keepdims=True))
            new_ms.append(m_new)
            alphas.append(alpha)
            p_s[h] = p.astype(BF16)
        for h in range(MLA_HEADS):
            sl = slice(h * MLA_V, (h + 1) * MLA_V)
            acc_s[sl, :] = alphas[h] * acc_s[sl, :] + _dot(vt_ref[sl, pl.ds(ks, tq)], p_s[h])
        return tuple(new_ms), tuple(new_ls)

    init = (tuple(jnp.full((1, tq), NEG_INF, F32) for _ in range(MLA_HEADS)),
            tuple(jnp.zeros((1, tq), F32) for _ in range(MLA_HEADS)))
    carry = lax.fori_loop(0, qi, functools.partial(step, masked=False), init)
    _, ls = step(qi, carry, True)
    for h in range(MLA_HEADS):
        sl = slice(h * MLA_V, (h + 1) * MLA_V)
        acc_s[sl, :] = acc_s[sl, :] / ls[h]
    o_ref[...] = acc_s[...].T.astype(o_ref.dtype)


def _mla_attn(q, k, vt, batch, seq, tq=256):
    T = q.shape[0]
    nq = seq // tq
    c_exp = (MLA_NOPE + MLA_ROPE) ** -0.5 * LOG2E
    kern = functools.partial(_mla_attn_kernel, tq=tq, c_exp=c_exp)
    W = MLA_HEADS * LANES
    return pl.pallas_call(
        kern, grid=(batch, nq),
        in_specs=[pl.BlockSpec((tq, W), lambda b, i: (b * nq + i, 0)),
                  pl.BlockSpec((seq, W), lambda b, i: (b, 0)),
                  pl.BlockSpec((MLA_HEADS * MLA_V, seq), lambda b, i: (0, b))],
        out_specs=pl.BlockSpec((tq, MLA_HEADS * MLA_V), lambda b, i: (b * nq + i, 0)),
        out_shape=jax.ShapeDtypeStruct((T, MLA_HEADS * MLA_V), BF16),
        scratch_shapes=[pltpu.VMEM((MLA_HEADS * MLA_V, tq), F32),
                        pltpu.VMEM((MLA_HEADS, tq, tq), F32), pltpu.VMEM((MLA_HEADS, tq, tq), BF16)],
        compiler_params=_params("parallel", "arbitrary"), name="mla_attn")(q, k, vt)


def _dil_lat_kernel(qkv_ref, halo_ref, o_ref, lse_ref, s_s, p_s, *, band, nsub, sub_per_seq):
    i = pl.program_id(0)
    HW = DIL_HEADS * HEAD_DIM
    v_t = lambda ref, rows: ref[rows, 2 * HW:3 * HW].astype(F32).T.astype(BF16)
    vts = [v_t(qkv_ref, slice(u * band, (u + 1) * band)) for u in range(nsub)]
    qk_ref, qkh_ref = qkv_ref, halo_ref
    kj = lax.broadcasted_iota(I32, (band, band), 0)
    qi = lax.broadcasted_iota(I32, (band, band), 1)
    bias_cur = jnp.where(kj <= qi, 0.0, NEG_INF).astype(F32)
    bias_prev = jnp.where(kj >= qi, 0.0, NEG_INF).astype(F32)
    prevs = []
    for u in range(nsub):
        if sub_per_seq == 1 or (u % sub_per_seq == 0 and nsub % sub_per_seq == 0):
            prevs.append(None)
        elif u > 0:
            prevs.append((qk_ref[(u - 1) * band:u * band, HW:2 * HW], vts[u - 1], bias_prev))
        else:
            has = (i * nsub) % sub_per_seq != 0
            prevs.append((qkh_ref[:, HW:2 * HW], v_t(qkh_ref, slice(0, band)),
                          jnp.where(has, bias_prev, NEG_INF)))

    for u in range(nsub):
        rows = slice(u * band, (u + 1) * band)
        q = qk_ref[rows, 0:HW]
        k_cur = qk_ref[rows, HW:2 * HW]
        for h in range(DIL_HEADS):
            hs = slice(h * HEAD_DIM, (h + 1) * HEAD_DIM)
            s_s[u, h, 0] = _dot_t(k_cur[:, hs], q[:, hs])
            if prevs[u] is not None:
                s_s[u, h, 1] = _dot_t(prevs[u][0][:, hs], q[:, hs])
    stats = {}
    for u in range(nsub):
        for h in range(DIL_HEADS):
            s_c = s_s[u, h, 0] + bias_cur
            m = jnp.max(s_c, axis=0, keepdims=True)
            if prevs[u] is not None:
                s_p = s_s[u, h, 1] + prevs[u][2]
                m = jnp.maximum(m, jnp.max(s_p, axis=0, keepdims=True))
            p_c = jnp.exp(s_c - m)
            l = jnp.sum(p_c, axis=0, keepdims=True)
            p_s[u, h, 0] = p_c.astype(BF16)
            if prevs[u] is not None:
                p_p = jnp.exp(s_p - m)
                l = l + jnp.sum(p_p, axis=0, keepdims=True)
                p_s[u, h, 1] = p_p.astype(BF16)
            stats[u, h] = (m, l)
    pad = jnp.zeros((band - DIL_HEADS, band), F32)
    for u in range(nsub):
        rows = slice(u * band, (u + 1) * band)
        outs, lses = [], []
        for h in range(DIL_HEADS):
            hs = slice(h * HEAD_DIM, (h + 1) * HEAD_DIM)
            m, l = stats[u, h]
            o = _dot(vts[u][hs, :], p_s[u, h, 0])
            if prevs[u] is not None:
                o = o + _dot(prevs[u][1][hs, :], p_s[u, h, 1])
            outs.append(o / l)
            lses.append(m + jnp.log(l))
        o_ref[rows, :] = jnp.concatenate(outs, axis=0).T.astype(o_ref.dtype)
        lse_ref[rows, :] = jnp.concatenate(lses + [pad], axis=0).T


def _dil_lat_attn(qkv, seq, d, band, nsub=4):
    T = qkv.shape[0]
    HW = DIL_HEADS * HEAD_DIM
    M = seq // d
    assert band == LANES and M % band == 0
    sub_per_seq = M // band
    assert sub_per_seq % nsub == 0 or nsub % sub_per_seq == 0
    R = nsub * band
    halo = lambda i: jnp.maximum(i * nsub - 1, 0)
    kern = functools.partial(_dil_lat_kernel, band=band, nsub=nsub, sub_per_seq=sub_per_seq)
    return pl.pallas_call(
        kern, grid=(T // R,),
        in_specs=[pl.BlockSpec((R, 3 * HW), lambda i: (i, 0)),
                  pl.BlockSpec((band, 3 * HW), lambda i: (halo(i), 0))],
        out_specs=[pl.BlockSpec((R, HW), lambda i: (i, 0)), pl.BlockSpec((R, LANES), lambda i: (i, 0))],
        out_shape=[jax.ShapeDtypeStruct((T, HW), BF16), jax.ShapeDtypeStruct((T, LANES), F32)],
        scratch_shapes=[pltpu.VMEM((nsub, DIL_HEADS, 2, band, band), F32),
                        pltpu.VMEM((nsub, DIL_HEADS, 2, band, band), BF16)],
        compiler_params=_params("parallel"), name="dil_lat")(qkv, qkv)


def _dil_merge_kernel(*refs, dils, tmm):
    ng = len(dils)
    o_refs, l_refs = refs[0:2 * ng:2], refs[1:2 * ng:2]
    out_ref, scr = refs[2 * ng], refs[2 * ng + 1]
    nch = DIL_HEADS * HEAD_DIM // LANES
    o_pos, l_pos = [], []
    for g, d in enumerate(dils):
        if d == 1:
            o_pos.append([o_refs[g][:, c * LANES:(c + 1) * LANES].astype(F32) for c in range(nch)])
            l_pos.append(l_refs[g][...])
            continue
        for r in range(d):
            rows = pl.ds(r, tmm // d, stride=d)
            for c in range(nch):
                scr[g, c, rows, :] = o_refs[g][r, :, c * LANES:(c + 1) * LANES].astype(F32)
            scr[g, nch, rows, :] = l_refs[g][r]
        o_pos.append([scr[g, c] for c in range(nch)])
        l_pos.append(scr[g, nch])
    mx = functools.reduce(jnp.maximum, l_pos)
    es = [jnp.exp(l - mx) for l in l_pos]
    den = functools.reduce(lambda a, b: a + b, es)
    hpc = LANES // HEAD_DIM
    for c in range(nch):
        acc = None
        for g in range(ng):
            w = es[g] / den
            wc = jnp.concatenate([jnp.broadcast_to(w[:, c * hpc + k:c * hpc + k + 1], (tmm, HEAD_DIM))
                                  for k in range(hpc)], axis=1)
            term = wc * o_pos[g][c]
            acc = term if acc is None else acc + term
        out_ref[:, c * LANES:(c + 1) * LANES] = acc.astype(out_ref.dtype)


def _dil_merge(os_, ls_, batch, seq, tmm=256):
    dils = tuple(d for _, d in DIL_GROUPS)
    HW = DIL_HEADS * HEAD_DIM
    T = batch * seq
    tps = seq // tmm
    args, in_specs = [], []
    for o, l, d in zip(os_, ls_, dils):
        if d == 1:
            args += [o, l]
            in_specs += [pl.BlockSpec((tmm, HW), lambda i: (i, 0)), pl.BlockSpec((tmm, LANES), lambda i: (i, 0))]
        else:
            M = seq // d
            args += [o.reshape(batch, d, M, HW), l.reshape(batch, d, M, LANES)]
            in_specs += [pl.BlockSpec((None, d, tmm // d, HW), lambda i: (i // tps, 0, i % tps, 0)),
                         pl.BlockSpec((None, d, tmm // d, LANES), lambda i: (i // tps, 0, i % tps, 0))]
    kern = functools.partial(_dil_merge_kernel, dils=dils, tmm=tmm)
    return pl.pallas_call(
        kern, grid=(T // tmm,), in_specs=in_specs,
        out_specs=pl.BlockSpec((tmm, HW), lambda i: (i, 0)),
        out_shape=jax.ShapeDtypeStruct((T, HW), BF16),
        scratch_shapes=[pltpu.VMEM((len(dils), HW // LANES + 1, tmm, LANES), F32)],
        compiler_params=_params("parallel"), name="dil_merge")(*args)


def _dsa_kernel(q_ref, iq_ref, kv_ref, ikw_ref, o_ref, qall_s, iqall_s, kvt_s, key_s, bias_s, j_s, acc_s, p_s,
                *, tq, ck, top, seq):
    qi = pl.program_id(1)
    lo = qi * tq
    nch = (lo + tq + ck - 1) // ck
    krow = lax.broadcasted_iota(I32, (ck, tq), 0)
    qcol = lo + lax.broadcasted_iota(I32, (ck, tq), 1)

    def chunk(c):
        return pl.ds(pl.multiple_of(c * ck, ck), ck)

    @pl.when(qi == 0)
    def _():
        def body(c, _):
            kvt_s[:, chunk(c)] = kv_ref[chunk(c), :].astype(F32).T.astype(BF16)
            return 0
        lax.fori_loop(0, seq // ck, body, 0)

    for h in range(DSA_HEADS):
        qall_s[h * tq:(h + 1) * tq, :] = q_ref[:, h * HEAD_DIM:(h + 1) * HEAD_DIM]
    for h in range(IDX_HEADS):
        iqall_s[h * tq:(h + 1) * tq, :] = iq_ref[:, h * IDX_DIM:(h + 1) * IDX_DIM]

    @pl.when(lo + tq <= top)
    def _():
        def body(c, _):
            bias_s[chunk(c), :] = jnp.where(c * ck + krow <= qcol, 0.0, NEG_INF).astype(F32)
            return 0
        lax.fori_loop(0, nch, body, 0)

    @pl.when(lo + tq > top)
    def _():
        iw_t = ikw_ref[pl.ds(pl.multiple_of(lo, tq), tq), :].T[IDX_DIM:IDX_DIM + IDX_HEADS, :]
        iw_t = iw_t * (IDX_HEADS ** -0.5 * IDX_DIM ** -0.5)

        def score_body(c, _):
            ik = ikw_ref[chunk(c), 0:IDX_DIM].astype(BF16)
            r = _dot_t(ik, iqall_s[...])
            sc = jnp.zeros((ck, tq), F32)
            for h in range(IDX_HEADS):
                sc = sc + iw_t[h:h + 1, :] * jnp.maximum(r[:, h * tq:(h + 1) * tq], 0.0)
            sc = jnp.where(c * ck + krow <= qcol, sc, NEG_INF)
            bits = lax.bitcast_convert_type(sc, I32)
            key_s[chunk(c), :] = bits ^ ((bits >> 31) & 0x7FFFFFFF)
            return 0
        lax.fori_loop(0, nch, score_body, 0)

        def count(pred):
            def body(c, acc):
                ind = pred(key_s[chunk(c), :], c)
                return acc + jnp.sum(ind.reshape(ck // 64, 64, tq), axis=0)
            acc = lax.fori_loop(0, nch, body, jnp.zeros((64, tq), F32))
            return jnp.sum(acc, axis=0, keepdims=True)

        def count_ge(cand):
            return count(lambda kc, c: jnp.where(kc >= cand, 1.0, 0.0))

        topf = float(top)
        tau = jnp.where(count_ge(jnp.zeros((1, tq), I32)) >= topf, 0, INT_MIN).astype(I32)

        def search(i, tau):
            cand = tau | jnp.left_shift(jnp.int32(1), 30 - i)
            return jnp.where(count_ge(cand) >= topf, cand, tau)
        tau = lax.fori_loop(0, 31, search, tau)

        need = topf - count_ge(tau + 1)
        j_s[...] = jnp.full(j_s.shape, seq, I32)

        @pl.when(jnp.max(count_ge(tau)) > topf)
        def _():
            def count_eq_lt(J):
                return count(lambda kc, c: jnp.where(kc == tau, jnp.where(c * ck + krow < J, 1.0, 0.0), 0.0))

            nbits = seq.bit_length()

            def tie(i, J):
                cand = J + jnp.left_shift(jnp.int32(1), nbits - 1 - i)
                return jnp.where(count_eq_lt(cand) <= need, cand, J)
            J = lax.fori_loop(0, nbits, tie, jnp.zeros((1, tq), I32))
            j_s[...] = jnp.broadcast_to(J, j_s.shape)

        J = j_s[0:1, :]

        def bias_body(c, _):
            kc = key_s[chunk(c), :]
            sel = jnp.where(kc > tau, 0.0,
                            jnp.where(kc == tau, jnp.where(c * ck + krow < J, 0.0, NEG_INF), NEG_INF))
            bias_s[chunk(c), :] = sel.astype(F32)
            return 0
        lax.fori_loop(0, nch, bias_body, 0)

    acc_s[...] = jnp.zeros_like(acc_s)

    def step(c, carry):
        ms, ls = carry
        kc = kv_ref[chunk(c), 0:HEAD_DIM]
        bias = bias_s[chunk(c), :]
        vt = kvt_s[HEAD_DIM:2 * HEAD_DIM, chunk(c)]
        new_ms, new_ls, alphas = [], [], []
        for h in range(DSA_HEADS):
            if h % 2 == 0:
                s_pair = _dot_t(kc, qall_s[h * tq:(h + 2) * tq, :])
            s = s_pair[:, (h % 2) * tq:(h % 2 + 1) * tq] + bias
            m_new = jnp.maximum(ms[h], jnp.max(s, axis=0, keepdims=True))
            alpha = jnp.exp(ms[h] - m_new)
            p = jnp.exp(s - m_new)
            new_ls.append(alpha * ls[h] + jnp.sum(p, axis=0, keepdims=True))
            new_ms.append(m_new)
            alphas.append(alpha)
            p_s[:, h * tq:(h + 1) * tq] = p.astype(BF16)
        acc_s[...] = acc_s[...] * jnp.concatenate(alphas, axis=1) + _dot(vt, p_s[...])
        return tuple(new_ms), tuple(new_ls)

    init = (tuple(jnp.full((1, tq), M_FLOOR, F32) for _ in range(DSA_HEADS)),
            tuple(jnp.zeros((1, tq), F32) for _ in range(DSA_HEADS)))
    _, ls = lax.fori_loop(0, nch, step, init)
    o_all = acc_s[...] / jnp.concatenate(ls, axis=1)
    for h in range(0, DSA_HEADS, 2):
        pair = jnp.concatenate([o_all[:, h * tq:(h + 1) * tq], o_all[:, (h + 1) * tq:(h + 2) * tq]], axis=0)
        o_ref[:, h * HEAD_DIM:(h + 2) * HEAD_DIM] = pair.T.astype(o_ref.dtype)


def _dsa_attn(q, kv, iq, ikw, batch, seq, tq=LANES, ck=256):
    T = q.shape[0]
    nq = seq // tq
    top = min(TOPK_MAX, seq // 4)
    assert tq == LANES and top % tq == 0 and seq % ck == 0
    kern = functools.partial(_dsa_kernel, tq=tq, ck=ck, top=top, seq=seq)
    W = DSA_HEADS * HEAD_DIM
    return pl.pallas_call(
        kern, grid=(batch, nq),
        in_specs=[pl.BlockSpec((tq, W), lambda b, i: (b * nq + i, 0)),
                  pl.BlockSpec((tq, IDX_HEADS * IDX_DIM), lambda b, i: (b * nq + i, 0)),
                  pl.BlockSpec((seq, LANES), lambda b, i: (b, 0)),
                  pl.BlockSpec((seq, LANES), lambda b, i: (b, 0))],
        out_specs=pl.BlockSpec((tq, W), lambda b, i: (b * nq + i, 0)),
        out_shape=jax.ShapeDtypeStruct((T, W), BF16),
        scratch_shapes=[pltpu.VMEM((DSA_HEADS * tq, HEAD_DIM), BF16),
                        pltpu.VMEM((IDX_HEADS * tq, IDX_DIM), BF16),
                        pltpu.VMEM((LANES, seq), BF16),
                        pltpu.VMEM((seq, tq), I32),
                        pltpu.VMEM((seq, tq), F32),
                        pltpu.VMEM((8, tq), I32),
                        pltpu.VMEM((HEAD_DIM, DSA_HEADS * tq), F32),
                        pltpu.VMEM((ck, DSA_HEADS * tq), BF16)],
        compiler_params=_params("arbitrary", "arbitrary"), name="dsa_attn")(q, iq, kv, ikw)


def _mix_out_kernel(x_ref, oa_ref, ob_ref, oc_ref, wg_ref, bg_ref, wa_ref, wb_ref, wc_ref, wo_ref,
                    lg_ref, lb_ref, o_ref, *, alpha, d_model):
    x = x_ref[...]
    xb = x.astype(BF16)
    merged = None
    for br, (o_r, w_r) in enumerate(((oa_ref, wa_ref), (ob_ref, wb_ref), (oc_ref, wc_ref))):
        sl = slice(br * d_model, (br + 1) * d_model)
        gate = jax.nn.sigmoid(_dot(xb, wg_ref[:, sl]) + bg_ref[:, sl])
        term = gate * _dot(o_r[...], w_r[...])
        merged = term if merged is None else merged + term
    mix = _dot(merged.astype(BF16), wo_ref[...])
    o_ref[...] = _layer_norm(alpha * x + mix, lg_ref[...], lb_ref[...])


def _mix_out(x, oa, ob, oc, wg, bg, wa, wb, wc, wo, lg, lb, alpha, tm=256):
    T, D = x.shape
    const = lambda i: (0, 0)
    full = lambda a: pl.BlockSpec(a.shape, const)
    row = lambda a: pl.BlockSpec((tm, a.shape[1]), lambda i: (i, 0))
    kern = functools.partial(_mix_out_kernel, alpha=alpha, d_model=D)
    return pl.pallas_call(
        kern, grid=(T // tm,),
        in_specs=[row(x), row(oa), row(ob), row(oc), full(wg), full(bg), full(wa), full(wb), full(wc),
                  full(wo), full(lg), full(lb)],
        out_specs=pl.BlockSpec((tm, D), lambda i: (i, 0)),
        out_shape=jax.ShapeDtypeStruct((T, D), F32),
        compiler_params=_params("parallel"), name="mix_out")(x, oa, ob, oc, wg, bg, wa, wb, wc, wo, lg, lb)


def _route(x, wr_hi_ref, wr_lo_ref, br_ref):
    xh = x.astype(BF16)
    xl = (x - xh.astype(F32)).astype(BF16)
    logits = _dot(xh, wr_hi_ref[...]) + _dot(xl, wr_hi_ref[...]) + _dot(xh, wr_lo_ref[...]) + br_ref[...]
    lane = lax.broadcasted_iota(I32, logits.shape, 1).astype(F32)
    none = float(LANES)
    glog = jnp.where(lane < N_GROUPS, logits, NEG_INF)
    gmax = jnp.max(glog, axis=1, keepdims=True)
    g_p = 1.0 / jnp.sum(jnp.exp(glog - gmax), axis=1, keepdims=True)
    g_idx = jnp.min(jnp.where(glog == gmax, lane, none), axis=1, keepdims=True)
    first = N_GROUPS + g_idx * EXPERTS_PER_GROUP
    sub = jnp.where(lane >= first, jnp.where(lane < first + EXPERTS_PER_GROUP, logits, NEG_INF), NEG_INF)
    v1 = jnp.max(sub, axis=1, keepdims=True)
    i1 = jnp.min(jnp.where(sub == v1, lane, none), axis=1, keepdims=True)
    sub2 = jnp.where(lane == i1, NEG_INF, sub)
    v2 = jnp.max(sub2, axis=1, keepdims=True)
    i2 = jnp.min(jnp.where(sub2 == v2, lane, none), axis=1, keepdims=True)
    e2 = jnp.exp(v2 - v1)
    w1 = g_p / (1.0 + e2)
    w2 = g_p * e2 / (1.0 + e2)
    return jnp.where(lane == i1, w1, jnp.where(lane == i2, w2, 0.0)), g_idx


def _split3(v):
    hi = v.astype(BF16)
    r = v - hi.astype(F32)
    mid = r.astype(BF16)
    return hi, mid, (r - mid.astype(F32)).astype(BF16)


def _moe_kernel(x_ref, wrh_ref, wrl_ref, br_ref, w1_ref, w3_ref, w2_ref, lg_ref, lb_ref, o_ref,
                xs_s, combs_s, pt_s, acc_s, rng_s, *, alpha, tm, rb, ts, eps):
    e = pl.program_id(1)

    @pl.when(e == 0)
    def _():
        x = x_ref[...]
        comb, g_idx = _route(x, wrh_ref, wrl_ref, br_ref)
        lane = lax.broadcasted_iota(I32, (tm, LANES), 1).astype(F32)
        onehot_g = jnp.where(lane == g_idx, 1.0, 0.0)
        ti = lax.broadcasted_iota(I32, (tm, tm), 0)
        tj = lax.broadcasted_iota(I32, (tm, tm), 1)
        earlier = jnp.where(tj < ti, 1.0, 0.0).astype(BF16)
        rank = _dot(earlier, onehot_g.astype(BF16))
        counts = jnp.sum(onehot_g, axis=0, keepdims=True)
        padded = jnp.ceil(counts * (1.0 / ROW_ALIGN)) * ROW_ALIGN
        lane1 = lax.broadcasted_iota(I32, (1, LANES), 1)
        off = jnp.zeros((1, LANES), F32)
        for k in range(1, N_GROUPS):
            off = off + jnp.where(lane1 >= k, pltpu.roll(padded, k, 1), 0.0)
        pos = jnp.sum(onehot_g * (off + rank), axis=1, keepdims=True)
        pos_row = jnp.broadcast_to(pos, (tm, LANES)).T[0:1, :]
        sj = lax.broadcasted_iota(I32, (tm, ts), 1)
        si = lax.broadcasted_iota(I32, (ts, tm), 0)
        pt_s[...] = jnp.where(sj.astype(F32) == pos, 1.0, 0.0).astype(BF16)
        perm = jnp.where(si.astype(F32) == pos_row, 1.0, 0.0).astype(BF16)
        xs_s[...] = _dot(perm, x.astype(BF16)).astype(BF16)
        c_hi, c_mid, c_lo = _split3(comb)
        combs_s[...] = _dot(perm, c_hi) + _dot(perm, c_mid) + _dot(perm, c_lo)
        acc_s[...] = jnp.zeros_like(acc_s)
        for g in range(N_GROUPS):
            start = jnp.sum(jnp.where(lane1 == g, off, 0.0)).astype(I32)
            cnt = jnp.sum(jnp.where(lane1 == g, counts, 0.0)).astype(I32)
            rng_s[g] = start
            rng_s[N_GROUPS + g] = (cnt + rb - 1) // rb

    g = (e * eps) // EXPERTS_PER_GROUP
    lane_b = lax.broadcasted_iota(I32, (rb, LANES), 1)
    start = rng_s[g]

    def block(b, _):
        rows = pl.ds(pl.multiple_of(start + b * rb, ROW_ALIGN), rb)
        xb = xs_s[rows, :]
        comb = combs_s[rows, :]
        y = None
        for k in range(eps):
            we = jnp.sum(jnp.where(lane_b == N_GROUPS + e * eps + k, comb, 0.0), axis=1, keepdims=True)
            a = _dot(xb, w1_ref[k])
            b3 = _dot(xb, w3_ref[k])
            hid = (a * jax.nn.sigmoid(a) * b3).astype(BF16)
            yk = we * _dot(hid, w2_ref[k])
            y = yk if y is None else y + yk
        acc_s[rows, :] += y
        return 0
    lax.fori_loop(0, rng_s[N_GROUPS + g], block, 0)

    @pl.when(e == N_EXPERTS // eps - 1)
    def _():
        y = _dot(pt_s[...], acc_s[...].astype(BF16))
        o_ref[...] = _layer_norm(alpha * x_ref[...] + y, lg_ref[...], lb_ref[...])


def _moe(x, wrh, wrl, br, w1, w3, w2, lg, lb, alpha, tm=1024, rb=320, eps=4):
    T, D = x.shape
    assert EXPERTS_PER_GROUP % eps == 0
    const = lambda i, e: (0, 0)
    full = lambda a: pl.BlockSpec(a.shape, const)
    ts = -(-(tm + N_GROUPS * ROW_ALIGN + rb) // LANES) * LANES
    kern = functools.partial(_moe_kernel, alpha=alpha, tm=tm, rb=rb, ts=ts, eps=eps)
    return pl.pallas_call(
        kern, grid=(T // tm, N_EXPERTS // eps),
        in_specs=[pl.BlockSpec((tm, D), lambda i, e: (i, 0)), full(wrh), full(wrl), full(br),
                  pl.BlockSpec((eps, D, D_EXPERT), lambda i, e: (e, 0, 0)),
                  pl.BlockSpec((eps, D, D_EXPERT), lambda i, e: (e, 0, 0)),
                  pl.BlockSpec((eps, D_EXPERT, D), lambda i, e: (e, 0, 0)),
                  full(lg), full(lb)],
        out_specs=pl.BlockSpec((tm, D), lambda i, e: (i, 0)),
        out_shape=jax.ShapeDtypeStruct((T, D), F32),
        scratch_shapes=[pltpu.VMEM((ts, D), BF16),
                        pltpu.VMEM((ts, LANES), F32),
                        pltpu.VMEM((tm, ts), BF16),
                        pltpu.VMEM((ts, D), F32),
                        pltpu.SMEM((2 * N_GROUPS,), I32)],
        compiler_params=_params("parallel", "arbitrary"), name="moe")(x, wrh, wrl, br, w1, w3, w2, lg, lb)


def _rope_tables(seq):
    pos = jnp.arange(seq, dtype=F32)[:, None]
    one = lambda n: jnp.ones((seq, n), F32)
    zero = lambda n: jnp.zeros((seq, n), F32)

    inv_p = ROPE_THETA ** (-jnp.arange(0, ROT_DIM, 2, dtype=F32) / ROT_DIM)
    cp, sp = jnp.cos(pos * inv_p), jnp.sin(pos * inv_p)
    hp = ROT_DIM // 2
    rest = HEAD_DIM - ROT_DIM
    head = (jnp.concatenate([cp, cp, one(rest)], 1),
            jnp.concatenate([zero(hp), sp, zero(rest)], 1),
            jnp.concatenate([-sp, zero(hp), zero(rest)], 1))
    ident = (one(HEAD_DIM), zero(HEAD_DIM), zero(HEAD_DIM))
    both = jnp.stack([jnp.concatenate([a, a], 1) for a in head])
    first = jnp.stack([jnp.concatenate([a, b], 1) for a, b in zip(head, ident)])
    tab_p = jnp.stack([both, first])

    inv_m = ROPE_THETA ** (-jnp.arange(0, MLA_ROPE, 2, dtype=F32) / MLA_ROPE)
    cm, sm = jnp.cos(pos * inv_m), jnp.sin(pos * inv_m)
    hm = MLA_ROPE // 2
    pad = LANES - MLA_NOPE - MLA_ROPE
    tab_m = jnp.stack([jnp.concatenate([one(MLA_NOPE), cm, cm, one(pad)], 1),
                       jnp.concatenate([zero(MLA_NOPE), zero(hm), sm, zero(pad)], 1),
                       jnp.concatenate([zero(MLA_NOPE), -sm, zero(hm), zero(pad)], 1)])[None]
    return tab_p, tab_m


def _layer_weights(p, l):
    w_in = p["w_in"][l]
    D = w_in.shape[0]
    z = lambda n: jnp.zeros((D, n), F32)
    o = 0
    cuts = []
    for n in (MLA_Q_RANK, MLA_KV_RANK, MLA_ROPE, 3 * 3 * DIL_HEADS * HEAD_DIM, DSA_HEADS * HEAD_DIM,
              HEAD_DIM, HEAD_DIM, IDX_HEADS * IDX_DIM, IDX_DIM, IDX_HEADS):
        cuts.append(w_in[:, o:o + n])
        o += n
    w_cq, w_ckv, w_kr, w_dil, w_q, w_k, w_v, w_iq, w_ik, w_iw = cuts
    qs = HEAD_DIM ** -0.5
    w_mla = jnp.concatenate([w_cq, w_ckv, z(MLA_NOPE), w_kr, z(LANES - MLA_NOPE - MLA_ROPE)], 1)
    HW = DIL_HEADS * HEAD_DIM
    w_dil = w_dil.reshape(D, len(DIL_GROUPS), 3, HW)
    w_dqkv = jnp.concatenate([w_dil[:, :, 0] * qs, w_dil[:, :, 1], w_dil[:, :, 2]], axis=-1)
    w_dqkv = w_dqkv.transpose(1, 0, 2)
    w_dsa = jnp.concatenate([w_q * qs, w_k, w_v, w_iq, w_ik, w_iw, z(LANES - IDX_DIM - IDX_HEADS)], 1)

    def per_head(w, n_in, n_keep_lo, n_keep_hi):
        r = w.shape[0]
        w = w.reshape(r, MLA_HEADS, n_in)[:, :, n_keep_lo:n_keep_hi]
        w = jnp.pad(w, ((0, 0), (0, 0), (0, LANES - (n_keep_hi - n_keep_lo))))
        return w.reshape(r, MLA_HEADS * LANES)

    w_uq = per_head(p["w_uq"][l], MLA_NOPE + MLA_ROPE, 0, MLA_NOPE + MLA_ROPE)
    w_uq3 = w_uq.reshape(MLA_Q_RANK, MLA_HEADS, LANES)
    w_uq = jnp.concatenate([jnp.roll(w_uq3, sh, axis=2).reshape(MLA_Q_RANK, -1)
                            for sh in (0, MLA_ROPE // 2, -(MLA_ROPE // 2))], axis=1)
    w_uk = per_head(p["w_ukv"][l], MLA_NOPE + MLA_V, 0, MLA_NOPE)
    w_uv = p["w_ukv"][l].reshape(MLA_KV_RANK, MLA_HEADS, MLA_NOPE + MLA_V)[:, :, MLA_NOPE:]
    w_uv = w_uv.reshape(MLA_KV_RANK, MLA_HEADS * MLA_V).T

    w_r = jnp.concatenate([p["w_group"][l], p["w_sub"][l], z(LANES - N_GROUPS - N_EXPERTS)], 1)
    w_r_hi = w_r.astype(BF16)
    w_r_lo = (w_r - w_r_hi.astype(F32)).astype(BF16)
    b_r = jnp.concatenate([p["b_group"][l], p["b_sub"][l], jnp.zeros((LANES - N_GROUPS - N_EXPERTS,), F32)])

    bf = lambda a: a.astype(BF16)
    return dict(
        w_proj=bf(jnp.concatenate([w_mla, w_dsa] + [w_dqkv[g] for g in range(len(DIL_GROUPS))], axis=1)),
        n_mla=w_mla.shape[1],
        q_g=p["q_norm_g"][l][None], kv_g=p["kv_norm_g"][l][None],
        w_uq=bf(w_uq), w_uk=bf(w_uk), w_uv=bf(w_uv),
        w_gate=bf(p["w_gate"][l]), b_gate=p["b_gate"][l][None],
        w_a=bf(p["w_a"][l]), w_b=bf(p["w_b"][l]), w_c=bf(p["w_c"][l]), w_o=bf(p["w_o"][l]),
        ln1_g=p["ln1_g"][l][None], ln1_b=p["ln1_b"][l][None],
        w_r_hi=w_r_hi, w_r_lo=w_r_lo, b_r=b_r[None],
        w1=bf(p["w1"][l]), w3=bf(p["w3"][l]), w2=bf(p["w2"][l]),
        ln2_g=p["ln2_g"][l][None], ln2_b=p["ln2_b"][l][None])


def _project(xt, w, tab_p, batch, seq):
    HW = DIL_HEADS * HEAD_DIM
    qw, iw_ = DSA_HEADS * HEAD_DIM, IDX_HEADS * IDX_DIM
    widths = [w["n_mla"], qw, LANES, iw_, LANES] + [3 * HW] * len(DIL_GROUPS)
    starts = [sum(widths[:k]) for k in range(len(widths))]
    rope2, rope1, none = 0, 1, -1
    kinds = ((none,) * (w["n_mla"] // LANES)
             + (rope2,) * (qw // LANES) + (rope1,) + (rope2,) * (iw_ // LANES) + (rope1,)
             + ((rope2,) * (2 * HW // LANES) + (none,) * (HW // LANES)) * len(DIL_GROUPS))
    return _proj(xt, w["w_proj"], tab_p, batch, seq, segs=tuple(zip(starts, widths)),
                 dtypes=(F32, BF16, BF16, BF16, F32) + (BF16,) * len(DIL_GROUPS), kinds=kinds,
                 dils=(1,) * 5 + tuple(d for _, d in DIL_GROUPS))


def _layer(xt, w, tab_p, tab_m, batch, seq, alpha):
    grp, q_c, kv_c, iq_c, ikw_c, *dil_qkv = _project(xt, w, tab_p, batch, seq)
    dil_o, dil_lse = [], []
    for qkv_g, (window, d) in zip(dil_qkv, DIL_GROUPS):
        o_g, lse_g = _dil_lat_attn(qkv_g, seq, d, band=window // d)
        dil_o.append(o_g)
        dil_lse.append(lse_g)

    q_a, k_a, v_a = _mla_up(grp, w["q_g"], w["kv_g"], w["w_uq"], w["w_uk"], w["w_uv"], tab_m, seq)
    o_a = _mla_attn(q_a, k_a, v_a, batch, seq)
    o_b = _dil_merge(dil_o, dil_lse, batch, seq)
    o_c = _dsa_attn(q_c, kv_c, iq_c, ikw_c, batch, seq)

    x1 = _mix_out(xt, o_a, o_b, o_c, w["w_gate"], w["b_gate"], w["w_a"], w["w_b"], w["w_c"], w["w_o"],
                  w["ln1_g"], w["ln1_b"], alpha)
    return _moe(x1, w["w_r_hi"], w["w_r_lo"], w["b_r"], w["w1"], w["w3"], w["w2"],
                w["ln2_g"], w["ln2_b"], alpha)


def kernel(x, w_in, q_norm_g, w_uq, kv_norm_g, w_ukv, w_gate, b_gate, w_a, w_b, w_c, w_o, ln1_g, ln1_b,
           w_group, b_group, w_sub, b_sub, w1, w3, w2, ln2_g, ln2_b):
    batch, seq, d_model = x.shape
    depth = w_in.shape[0]
    alpha = (2 * depth) ** 0.25
    p = dict(w_in=w_in, q_norm_g=q_norm_g, w_uq=w_uq, kv_norm_g=kv_norm_g, w_ukv=w_ukv, w_gate=w_gate,
             b_gate=b_gate, w_a=w_a, w_b=w_b, w_c=w_c, w_o=w_o, ln1_g=ln1_g, ln1_b=ln1_b, w_group=w_group,
             b_group=b_group, w_sub=w_sub, b_sub=b_sub, w1=w1, w3=w3, w2=w2, ln2_g=ln2_g, ln2_b=ln2_b)
    tab_p, tab_m = _rope_tables(seq)
    xt = x.reshape(batch * seq, d_model)
    for l in range(depth):
        xt = _layer(xt, _layer_weights(p, l), tab_p, tab_m, batch, seq, alpha)
    return xt.reshape(batch, seq, d_model)
```

```python
import functools

import jax
import jax.numpy as jnp
from jax import lax
from jax.experimental import pallas as pl
from jax.experimental.pallas import tpu as pltpu

F32 = jnp.float32
BF16 = jnp.bfloat16
I32 = jnp.int32

HEAD_DIM = 64
ROT_DIM = HEAD_DIM // 4
ROPE_THETA = 500000.0
EPS = 1e-6
MLA_HEADS = 8
MLA_Q_RANK = 256
MLA_KV_RANK = 128
MLA_NOPE = 64
MLA_ROPE = 32
MLA_V = 64
DIL_GROUPS = ((128, 1), (512, 4), (2048, 16))
DIL_HEADS = 4
DSA_HEADS = 8
IDX_HEADS = 8
IDX_DIM = 64
TOPK_MAX = 256
N_GROUPS = 4
EXPERTS_PER_GROUP = 8
N_EXPERTS = N_GROUPS * EXPERTS_PER_GROUP
D_EXPERT = 256
N_BRANCH = 3

LANES = 128
VMEM_LIMIT = 56 * 1024 * 1024
NEG_INF = float("-inf")
INT_MIN = -2 ** 31
M_FLOOR = -1e30
LOG2E = 1.4426950408889634
ROW_ALIGN = 16


def _params(*sem):
    return pltpu.CompilerParams(dimension_semantics=sem, vmem_limit_bytes=VMEM_LIMIT)


def _dot(a, b):
    return jnp.dot(a, b, preferred_element_type=F32)


def _dot_t(a, b):
    return lax.dot_general(a, b, (((1,), (1,)), ((), ())), preferred_element_type=F32)


def _rope_chunk(v, tab_ref, kind, shift):
    c = tab_ref[kind, 0]
    s1 = tab_ref[kind, 1]
    s2 = tab_ref[kind, 2]
    return v * c + pltpu.roll(v, shift, 1) * s1 + pltpu.roll(v, LANES - shift, 1) * s2


def _layer_norm(z, g, b):
    mu = jnp.mean(z, axis=-1, keepdims=True)
    zc = z - mu
    var = jnp.mean(zc * zc, axis=-1, keepdims=True)
    return zc * lax.rsqrt(var + EPS) * g + b


def _proj_kernel(*refs, segs, kinds, dils, shift):
    x_ref, w_ref, tab_ref = refs[:3]
    outs, scr = refs[3:3 + len(segs)], refs[3 + len(segs)]
    xb = x_ref[...].astype(BF16)
    tm = xb.shape[0]
    for (c0, width), d, o_ref in zip(segs, dils, outs):
        acc = _dot(xb, w_ref[:, c0:c0 + width])
        nch = width // LANES
        for c in range(nch):
            sl = slice(c * LANES, (c + 1) * LANES)
            v = acc[:, sl]
            kind = kinds[c0 // LANES + c]
            if kind >= 0:
                v = _rope_chunk(v, tab_ref, kind, shift)
            if d == 1:
                o_ref[:, sl] = v.astype(o_ref.dtype)
            else:
                scr[c] = v
        if d > 1:
            for r in range(d):
                for c in range(nch):
                    o_ref[r, :, c * LANES:(c + 1) * LANES] = (
                        scr[c, pl.ds(r, tm // d, stride=d), :].astype(o_ref.dtype))


def _proj(x, w, tab, batch, seq, *, segs, dtypes, kinds, dils, shift=ROT_DIM // 2, tm=256):
    T, K = x.shape
    spt = seq // tm
    out_specs, out_shape = [], []
    for (_, wd), dt, d in zip(segs, dtypes, dils):
        if d == 1:
            out_specs.append(pl.BlockSpec((tm, wd), lambda i: (i, 0)))
            out_shape.append(jax.ShapeDtypeStruct((T, wd), dt))
        else:
            out_specs.append(pl.BlockSpec((None, d, tm // d, wd), lambda i: (i // spt, 0, i % spt, 0)))
            out_shape.append(jax.ShapeDtypeStruct((batch, d, seq // d, wd), dt))
    kern = functools.partial(_proj_kernel, segs=segs, kinds=kinds, dils=dils, shift=shift)
    outs = pl.pallas_call(
        kern, grid=(T // tm,),
        in_specs=[pl.BlockSpec((tm, K), lambda i: (i, 0)), pl.BlockSpec(w.shape, lambda i: (0, 0)),
                  pl.BlockSpec((tab.shape[0], 3, tm, LANES), lambda i: (0, 0, i % spt, 0))],
        out_specs=out_specs, out_shape=out_shape,
        scratch_shapes=[pltpu.VMEM((max(wd for _, wd in segs) // LANES, tm, LANES), F32)],
        compiler_params=_params("parallel"), name="in_proj")(x, w, tab)
    return [o.reshape(T, o.shape[-1]) for o in outs]


def _mla_up_kernel(g_ref, qg_ref, kvg_ref, wq_ref, wk_ref, wv_ref, tab_ref, q_out, k_out, v_out):
    g = g_ref[...]
    cq = g[:, :MLA_Q_RANK]
    ckv = g[:, MLA_Q_RANK:MLA_Q_RANK + MLA_KV_RANK]
    kr = g[:, MLA_Q_RANK + MLA_KV_RANK:]
    cqn = (cq * lax.rsqrt(jnp.mean(cq * cq, axis=-1, keepdims=True) + EPS) * qg_ref[...]).astype(BF16)
    ckvn = (ckv * lax.rsqrt(jnp.mean(ckv * ckv, axis=-1, keepdims=True) + EPS) * kvg_ref[...]).astype(BF16)
    q3 = _dot(cqn, wq_ref[...])
    k = _dot(ckvn, wk_ref[...])
    v_out[...] = _dot_t(wv_ref[...], ckvn).astype(v_out.dtype)
    W = MLA_HEADS * LANES
    c, s1, s2 = tab_ref[0, 0], tab_ref[0, 1], tab_ref[0, 2]
    kr_rot = _rope_chunk(kr, tab_ref, 0, MLA_ROPE // 2)
    for h in range(MLA_HEADS):
        sl = slice(h * LANES, (h + 1) * LANES)
        q_rot = (q3[:, sl] * c + q3[:, W + h * LANES:W + (h + 1) * LANES] * s1
                 + q3[:, 2 * W + h * LANES:2 * W + (h + 1) * LANES] * s2)
        q_out[:, sl] = q_rot.astype(q_out.dtype)
        k_out[:, sl] = (k[:, sl] + kr_rot).astype(k_out.dtype)


def _mla_up(grp, qg, kvg, wq, wk, wv, tab, seq, tm=512):
    T = grp.shape[0]
    spt = seq // tm
    const = lambda i: (0, 0)
    return pl.pallas_call(
        _mla_up_kernel, grid=(T // tm,),
        in_specs=[pl.BlockSpec((tm, grp.shape[1]), lambda i: (i, 0)),
                  pl.BlockSpec(qg.shape, const), pl.BlockSpec(kvg.shape, const),
                  pl.BlockSpec(wq.shape, const), pl.BlockSpec(wk.shape, const), pl.BlockSpec(wv.shape, const),
                  pl.BlockSpec((1, 3, tm, LANES), lambda i: (0, 0, i % spt, 0))],
        out_specs=[pl.BlockSpec((tm, MLA_HEADS * LANES), lambda i: (i, 0)),
                   pl.BlockSpec((tm, MLA_HEADS * LANES), lambda i: (i, 0)),
                   pl.BlockSpec((MLA_HEADS * MLA_V, tm), lambda i: (0, i))],
        out_shape=[jax.ShapeDtypeStruct((T, MLA_HEADS * LANES), BF16),
                   jax.ShapeDtypeStruct((T, MLA_HEADS * LANES), BF16),
                   jax.ShapeDtypeStruct((MLA_HEADS * MLA_V, T), BF16)],
        compiler_params=_params("parallel"), name="mla_up")(grp, qg, kvg, wq, wk, wv, tab)


def _mla_attn_kernel(q_ref, k_ref, vt_ref, o_ref, acc_s, s_s, p_s, *, tq, c_exp):
    qi = pl.program_id(1)
    krow = lax.broadcasted_iota(I32, (tq, tq), 0)
    qcol = lax.broadcasted_iota(I32, (tq, tq), 1)
    diag_bias = jnp.where(krow <= qcol, 0.0, NEG_INF).astype(F32)
    acc_s[...] = jnp.zeros_like(acc_s)

    def step(j, carry, masked):
        ms, ls = carry
        ks = pl.multiple_of(j * tq, tq)
        for h in range(MLA_HEADS):
            kb = k_ref[pl.ds(ks, tq), h * LANES:(h + 1) * LANES]
            s_s[h] = _dot_t(kb, q_ref[:, h * LANES:(h + 1) * LANES])
        new_ms, new_ls, alphas = [], [], []
        for h in range(MLA_HEADS):
            s = s_s[h]
            if masked:
                s = s + diag_bias
            m_new = jnp.maximum(ms[h], jnp.max(s, axis=0, keepdims=True))
            alpha = jnp.exp2((ms[h] - m_new) * c_exp)
            p = jnp.exp2((s - m_new) * c_exp)
            new_ls.append(alpha * ls[h] + jnp.sum(p, axis=0, keepdims=True))
            new_ms.append(m_new)
            alphas.append(alpha)
            p_s[h] = p.astype(BF16)
        for h in range(MLA_HEADS):
            sl = slice(h * MLA_V, (h + 1) * MLA_V)
            acc_s[sl, :] = alphas[h] * acc_s[sl, :] + _dot(vt_ref[sl, pl.ds(ks, tq)], p_s[h])
        return tuple(new_ms), tuple(new_ls)

    init = (tuple(jnp.full((1, tq), NEG_INF, F32) for _ in range(MLA_HEADS)),
            tuple(jnp.zeros((1, tq), F32) for _ in range(MLA_HEADS)))
    carry = lax.fori_loop(0, qi, functools.partial(step, masked=False), init)
    _, ls = step(qi, carry, True)
    for h in range(MLA_HEADS):
        sl = slice(h * MLA_V, (h + 1) * MLA_V)
        acc_s[sl, :] = acc_s[sl, :] / ls[h]
    o_ref[...] = acc_s[...].T.astype(o_ref.dtype)


def _mla_attn(q, k, vt, batch, seq, tq=256):
    T = q.shape[0]
    nq = seq // tq
    c_exp = (MLA_NOPE + MLA_ROPE) ** -0.5 * LOG2E
    kern = functools.partial(_mla_attn_kernel, tq=tq, c_exp=c_exp)
    W = MLA_HEADS * LANES
    return pl.pallas_call(
        kern, grid=(batch, nq),
        in_specs=[pl.BlockSpec((tq, W), lambda b, i: (b * nq + i, 0)),
                  pl.BlockSpec((seq, W), lambda b, i: (b, 0)),
                  pl.BlockSpec((MLA_HEADS * MLA_V, seq), lambda b, i: (0, b))],
        out_specs=pl.BlockSpec((tq, MLA_HEADS * MLA_V), lambda b, i: (b * nq + i, 0)),
        out_shape=jax.ShapeDtypeStruct((T, MLA_HEADS * MLA_V), BF16),
        scratch_shapes=[pltpu.VMEM((MLA_HEADS * MLA_V, tq), F32),
                        pltpu.VMEM((MLA_HEADS, tq, tq), F32), pltpu.VMEM((MLA_HEADS, tq, tq), BF16)],
        compiler_params=_params("parallel", "arbitrary"), name="mla_attn")(q, k, vt)


def _dil_lat_kernel(qkv_ref, halo_ref, o_ref, lse_ref, s_s, p_s, *, band, nsub, sub_per_seq):
    i = pl.program_id(0)
    HW = DIL_HEADS * HEAD_DIM
    v_t = lambda ref, rows: ref[rows, 2 * HW:3 * HW].astype(F32).T.astype(BF16)
    vts = [v_t(qkv_ref, slice(u * band, (u + 1) * band)) for u in range(nsub)]
    qk_ref, qkh_ref = qkv_ref, halo_ref
    kj = lax.broadcasted_iota(I32, (band, band), 0)
    qi = lax.broadcasted_iota(I32, (band, band), 1)
    bias_cur = jnp.where(kj <= qi, 0.0, NEG_INF).astype(F32)
    bias_prev = jnp.where(kj >= qi, 0.0, NEG_INF).astype(F32)
    prevs = []
    for u in range(nsub):
        if sub_per_seq == 1 or (u % sub_per_seq == 0 and nsub % sub_per_seq == 0):
            prevs.append(None)
        elif u > 0:
            prevs.append((qk_ref[(u - 1) * band:u * band, HW:2 * HW], vts[u - 1], bias_prev))
        else:
            has = (i * nsub) % sub_per_seq != 0
            prevs.append((qkh_ref[:, HW:2 * HW], v_t(qkh_ref, slice(0, band)),
                          jnp.where(has, bias_prev, NEG_INF)))

    for u in range(nsub):
        rows = slice(u * band, (u + 1) * band)
        q = qk_ref[rows, 0:HW]
        k_cur = qk_ref[rows, HW:2 * HW]
        for h in range(DIL_HEADS):
            hs = slice(h * HEAD_DIM, (h + 1) * HEAD_DIM)
            s_s[u, h, 0] = _dot_t(k_cur[:, hs], q[:, hs])
            if prevs[u] is not None:
                s_s[u, h, 1] = _dot_t(prevs[u][0][:, hs], q[:, hs])
    stats = {}
    for u in range(nsub):
        for h in range(DIL_HEADS):
            s_c = s_s[u, h, 0] + bias_cur
            m = jnp.max(s_c, axis=0, keepdims=True)
            if prevs[u] is not None:
                s_p = s_s[u, h, 1] + prevs[u][2]
                m = jnp.maximum(m, jnp.max(s_p, axis=0, keepdims=True))
            p_c = jnp.exp(s_c - m)
            l = jnp.sum(p_c, axis=0, keepdims=True)
            p_s[u, h, 0] = p_c.astype(BF16)
            if prevs[u] is not None:
                p_p = jnp.exp(s_p - m)
                l = l + jnp.sum(p_p, axis=0, keepdims=True)
                p_s[u, h, 1] = p_p.astype(BF16)
            stats[u, h] = (m, l)
    pad = jnp.zeros((band - DIL_HEADS, band), F32)
    for u in range(nsub):
        rows = slice(u * band, (u + 1) * band)
        outs, lses = [], []
        for h in range(DIL_HEADS):
            hs = slice(h * HEAD_DIM, (h + 1) * HEAD_DIM)
            m, l = stats[u, h]
            o = _dot(vts[u][hs, :], p_s[u, h, 0])
            if prevs[u] is not None:
                o = o + _dot(prevs[u][1][hs, :], p_s[u, h, 1])
            outs.append(o / l)
            lses.append(m + jnp.log(l))
        o_ref[rows, :] = jnp.concatenate(outs, axis=0).T.astype(o_ref.dtype)
        lse_ref[rows, :] = jnp.concatenate(lses + [pad], axis=0).T


def _dil_lat_attn(qkv, seq, d, band, nsub=4):
    T = qkv.shape[0]
    HW = DIL_HEADS * HEAD_DIM
    M = seq // d
    assert band == LANES and M % band == 0
    sub_per_seq = M // band
    assert sub_per_seq % nsub == 0 or nsub % sub_per_seq == 0
    R = nsub * band
    halo = lambda i: jnp.maximum(i * nsub - 1, 0)
    kern = functools.partial(_dil_lat_kernel, band=band, nsub=nsub, sub_per_seq=sub_per_seq)
    return pl.pallas_call(
        kern, grid=(T // R,),
        in_specs=[pl.BlockSpec((R, 3 * HW), lambda i: (i, 0)),
                  pl.BlockSpec((band, 3 * HW), lambda i: (halo(i), 0))],
        out_specs=[pl.BlockSpec((R, HW), lambda i: (i, 0)), pl.BlockSpec((R, LANES), lambda i: (i, 0))],
        out_shape=[jax.ShapeDtypeStruct((T, HW), BF16), jax.ShapeDtypeStruct((T, LANES), F32)],
        scratch_shapes=[pltpu.VMEM((nsub, DIL_HEADS, 2, band, band), F32),
                        pltpu.VMEM((nsub, DIL_HEADS, 2, band, band), BF16)],
        compiler_params=_params("parallel"), name="dil_lat")(qkv, qkv)


def _dil_merge_kernel(*refs, dils, tmm):
    ng = len(dils)
    o_refs, l_refs = refs[0:2 * ng:2], refs[1:2 * ng:2]
    out_ref, scr = refs[2 * ng], refs[2 * ng + 1]
    nch = DIL_HEADS * HEAD_DIM // LANES
    o_pos, l_pos = [], []
    for g, d in enumerate(dils):
        if d == 1:
            o_pos.append([o_refs[g][:, c * LANES:(c + 1) * LANES].astype(F32) for c in range(nch)])
            l_pos.append(l_refs[g][...])
            continue
        for r in range(d):
            rows = pl.ds(r, tmm // d, stride=d)
            for c in range(nch):
                scr[g, c, rows, :] = o_refs[g][r, :, c * LANES:(c + 1) * LANES].astype(F32)
            scr[g, nch, rows, :] = l_refs[g][r]
        o_pos.append([scr[g, c] for c in range(nch)])
        l_pos.append(scr[g, nch])
    mx = functools.reduce(jnp.maximum, l_pos)
    es = [jnp.exp(l - mx) for l in l_pos]
    den = functools.reduce(lambda a, b: a + b, es)
    hpc = LANES // HEAD_DIM
    for c in range(nch):
        acc = None
        for g in range(ng):
            w = es[g] / den
            wc = jnp.concatenate([jnp.broadcast_to(w[:, c * hpc + k:c * hpc + k + 1], (tmm, HEAD_DIM))
                                  for k in range(hpc)], axis=1)
            term = wc * o_pos[g][c]
            acc = term if acc is None else acc + term
        out_ref[:, c * LANES:(c + 1) * LANES] = acc.astype(out_ref.dtype)


def _dil_merge(os_, ls_, batch, seq, tmm=256):
    dils = tuple(d for _, d in DIL_GROUPS)
    HW = DIL_HEADS * HEAD_DIM
    T = batch * seq
    tps = seq // tmm
    args, in_specs = [], []
    for o, l, d in zip(os_, ls_, dils):
        if d == 1:
            args += [o, l]
            in_specs += [pl.BlockSpec((tmm, HW), lambda i: (i, 0)), pl.BlockSpec((tmm, LANES), lambda i: (i, 0))]
        else:
            M = seq // d
            args += [o.reshape(batch, d, M, HW), l.reshape(batch, d, M, LANES)]
            in_specs += [pl.BlockSpec((None, d, tmm // d, HW), lambda i: (i // tps, 0, i % tps, 0)),
                         pl.BlockSpec((None, d, tmm // d, LANES), lambda i: (i // tps, 0, i % tps, 0))]
    kern = functools.partial(_dil_merge_kernel, dils=dils, tmm=tmm)
    return pl.pallas_call(
        kern, grid=(T // tmm,), in_specs=in_specs,
        out_specs=pl.BlockSpec((tmm, HW), lambda i: (i, 0)),
        out_shape=jax.ShapeDtypeStruct((T, HW), BF16),
        scratch_shapes=[pltpu.VMEM((len(dils), HW // LANES + 1, tmm, LANES), F32)],
        compiler_params=_params("parallel"), name="dil_merge")(*args)


def _dsa_kernel(q_ref, iq_ref, kv_ref, ikw_ref, o_ref, qall_s, iqall_s, kvt_s, key_s, bias_s, j_s, acc_s, p_s,
                *, tq, ck, top, seq):
    qi = pl.program_id(1)
    lo = qi * tq
    nch = (lo + tq + ck - 1) // ck
    krow = lax.broadcasted_iota(I32, (ck, tq), 0)
    qcol = lo + lax.broadcasted_iota(I32, (ck, tq), 1)

    def chunk(c):
        return pl.ds(pl.multiple_of(c * ck, ck), ck)

    @pl.when(qi == 0)
    def _():
        def body(c, _):
            kvt_s[:, chunk(c)] = kv_ref[chunk(c), :].astype(F32).T.astype(BF16)
            return 0
        lax.fori_loop(0, seq // ck, body, 0)

    for h in range(DSA_HEADS):
        qall_s[h * tq:(h + 1) * tq, :] = q_ref[:, h * HEAD_DIM:(h + 1) * HEAD_DIM]
    for h in range(IDX_HEADS):
        iqall_s[h * tq:(h + 1) * tq, :] = iq_ref[:, h * IDX_DIM:(h + 1) * IDX_DIM]

    @pl.when(lo + tq <= top)
    def _():
        def body(c, _):
            bias_s[chunk(c), :] = jnp.where(c * ck + krow <= qcol, 0.0, NEG_INF).astype(F32)
            return 0
        lax.fori_loop(0, nch, body, 0)

    @pl.when(lo + tq > top)
    def _():
        iw_t = ikw_ref[pl.ds(pl.multiple_of(lo, tq), tq), :].T[IDX_DIM:IDX_DIM + IDX_HEADS, :]
        iw_t = iw_t * (IDX_HEADS ** -0.5 * IDX_DIM ** -0.5)

        def score_chunk(c):
            ik = ikw_ref[chunk(c), 0:IDX_DIM].astype(BF16)
            r = _dot_t(ik, iqall_s[...])
            sc = jnp.zeros((ck, tq), F32)
            for h in range(IDX_HEADS):
                sc = sc + iw_t[h:h + 1, :] * jnp.maximum(r[:, h * tq:(h + 1) * tq], 0.0)
            sc = jnp.where(c * ck + krow <= qcol, sc, NEG_INF)
            bits = lax.bitcast_convert_type(sc, I32)
            key_s[chunk(c), :] = bits ^ ((bits >> 31) & 0x7FFFFFFF)

        def score_body(i, _):
            score_chunk(2 * i)
            score_chunk(jnp.minimum(2 * i + 1, nch - 1))
            return 0
        lax.fori_loop(0, (nch + 1) // 2, score_body, 0)

        def count(pred):
            def body(c, acc):
                ind = pred(key_s[chunk(c), :], c)
                return acc + jnp.sum(ind.reshape(ck // 64, 64, tq), axis=0)
            acc = lax.fori_loop(0, nch, body, jnp.zeros((64, tq), F32))
            return jnp.sum(acc, axis=0, keepdims=True)

        def count_ge(cand):
            return count(lambda kc, c: jnp.where(kc >= cand, 1.0, 0.0))

        topf = float(top)
        tau = jnp.where(count_ge(jnp.zeros((1, tq), I32)) >= topf, 0, INT_MIN).astype(I32)

        def search(i, tau):
            cand = tau | jnp.left_shift(jnp.int32(1), 30 - i)
            return jnp.where(count_ge(cand) >= topf, cand, tau)
        tau = lax.fori_loop(0, 31, search, tau)

        need = topf - count_ge(tau + 1)
        j_s[...] = jnp.full(j_s.shape, seq, I32)

        @pl.when(jnp.max(count_ge(tau)) > topf)
        def _():
            def count_eq_lt(J):
                return count(lambda kc, c: jnp.where(kc == tau, jnp.where(c * ck + krow < J, 1.0, 0.0), 0.0))

            nbits = seq.bit_length()

            def tie(i, J):
                cand = J + jnp.left_shift(jnp.int32(1), nbits - 1 - i)
                return jnp.where(count_eq_lt(cand) <= need, cand, J)
            J = lax.fori_loop(0, nbits, tie, jnp.zeros((1, tq), I32))
            j_s[...] = jnp.broadcast_to(J, j_s.shape)

        J = j_s[0:1, :]

        def bias_body(c, _):
            kc = key_s[chunk(c), :]
            sel = jnp.where(kc > tau, 0.0,
                            jnp.where(kc == tau, jnp.where(c * ck + krow < J, 0.0, NEG_INF), NEG_INF))
            bias_s[chunk(c), :] = sel.astype(F32)
            return 0
        lax.fori_loop(0, nch, bias_body, 0)

    acc_s[...] = jnp.zeros_like(acc_s)

    def step(c, carry, slot):
        ms, ls = carry
        kc = kv_ref[chunk(c), 0:HEAD_DIM]
        bias = bias_s[chunk(c), :]
        vt = kvt_s[HEAD_DIM:2 * HEAD_DIM, chunk(c)]
        new_ms, new_ls, alphas = [], [], []
        for h in range(DSA_HEADS):
            if h % 2 == 0:
                s_pair = _dot_t(kc, qall_s[h * tq:(h + 2) * tq, :])
            s = s_pair[:, (h % 2) * tq:(h % 2 + 1) * tq] + bias
            m_new = jnp.maximum(ms[h], jnp.max(s, axis=0, keepdims=True))
            alpha = jnp.exp(ms[h] - m_new)
            p = jnp.exp(s - m_new)
            new_ls.append(alpha * ls[h] + jnp.sum(p, axis=0, keepdims=True))
            new_ms.append(m_new)
            alphas.append(alpha)
            p_s[slot, :, h * tq:(h + 1) * tq] = p.astype(BF16)
        acc_s[...] = acc_s[...] * jnp.concatenate(alphas, axis=1) + _dot(vt, p_s[slot])
        return tuple(new_ms), tuple(new_ls)

    def pair(i, carry):
        return step(2 * i + 1, step(2 * i, carry, 0), 1)

    init = (tuple(jnp.full((1, tq), M_FLOOR, F32) for _ in range(DSA_HEADS)),
            tuple(jnp.zeros((1, tq), F32) for _ in range(DSA_HEADS)))
    carry = lax.fori_loop(0, nch // 2, pair, init)
    _, ls = lax.cond(nch % 2 == 1, lambda cr: step(nch - 1, cr, 0), lambda cr: cr, carry)
    o_all = acc_s[...] / jnp.concatenate(ls, axis=1)
    for h in range(0, DSA_HEADS, 2):
        pair = jnp.concatenate([o_all[:, h * tq:(h + 1) * tq], o_all[:, (h + 1) * tq:(h + 2) * tq]], axis=0)
        o_ref[:, h * HEAD_DIM:(h + 2) * HEAD_DIM] = pair.T.astype(o_ref.dtype)


def _dsa_attn(q, kv, iq, ikw, batch, seq, tq=LANES, ck=256):
    T = q.shape[0]
    nq = seq // tq
    top = min(TOPK_MAX, seq // 4)
    assert tq == LANES and top % tq == 0 and seq % ck == 0
    kern = functools.partial(_dsa_kernel, tq=tq, ck=ck, top=top, seq=seq)
    W = DSA_HEADS * HEAD_DIM
    return pl.pallas_call(
        kern, grid=(batch, nq),
        in_specs=[pl.BlockSpec((tq, W), lambda b, i: (b * nq + i, 0)),
                  pl.BlockSpec((tq, IDX_HEADS * IDX_DIM), lambda b, i: (b * nq + i, 0)),
                  pl.BlockSpec((seq, LANES), lambda b, i: (b, 0)),
                  pl.BlockSpec((seq, LANES), lambda b, i: (b, 0))],
        out_specs=pl.BlockSpec((tq, W), lambda b, i: (b * nq + i, 0)),
        out_shape=jax.ShapeDtypeStruct((T, W), BF16),
        scratch_shapes=[pltpu.VMEM((DSA_HEADS * tq, HEAD_DIM), BF16),
                        pltpu.VMEM((IDX_HEADS * tq, IDX_DIM), BF16),
                        pltpu.VMEM((LANES, seq), BF16),
                        pltpu.VMEM((seq, tq), I32),
                        pltpu.VMEM((seq, tq), F32),
                        pltpu.VMEM((8, tq), I32),
                        pltpu.VMEM((HEAD_DIM, DSA_HEADS * tq), F32),
                        pltpu.VMEM((2, ck, DSA_HEADS * tq), BF16)],
        compiler_params=_params("arbitrary", "arbitrary"), name="dsa_attn")(q, iq, kv, ikw)


def _mix_out_kernel(x_ref, oa_ref, ob_ref, oc_ref, wg_ref, bg_ref, wa_ref, wb_ref, wc_ref, wo_ref,
                    lg_ref, lb_ref, o_ref, *, alpha, d_model):
    x = x_ref[...]
    xb = x.astype(BF16)
    merged = None
    for br, (o_r, w_r) in enumerate(((oa_ref, wa_ref), (ob_ref, wb_ref), (oc_ref, wc_ref))):
        sl = slice(br * d_model, (br + 1) * d_model)
        gate = jax.nn.sigmoid(_dot(xb, wg_ref[:, sl]) + bg_ref[:, sl])
        term = gate * _dot(o_r[...], w_r[...])
        merged = term if merged is None else merged + term
    mix = _dot(merged.astype(BF16), wo_ref[...])
    o_ref[...] = _layer_norm(alpha * x + mix, lg_ref[...], lb_ref[...])


def _mix_out(x, oa, ob, oc, wg, bg, wa, wb, wc, wo, lg, lb, alpha, tm=256):
    T, D = x.shape
    const = lambda i: (0, 0)
    full = lambda a: pl.BlockSpec(a.shape, const)
    row = lambda a: pl.BlockSpec((tm, a.shape[1]), lambda i: (i, 0))
    kern = functools.partial(_mix_out_kernel, alpha=alpha, d_model=D)
    return pl.pallas_call(
        kern, grid=(T // tm,),
        in_specs=[row(x), row(oa), row(ob), row(oc), full(wg), full(bg), full(wa), full(wb), full(wc),
                  full(wo), full(lg), full(lb)],
        out_specs=pl.BlockSpec((tm, D), lambda i: (i, 0)),
        out_shape=jax.ShapeDtypeStruct((T, D), F32),
        compiler_params=_params("parallel"), name="mix_out")(x, oa, ob, oc, wg, bg, wa, wb, wc, wo, lg, lb)


def _route(x, wr_hi_ref, wr_lo_ref, br_ref):
    xh = x.astype(BF16)
    xl = (x - xh.astype(F32)).astype(BF16)
    logits = _dot(xh, wr_hi_ref[...]) + _dot(xl, wr_hi_ref[...]) + _dot(xh, wr_lo_ref[...]) + br_ref[...]
    lane = lax.broadcasted_iota(I32, logits.shape, 1).astype(F32)
    none = float(LANES)
    glog = jnp.where(lane < N_GROUPS, logits, NEG_INF)
    gmax = jnp.max(glog, axis=1, keepdims=True)
    g_p = 1.0 / jnp.sum(jnp.exp(glog - gmax), axis=1, keepdims=True)
    g_idx = jnp.min(jnp.where(glog == gmax, lane, none), axis=1, keepdims=True)
    first = N_GROUPS + g_idx * EXPERTS_PER_GROUP
    sub = jnp.where(lane >= first, jnp.where(lane < first + EXPERTS_PER_GROUP, logits, NEG_INF), NEG_INF)
    v1 = jnp.max(sub, axis=1, keepdims=True)
    i1 = jnp.min(jnp.where(sub == v1, lane, none), axis=1, keepdims=True)
    sub2 = jnp.where(lane == i1, NEG_INF, sub)
    v2 = jnp.max(sub2, axis=1, keepdims=True)
    i2 = jnp.min(jnp.where(sub2 == v2, lane, none), axis=1, keepdims=True)
    e2 = jnp.exp(v2 - v1)
    w1 = g_p / (1.0 + e2)
    w2 = g_p * e2 / (1.0 + e2)
    return jnp.where(lane == i1, w1, jnp.where(lane == i2, w2, 0.0)), g_idx


def _split3(v):
    hi = v.astype(BF16)
    r = v - hi.astype(F32)
    mid = r.astype(BF16)
    return hi, mid, (r - mid.astype(F32)).astype(BF16)


def _moe_kernel(x_ref, wrh_ref, wrl_ref, br_ref, w1_ref, w3_ref, w2_ref, lg_ref, lb_ref, o_ref,
                xs_s, combs_s, pt_s, acc_s, rng_s, *, alpha, tm, rb, ts, eps):
    e = pl.program_id(1)

    @pl.when(e == 0)
    def _():
        x = x_ref[...]
        comb, g_idx = _route(x, wrh_ref, wrl_ref, br_ref)
        lane = lax.broadcasted_iota(I32, (tm, LANES), 1).astype(F32)
        onehot_g = jnp.where(lane == g_idx, 1.0, 0.0)
        ti = lax.broadcasted_iota(I32, (tm, tm), 0)
        tj = lax.broadcasted_iota(I32, (tm, tm), 1)
        earlier = jnp.where(tj < ti, 1.0, 0.0).astype(BF16)
        rank = _dot(earlier, onehot_g.astype(BF16))
        counts = jnp.sum(onehot_g, axis=0, keepdims=True)
        padded = jnp.ceil(counts * (1.0 / ROW_ALIGN)) * ROW_ALIGN
        lane1 = lax.broadcasted_iota(I32, (1, LANES), 1)
        off = jnp.zeros((1, LANES), F32)
        for k in range(1, N_GROUPS):
            off = off + jnp.where(lane1 >= k, pltpu.roll(padded, k, 1), 0.0)
        pos = jnp.sum(onehot_g * (off + rank), axis=1, keepdims=True)
        pos_row = jnp.broadcast_to(pos, (tm, LANES)).T[0:1, :]
        sj = lax.broadcasted_iota(I32, (tm, ts), 1)
        si = lax.broadcasted_iota(I32, (ts, tm), 0)
        pt_s[...] = jnp.where(sj.astype(F32) == pos, 1.0, 0.0).astype(BF16)
        perm = jnp.where(si.astype(F32) == pos_row, 1.0, 0.0).astype(BF16)
        xs_s[...] = _dot(perm, x.astype(BF16)).astype(BF16)
        c_hi, c_mid, c_lo = _split3(comb)
        combs_s[...] = _dot(perm, c_hi) + _dot(perm, c_mid) + _dot(perm, c_lo)
        acc_s[...] = jnp.zeros_like(acc_s)
        for g in range(N_GROUPS):
            start = jnp.sum(jnp.where(lane1 == g, off, 0.0)).astype(I32)
            cnt = jnp.sum(jnp.where(lane1 == g, counts, 0.0)).astype(I32)
            rng_s[g] = start
            rng_s[N_GROUPS + g] = (cnt + rb - 1) // rb

    g = (e * eps) // EXPERTS_PER_GROUP
    lane_b = lax.broadcasted_iota(I32, (rb, LANES), 1)
    start = rng_s[g]

    def block(b, _):
        rows = pl.ds(pl.multiple_of(start + b * rb, ROW_ALIGN), rb)
        xb = xs_s[rows, :]
        comb = combs_s[rows, :]
        y = None
        for k in range(eps):
            we = jnp.sum(jnp.where(lane_b == N_GROUPS + e * eps + k, comb, 0.0), axis=1, keepdims=True)
            a = _dot(xb, w1_ref[k])
            b3 = _dot(xb, w3_ref[k])
            hid = (a * jax.nn.sigmoid(a) * b3).astype(BF16)
            yk = we * _dot(hid, w2_ref[k])
            y = yk if y is None else y + yk
        acc_s[rows, :] += y
        return 0
    lax.fori_loop(0, rng_s[N_GROUPS + g], block, 0)

    @pl.when(e == N_EXPERTS // eps - 1)
    def _():
        y = _dot(pt_s[...], acc_s[...].astype(BF16))
        o_ref[...] = _layer_norm(alpha * x_ref[...] + y, lg_ref[...], lb_ref[...])


def _moe(x, wrh, wrl, br, w1, w3, w2, lg, lb, alpha, tm=1024, rb=320, eps=4):
    T, D = x.shape
    assert EXPERTS_PER_GROUP % eps == 0
    const = lambda i, e: (0, 0)
    full = lambda a: pl.BlockSpec(a.shape, const)
    ts = -(-(tm + N_GROUPS * ROW_ALIGN + rb) // LANES) * LANES
    kern = functools.partial(_moe_kernel, alpha=alpha, tm=tm, rb=rb, ts=ts, eps=eps)
    return pl.pallas_call(
        kern, grid=(T // tm, N_EXPERTS // eps),
        in_specs=[pl.BlockSpec((tm, D), lambda i, e: (i, 0)), full(wrh), full(wrl), full(br),
                  pl.BlockSpec((eps, D, D_EXPERT), lambda i, e: (e, 0, 0)),
                  pl.BlockSpec((eps, D, D_EXPERT), lambda i, e: (e, 0, 0)),
                  pl.BlockSpec((eps, D_EXPERT, D), lambda i, e: (e, 0, 0)),
                  full(lg), full(lb)],
        out_specs=pl.BlockSpec((tm, D), lambda i, e: (i, 0)),
        out_shape=jax.ShapeDtypeStruct((T, D), F32),
        scratch_shapes=[pltpu.VMEM((ts, D), BF16),
                        pltpu.VMEM((ts, LANES), F32),
                        pltpu.VMEM((tm, ts), BF16),
                        pltpu.VMEM((ts, D), F32),
                        pltpu.SMEM((2 * N_GROUPS,), I32)],
        compiler_params=_params("parallel", "arbitrary"), name="moe")(x, wrh, wrl, br, w1, w3, w2, lg, lb)


def _rope_tables(seq):
    pos = jnp.arange(seq, dtype=F32)[:, None]
    one = lambda n: jnp.ones((seq, n), F32)
    zero = lambda n: jnp.zeros((seq, n), F32)

    inv_p = ROPE_THETA ** (-jnp.arange(0, ROT_DIM, 2, dtype=F32) / ROT_DIM)
    cp, sp = jnp.cos(pos * inv_p), jnp.sin(pos * inv_p)
    hp = ROT_DIM // 2
    rest = HEAD_DIM - ROT_DIM
    head = (jnp.concatenate([cp, cp, one(rest)], 1),
            jnp.concatenate([zero(hp), sp, zero(rest)], 1),
            jnp.concatenate([-sp, zero(hp), zero(rest)], 1))
    ident = (one(HEAD_DIM), zero(HEAD_DIM), zero(HEAD_DIM))
    both = jnp.stack([jnp.concatenate([a, a], 1) for a in head])
    first = jnp.stack([jnp.concatenate([a, b], 1) for a, b in zip(head, ident)])
    tab_p = jnp.stack([both, first])

    inv_m = ROPE_THETA ** (-jnp.arange(0, MLA_ROPE, 2, dtype=F32) / MLA_ROPE)
    cm, sm = jnp.cos(pos * inv_m), jnp.sin(pos * inv_m)
    hm = MLA_ROPE // 2
    pad = LANES - MLA_NOPE - MLA_ROPE
    tab_m = jnp.stack([jnp.concatenate([one(MLA_NOPE), cm, cm, one(pad)], 1),
                       jnp.concatenate([zero(MLA_NOPE), zero(hm), sm, zero(pad)], 1),
                       jnp.concatenate([zero(MLA_NOPE), -sm, zero(hm), zero(pad)], 1)])[None]
    return tab_p, tab_m


def _layer_weights(p, l):
    w_in = p["w_in"][l]
    D = w_in.shape[0]
    z = lambda n: jnp.zeros((D, n), F32)
    o = 0
    cuts = []
    for n in (MLA_Q_RANK, MLA_KV_RANK, MLA_ROPE, 3 * 3 * DIL_HEADS * HEAD_DIM, DSA_HEADS * HEAD_DIM,
              HEAD_DIM, HEAD_DIM, IDX_HEADS * IDX_DIM, IDX_DIM, IDX_HEADS):
        cuts.append(w_in[:, o:o + n])
        o += n
    w_cq, w_ckv, w_kr, w_dil, w_q, w_k, w_v, w_iq, w_ik, w_iw = cuts
    qs = HEAD_DIM ** -0.5
    w_mla = jnp.concatenate([w_cq, w_ckv, z(MLA_NOPE), w_kr, z(LANES - MLA_NOPE - MLA_ROPE)], 1)
    HW = DIL_HEADS * HEAD_DIM
    w_dil = w_dil.reshape(D, len(DIL_GROUPS), 3, HW)
    w_dqkv = jnp.concatenate([w_dil[:, :, 0] * qs, w_dil[:, :, 1], w_dil[:, :, 2]], axis=-1)
    w_dqkv = w_dqkv.transpose(1, 0, 2)
    w_dsa = jnp.concatenate([w_q * qs, w_k, w_v, w_iq, w_ik, w_iw, z(LANES - IDX_DIM - IDX_HEADS)], 1)

    def per_head(w, n_in, n_keep_lo, n_keep_hi):
        r = w.shape[0]
        w = w.reshape(r, MLA_HEADS, n_in)[:, :, n_keep_lo:n_keep_hi]
        w = jnp.pad(w, ((0, 0), (0, 0), (0, LANES - (n_keep_hi - n_keep_lo))))
        return w.reshape(r, MLA_HEADS * LANES)

    w_uq = per_head(p["w_uq"][l], MLA_NOPE + MLA_ROPE, 0, MLA_NOPE + MLA_ROPE)
    w_uq3 = w_uq.reshape(MLA_Q_RANK, MLA_HEADS, LANES)
    w_uq = jnp.concatenate([jnp.roll(w_uq3, sh, axis=2).reshape(MLA_Q_RANK, -1)
                            for sh in (0, MLA_ROPE // 2, -(MLA_ROPE // 2))], axis=1)
    w_uk = per_head(p["w_ukv"][l], MLA_NOPE + MLA_V, 0, MLA_NOPE)
    w_uv = p["w_ukv"][l].reshape(MLA_KV_RANK, MLA_HEADS, MLA_NOPE + MLA_V)[:, :, MLA_NOPE:]
    w_uv = w_uv.reshape(MLA_KV_RANK, MLA_HEADS * MLA_V).T

    w_r = jnp.concatenate([p["w_group"][l], p["w_sub"][l], z(LANES - N_GROUPS - N_EXPERTS)], 1)
    w_r_hi = w_r.astype(BF16)
    w_r_lo = (w_r - w_r_hi.astype(F32)).astype(BF16)
    b_r = jnp.concatenate([p["b_group"][l], p["b_sub"][l], jnp.zeros((LANES - N_GROUPS - N_EXPERTS,), F32)])

    bf = lambda a: a.astype(BF16)
    return dict(
        w_proj=bf(jnp.concatenate([w_mla, w_dsa] + [w_dqkv[g] for g in range(len(DIL_GROUPS))], axis=1)),
        n_mla=w_mla.shape[1],
        q_g=p["q_norm_g"][l][None], kv_g=p["kv_norm_g"][l][None],
        w_uq=bf(w_uq), w_uk=bf(w_uk), w_uv=bf(w_uv),
        w_gate=bf(p["w_gate"][l]), b_gate=p["b_gate"][l][None],
        w_a=bf(p["w_a"][l]), w_b=bf(p["w_b"][l]), w_c=bf(p["w_c"][l]), w_o=bf(p["w_o"][l]),
        ln1_g=p["ln1_g"][l][None], ln1_b=p["ln1_b"][l][None],
        w_r_hi=w_r_hi, w_r_lo=w_r_lo, b_r=b_r[None],
        w1=bf(p["w1"][l]), w3=bf(p["w3"][l]), w2=bf(p["w2"][l]),
        ln2_g=p["ln2_g"][l][None], ln2_b=p["ln2_b"][l][None])


def _project(xt, w, tab_p, batch, seq):
    HW = DIL_HEADS * HEAD_DIM
    qw, iw_ = DSA_HEADS * HEAD_DIM, IDX_HEADS * IDX_DIM
    widths = [w["n_mla"], qw, LANES, iw_, LANES] + [3 * HW] * len(DIL_GROUPS)
    starts = [sum(widths[:k]) for k in range(len(widths))]
    rope2, rope1, none = 0, 1, -1
    kinds = ((none,) * (w["n_mla"] // LANES)
             + (rope2,) * (qw // LANES) + (rope1,) + (rope2,) * (iw_ // LANES) + (rope1,)
             + ((rope2,) * (2 * HW // LANES) + (none,) * (HW // LANES)) * len(DIL_GROUPS))
    return _proj(xt, w["w_proj"], tab_p, batch, seq, segs=tuple(zip(starts, widths)),
                 dtypes=(F32, BF16, BF16, BF16, F32) + (BF16,) * len(DIL_GROUPS), kinds=kinds,
                 dils=(1,) * 5 + tuple(d for _, d in DIL_GROUPS))


def _layer(xt, w, tab_p, tab_m, batch, seq, alpha):
    grp, q_c, kv_c, iq_c, ikw_c, *dil_qkv = _project(xt, w, tab_p, batch, seq)
    dil_o, dil_lse = [], []
    for qkv_g, (window, d) in zip(dil_qkv, DIL_GROUPS):
        o_g, lse_g = _dil_lat_attn(qkv_g, seq, d, band=window // d)
        dil_o.append(o_g)
        dil_lse.append(lse_g)

    q_a, k_a, v_a = _mla_up(grp, w["q_g"], w["kv_g"], w["w_uq"], w["w_uk"], w["w_uv"], tab_m, seq)
    o_a = _mla_attn(q_a, k_a, v_a, batch, seq)
    o_b = _dil_merge(dil_o, dil_lse, batch, seq)
    o_c = _dsa_attn(q_c, kv_c, iq_c, ikw_c, batch, seq)

    x1 = _mix_out(xt, o_a, o_b, o_c, w["w_gate"], w["b_gate"], w["w_a"], w["w_b"], w["w_c"], w["w_o"],
                  w["ln1_g"], w["ln1_b"], alpha)
    return _moe(x1, w["w_r_hi"], w["w_r_lo"], w["b_r"], w["w1"], w["w3"], w["w2"],
                w["ln2_g"], w["ln2_b"], alpha)


def kernel(x, w_in, q_norm_g, w_uq, kv_norm_g, w_ukv, w_gate, b_gate, w_a, w_b, w_c, w_o, ln1_g, ln1_b,
           w_group, b_group, w_sub, b_sub, w1, w3, w2, ln2_g, ln2_b):
    batch, seq, d_model = x.shape
    depth = w_in.shape[0]
    alpha = (2 * depth) ** 0.25
    p = dict(w_in=w_in, q_norm_g=q_norm_g, w_uq=w_uq, kv_norm_g=kv_norm_g, w_ukv=w_ukv, w_gate=w_gate,
             b_gate=b_gate, w_a=w_a, w_b=w_b, w_c=w_c, w_o=w_o, ln1_g=ln1_g, ln1_b=ln1_b, w_group=w_group,
             b_group=b_group, w_sub=w_sub, b_sub=b_sub, w1=w1, w3=w3, w2=w2, ln2_g=ln2_g, ln2_b=ln2_b)
    tab_p, tab_m = _rope_tables(seq)
    xt = x.reshape(batch * seq, d_model)
    for l in range(depth):
        xt = _layer(xt, _layer_weights(p, l), tab_p, tab_m, batch, seq, alpha)
    return xt.reshape(batch, seq, d_model)
```

```python
import functools

import jax
import jax.numpy as jnp
from jax import lax
from jax.experimental import pallas as pl
from jax.experimental.pallas import tpu as pltpu

F32 = jnp.float32
BF16 = jnp.bfloat16
I32 = jnp.int32

HEAD_DIM = 64
ROT_DIM = HEAD_DIM // 4
ROPE_THETA = 500000.0
EPS = 1e-6
MLA_HEADS = 8
MLA_Q_RANK = 256
MLA_KV_RANK = 128
MLA_NOPE = 64
MLA_ROPE = 32
MLA_V = 64
DIL_GROUPS = ((128, 1), (512, 4), (2048, 16))
DIL_HEADS = 4
DSA_HEADS = 8
IDX_HEADS = 8
IDX_DIM = 64
TOPK_MAX = 256
N_GROUPS = 4
EXPERTS_PER_GROUP = 8
N_EXPERTS = N_GROUPS * EXPERTS_PER_GROUP
D_EXPERT = 256
N_BRANCH = 3

LANES = 128
VMEM_LIMIT = 56 * 1024 * 1024
NEG_INF = float("-inf")
INT_MIN = -2 ** 31
M_FLOOR = -1e30
LOG2E = 1.4426950408889634
ROW_ALIGN = 16


def _params(*sem):
    return pltpu.CompilerParams(dimension_semantics=sem, vmem_limit_bytes=VMEM_LIMIT)


def _dot(a, b):
    return jnp.dot(a, b, preferred_element_type=F32)


def _dot_t(a, b):
    return lax.dot_general(a, b, (((1,), (1,)), ((), ())), preferred_element_type=F32)


def _rope_chunk(v, tab_ref, kind, shift):
    c = tab_ref[kind, 0]
    s1 = tab_ref[kind, 1]
    s2 = tab_ref[kind, 2]
    return v * c + pltpu.roll(v, shift, 1) * s1 + pltpu.roll(v, LANES - shift, 1) * s2


def _layer_norm(z, g, b):
    mu = jnp.mean(z, axis=-1, keepdims=True)
    zc = z - mu
    var = jnp.mean(zc * zc, axis=-1, keepdims=True)
    return zc * lax.rsqrt(var + EPS) * g + b


def _proj_kernel(*refs, segs, kinds, dils, shift):
    x_ref, w_ref, tab_ref = refs[:3]
    outs, scr = refs[3:3 + len(segs)], refs[3 + len(segs)]
    xb = x_ref[...].astype(BF16)
    tm = xb.shape[0]
    for (c0, width), d, o_ref in zip(segs, dils, outs):
        acc = _dot(xb, w_ref[:, c0:c0 + width])
        nch = width // LANES
        for c in range(nch):
            sl = slice(c * LANES, (c + 1) * LANES)
            v = acc[:, sl]
            kind = kinds[c0 // LANES + c]
            if kind >= 0:
                v = _rope_chunk(v, tab_ref, kind, shift)
            if d == 1:
                o_ref[:, sl] = v.astype(o_ref.dtype)
            else:
                scr[c] = v
        if d > 1:
            for r in range(d):
                for c in range(nch):
                    o_ref[r, :, c * LANES:(c + 1) * LANES] = (
                        scr[c, pl.ds(r, tm // d, stride=d), :].astype(o_ref.dtype))


def _proj(x, w, tab, batch, seq, *, segs, dtypes, kinds, dils, shift=ROT_DIM // 2, tm=256):
    T, K = x.shape
    spt = seq // tm
    out_specs, out_shape = [], []
    for (_, wd), dt, d in zip(segs, dtypes, dils):
        if d == 1:
            out_specs.append(pl.BlockSpec((tm, wd), lambda i: (i, 0)))
            out_shape.append(jax.ShapeDtypeStruct((T, wd), dt))
        else:
            out_specs.append(pl.BlockSpec((None, d, tm // d, wd), lambda i: (i // spt, 0, i % spt, 0)))
            out_shape.append(jax.ShapeDtypeStruct((batch, d, seq // d, wd), dt))
    kern = functools.partial(_proj_kernel, segs=segs, kinds=kinds, dils=dils, shift=shift)
    outs = pl.pallas_call(
        kern, grid=(T // tm,),
        in_specs=[pl.BlockSpec((tm, K), lambda i: (i, 0)), pl.BlockSpec(w.shape, lambda i: (0, 0)),
                  pl.BlockSpec((tab.shape[0], 3, tm, LANES), lambda i: (0, 0, i % spt, 0))],
        out_specs=out_specs, out_shape=out_shape,
        scratch_shapes=[pltpu.VMEM((max(wd for _, wd in segs) // LANES, tm, LANES), F32)],
        compiler_params=_params("parallel"), name="in_proj")(x, w, tab)
    return [o.reshape(T, o.shape[-1]) for o in outs]


def _mla_up_kernel(g_ref, qg_ref, kvg_ref, wq_ref, wk_ref, wv_ref, tab_ref, q_out, k_out, v_out):
    g = g_ref[...]
    cq = g[:, :MLA_Q_RANK]
    ckv = g[:, MLA_Q_RANK:MLA_Q_RANK + MLA_KV_RANK]
    kr = g[:, MLA_Q_RANK + MLA_KV_RANK:]
    cqn = (cq * lax.rsqrt(jnp.mean(cq * cq, axis=-1, keepdims=True) + EPS) * qg_ref[...]).astype(BF16)
    ckvn = (ckv * lax.rsqrt(jnp.mean(ckv * ckv, axis=-1, keepdims=True) + EPS) * kvg_ref[...]).astype(BF16)
    q3 = _dot(cqn, wq_ref[...])
    k = _dot(ckvn, wk_ref[...])
    v_out[...] = _dot_t(wv_ref[...], ckvn).astype(v_out.dtype)
    W = MLA_HEADS * LANES
    c, s1, s2 = tab_ref[0, 0], tab_ref[0, 1], tab_ref[0, 2]
    kr_rot = _rope_chunk(kr, tab_ref, 0, MLA_ROPE // 2)
    for h in range(MLA_HEADS):
        sl = slice(h * LANES, (h + 1) * LANES)
        q_rot = (q3[:, sl] * c + q3[:, W + h * LANES:W + (h + 1) * LANES] * s1
                 + q3[:, 2 * W + h * LANES:2 * W + (h + 1) * LANES] * s2)
        q_out[:, sl] = q_rot.astype(q_out.dtype)
        k_out[:, sl] = (k[:, sl] + kr_rot).astype(k_out.dtype)


def _mla_up(grp, qg, kvg, wq, wk, wv, tab, seq, tm=512):
    T = grp.shape[0]
    spt = seq // tm
    const = lambda i: (0, 0)
    return pl.pallas_call(
        _mla_up_kernel, grid=(T // tm,),
        in_specs=[pl.BlockSpec((tm, grp.shape[1]), lambda i: (i, 0)),
                  pl.BlockSpec(qg.shape, const), pl.BlockSpec(kvg.shape, const),
                  pl.BlockSpec(wq.shape, const), pl.BlockSpec(wk.shape, const), pl.BlockSpec(wv.shape, const),
                  pl.BlockSpec((1, 3, tm, LANES), lambda i: (0, 0, i % spt, 0))],
        out_specs=[pl.BlockSpec((tm, MLA_HEADS * LANES), lambda i: (i, 0)),
                   pl.BlockSpec((tm, MLA_HEADS * LANES), lambda i: (i, 0)),
                   pl.BlockSpec((MLA_HEADS * MLA_V, tm), lambda i: (0, i))],
        out_shape=[jax.ShapeDtypeStruct((T, MLA_HEADS * LANES), BF16),
                   jax.ShapeDtypeStruct((T, MLA_HEADS * LANES), BF16),
                   jax.ShapeDtypeStruct((MLA_HEADS * MLA_V, T), BF16)],
        compiler_params=_params("parallel"), name="mla_up")(grp, qg, kvg, wq, wk, wv, tab)


def _mla_attn_kernel(q_ref, k_ref, vt_ref, o_ref, acc_s, s_s, p_s, *, tq, c_exp):
    qi = pl.program_id(1)
    krow = lax.broadcasted_iota(I32, (tq, tq), 0)
    qcol = lax.broadcasted_iota(I32, (tq, tq), 1)
    diag_bias = jnp.where(krow <= qcol, 0.0, NEG_INF).astype(F32)
    acc_s[...] = jnp.zeros_like(acc_s)

    def step(j, carry, masked, slot):
        ms, ls = carry
        ks = pl.multiple_of(j * tq, tq)
        for h in range(MLA_HEADS):
            kb = k_ref[pl.ds(ks, tq), h * LANES:(h + 1) * LANES]
            s_s[slot, h] = _dot_t(kb, q_ref[:, h * LANES:(h + 1) * LANES])
        new_ms, new_ls, alphas = [], [], []
        for h in range(MLA_HEADS):
            s = s_s[slot, h]
            if masked:
                s = s + diag_bias
            m_new = jnp.maximum(ms[h], jnp.max(s, axis=0, keepdims=True))
            alpha = jnp.exp2((ms[h] - m_new) * c_exp)
            p = jnp.exp2((s - m_new) * c_exp)
            new_ls.append(alpha * ls[h] + jnp.sum(p, axis=0, keepdims=True))
            new_ms.append(m_new)
            alphas.append(alpha)
            p_s[slot, h] = p.astype(BF16)
        for h in range(MLA_HEADS):
            sl = slice(h * MLA_V, (h + 1) * MLA_V)
            acc_s[sl, :] = alphas[h] * acc_s[sl, :] + _dot(vt_ref[sl, pl.ds(ks, tq)], p_s[slot, h])
        return tuple(new_ms), tuple(new_ls)

    init = (tuple(jnp.full((1, tq), NEG_INF, F32) for _ in range(MLA_HEADS)),
            tuple(jnp.zeros((1, tq), F32) for _ in range(MLA_HEADS)))
    def pair(i, carry):
        return step(2 * i + 1, step(2 * i, carry, False, 0), False, 1)

    carry = lax.fori_loop(0, qi // 2, pair, init)
    carry = lax.cond(qi % 2 == 1, lambda cr: step(qi - 1, cr, False, 0), lambda cr: cr, carry)
    _, ls = step(qi, carry, True, 1)
    for h in range(MLA_HEADS):
        sl = slice(h * MLA_V, (h + 1) * MLA_V)
        acc_s[sl, :] = acc_s[sl, :] / ls[h]
    o_ref[...] = acc_s[...].T.astype(o_ref.dtype)


def _mla_attn(q, k, vt, batch, seq, tq=256):
    T = q.shape[0]
    nq = seq // tq
    c_exp = (MLA_NOPE + MLA_ROPE) ** -0.5 * LOG2E
    kern = functools.partial(_mla_attn_kernel, tq=tq, c_exp=c_exp)
    W = MLA_HEADS * LANES
    return pl.pallas_call(
        kern, grid=(batch, nq),
        in_specs=[pl.BlockSpec((tq, W), lambda b, i: (b * nq + i, 0)),
                  pl.BlockSpec((seq, W), lambda b, i: (b, 0)),
                  pl.BlockSpec((MLA_HEADS * MLA_V, seq), lambda b, i: (0, b))],
        out_specs=pl.BlockSpec((tq, MLA_HEADS * MLA_V), lambda b, i: (b * nq + i, 0)),
        out_shape=jax.ShapeDtypeStruct((T, MLA_HEADS * MLA_V), BF16),
        scratch_shapes=[pltpu.VMEM((MLA_HEADS * MLA_V, tq), F32),
                        pltpu.VMEM((2, MLA_HEADS, tq, tq), F32), pltpu.VMEM((2, MLA_HEADS, tq, tq), BF16)],
        compiler_params=_params("parallel", "arbitrary"), name="mla_attn")(q, k, vt)


def _dil_lat_kernel(qkv_ref, halo_ref, o_ref, lse_ref, s_s, p_s, *, band, nsub, sub_per_seq):
    i = pl.program_id(0)
    HW = DIL_HEADS * HEAD_DIM
    v_t = lambda ref, rows: ref[rows, 2 * HW:3 * HW].astype(F32).T.astype(BF16)
    vts = [v_t(qkv_ref, slice(u * band, (u + 1) * band)) for u in range(nsub)]
    qk_ref, qkh_ref = qkv_ref, halo_ref
    kj = lax.broadcasted_iota(I32, (band, band), 0)
    qi = lax.broadcasted_iota(I32, (band, band), 1)
    bias_cur = jnp.where(kj <= qi, 0.0, NEG_INF).astype(F32)
    bias_prev = jnp.where(kj >= qi, 0.0, NEG_INF).astype(F32)
    prevs = []
    for u in range(nsub):
        if sub_per_seq == 1 or (u % sub_per_seq == 0 and nsub % sub_per_seq == 0):
            prevs.append(None)
        elif u > 0:
            prevs.append((qk_ref[(u - 1) * band:u * band, HW:2 * HW], vts[u - 1], bias_prev))
        else:
            has = (i * nsub) % sub_per_seq != 0
            prevs.append((qkh_ref[:, HW:2 * HW], v_t(qkh_ref, slice(0, band)),
                          jnp.where(has, bias_prev, NEG_INF)))

    for u in range(nsub):
        rows = slice(u * band, (u + 1) * band)
        q = qk_ref[rows, 0:HW]
        k_cur = qk_ref[rows, HW:2 * HW]
        for h in range(DIL_HEADS):
            hs = slice(h * HEAD_DIM, (h + 1) * HEAD_DIM)
            s_s[u, h, 0] = _dot_t(k_cur[:, hs], q[:, hs])
            if prevs[u] is not None:
                s_s[u, h, 1] = _dot_t(prevs[u][0][:, hs], q[:, hs])
    stats = {}
    for u in range(nsub):
        for h in range(DIL_HEADS):
            s_c = s_s[u, h, 0] + bias_cur
            m = jnp.max(s_c, axis=0, keepdims=True)
            if prevs[u] is not None:
                s_p = s_s[u, h, 1] + prevs[u][2]
                m = jnp.maximum(m, jnp.max(s_p, axis=0, keepdims=True))
            p_c = jnp.exp(s_c - m)
            l = jnp.sum(p_c, axis=0, keepdims=True)
            p_s[u, h, 0] = p_c.astype(BF16)
            if prevs[u] is not None:
                p_p = jnp.exp(s_p - m)
                l = l + jnp.sum(p_p, axis=0, keepdims=True)
                p_s[u, h, 1] = p_p.astype(BF16)
            stats[u, h] = (m, l)
    pad = jnp.zeros((band - DIL_HEADS, band), F32)
    for u in range(nsub):
        rows = slice(u * band, (u + 1) * band)
        outs, lses = [], []
        for h in range(DIL_HEADS):
            hs = slice(h * HEAD_DIM, (h + 1) * HEAD_DIM)
            m, l = stats[u, h]
            o = _dot(vts[u][hs, :], p_s[u, h, 0])
            if prevs[u] is not None:
                o = o + _dot(prevs[u][1][hs, :], p_s[u, h, 1])
            outs.append(o / l)
            lses.append(m + jnp.log(l))
        o_ref[rows, :] = jnp.concatenate(outs, axis=0).T.astype(o_ref.dtype)
        lse_ref[rows, :] = jnp.concatenate(lses + [pad], axis=0).T


def _dil_lat_attn(qkv, seq, d, band, nsub=8):
    T = qkv.shape[0]
    HW = DIL_HEADS * HEAD_DIM
    M = seq // d
    assert band == LANES and M % band == 0
    sub_per_seq = M // band
    assert sub_per_seq % nsub == 0 or nsub % sub_per_seq == 0
    R = nsub * band
    halo = lambda i: jnp.maximum(i * nsub - 1, 0)
    kern = functools.partial(_dil_lat_kernel, band=band, nsub=nsub, sub_per_seq=sub_per_seq)
    return pl.pallas_call(
        kern, grid=(T // R,),
        in_specs=[pl.BlockSpec((R, 3 * HW), lambda i: (i, 0)),
                  pl.BlockSpec((band, 3 * HW), lambda i: (halo(i), 0))],
        out_specs=[pl.BlockSpec((R, HW), lambda i: (i, 0)), pl.BlockSpec((R, LANES), lambda i: (i, 0))],
        out_shape=[jax.ShapeDtypeStruct((T, HW), BF16), jax.ShapeDtypeStruct((T, LANES), F32)],
        scratch_shapes=[pltpu.VMEM((nsub, DIL_HEADS, 2, band, band), F32),
                        pltpu.VMEM((nsub, DIL_HEADS, 2, band, band), BF16)],
        compiler_params=_params("parallel"), name="dil_lat")(qkv, qkv)


def _dil_merge_kernel(*refs, dils, tmm):
    ng = len(dils)
    o_refs, l_refs = refs[0:2 * ng:2], refs[1:2 * ng:2]
    out_ref, scr = refs[2 * ng], refs[2 * ng + 1]
    nch = DIL_HEADS * HEAD_DIM // LANES
    o_pos, l_pos = [], []
    for g, d in enumerate(dils):
        if d == 1:
            o_pos.append([o_refs[g][:, c * LANES:(c + 1) * LANES].astype(F32) for c in range(nch)])
            l_pos.append(l_refs[g][...])
            continue
        for r in range(d):
            rows = pl.ds(r, tmm // d, stride=d)
            for c in range(nch):
                scr[g, c, rows, :] = o_refs[g][r, :, c * LANES:(c + 1) * LANES].astype(F32)
            scr[g, nch, rows, :] = l_refs[g][r]
        o_pos.append([scr[g, c] for c in range(nch)])
        l_pos.append(scr[g, nch])
    mx = functools.reduce(jnp.maximum, l_pos)
    es = [jnp.exp(l - mx) for l in l_pos]
    den = functools.reduce(lambda a, b: a + b, es)
    hpc = LANES // HEAD_DIM
    for c in range(nch):
        acc = None
        for g in range(ng):
            w = es[g] / den
            wc = jnp.concatenate([jnp.broadcast_to(w[:, c * hpc + k:c * hpc + k + 1], (tmm, HEAD_DIM))
                                  for k in range(hpc)], axis=1)
            term = wc * o_pos[g][c]
            acc = term if acc is None else acc + term
        out_ref[:, c * LANES:(c + 1) * LANES] = acc.astype(out_ref.dtype)


def _dil_merge(os_, ls_, batch, seq, tmm=256):
    dils = tuple(d for _, d in DIL_GROUPS)
    HW = DIL_HEADS * HEAD_DIM
    T = batch * seq
    tps = seq // tmm
    args, in_specs = [], []
    for o, l, d in zip(os_, ls_, dils):
        if d == 1:
            args += [o, l]
            in_specs += [pl.BlockSpec((tmm, HW), lambda i: (i, 0)), pl.BlockSpec((tmm, LANES), lambda i: (i, 0))]
        else:
            M = seq // d
            args += [o.reshape(batch, d, M, HW), l.reshape(batch, d, M, LANES)]
            in_specs += [pl.BlockSpec((None, d, tmm // d, HW), lambda i: (i // tps, 0, i % tps, 0)),
                         pl.BlockSpec((None, d, tmm // d, LANES), lambda i: (i // tps, 0, i % tps, 0))]
    kern = functools.partial(_dil_merge_kernel, dils=dils, tmm=tmm)
    return pl.pallas_call(
        kern, grid=(T // tmm,), in_specs=in_specs,
        out_specs=pl.BlockSpec((tmm, HW), lambda i: (i, 0)),
        out_shape=jax.ShapeDtypeStruct((T, HW), BF16),
        scratch_shapes=[pltpu.VMEM((len(dils), HW // LANES + 1, tmm, LANES), F32)],
        compiler_params=_params("parallel"), name="dil_merge")(*args)


def _dsa_kernel(q_ref, iq_ref, kv_ref, ikw_ref, o_ref, qall_s, iqall_s, kvt_s, key_s, bias_s, j_s, acc_s, p_s,
                *, tq, ck, top, seq):
    qi = pl.program_id(1)
    lo = qi * tq
    nch = (lo + tq + ck - 1) // ck
    krow = lax.broadcasted_iota(I32, (ck, tq), 0)
    qcol = lo + lax.broadcasted_iota(I32, (ck, tq), 1)

    def chunk(c):
        return pl.ds(pl.multiple_of(c * ck, ck), ck)

    @pl.when(qi == 0)
    def _():
        def body(c, _):
            kvt_s[:, chunk(c)] = kv_ref[chunk(c), :].astype(F32).T.astype(BF16)
            return 0
        lax.fori_loop(0, seq // ck, body, 0)

    for h in range(DSA_HEADS):
        qall_s[h * tq:(h + 1) * tq, :] = q_ref[:, h * HEAD_DIM:(h + 1) * HEAD_DIM]
    for h in range(IDX_HEADS):
        iqall_s[h * tq:(h + 1) * tq, :] = iq_ref[:, h * IDX_DIM:(h + 1) * IDX_DIM]

    @pl.when(lo + tq <= top)
    def _():
        def body(c, _):
            bias_s[chunk(c), :] = jnp.where(c * ck + krow <= qcol, 0.0, NEG_INF).astype(F32)
            return 0
        lax.fori_loop(0, nch, body, 0)

    @pl.when(lo + tq > top)
    def _():
        iw_t = ikw_ref[pl.ds(pl.multiple_of(lo, tq), tq), :].T[IDX_DIM:IDX_DIM + IDX_HEADS, :]
        iw_t = iw_t * (IDX_HEADS ** -0.5 * IDX_DIM ** -0.5)

        def score_chunk(c):
            ik = ikw_ref[chunk(c), 0:IDX_DIM].astype(BF16)
            r = _dot_t(ik, iqall_s[...])
            sc = jnp.zeros((ck, tq), F32)
            for h in range(IDX_HEADS):
                sc = sc + iw_t[h:h + 1, :] * jnp.maximum(r[:, h * tq:(h + 1) * tq], 0.0)
            sc = jnp.where(c * ck + krow <= qcol, sc, NEG_INF)
            bits = lax.bitcast_convert_type(sc, I32)
            key_s[chunk(c), :] = bits ^ ((bits >> 31) & 0x7FFFFFFF)

        def score_body(i, _):
            score_chunk(2 * i)
            score_chunk(jnp.minimum(2 * i + 1, nch - 1))
            return 0
        lax.fori_loop(0, (nch + 1) // 2, score_body, 0)

        def count(pred):
            def body(c, acc):
                ind = pred(key_s[chunk(c), :], c)
                return acc + jnp.sum(ind.reshape(ck // 64, 64, tq), axis=0)
            acc = lax.fori_loop(0, nch, body, jnp.zeros((64, tq), F32))
            return jnp.sum(acc, axis=0, keepdims=True)

        def count_ge(cand):
            return count(lambda kc, c: jnp.where(kc >= cand, 1.0, 0.0))

        topf = float(top)
        tau = jnp.where(count_ge(jnp.zeros((1, tq), I32)) >= topf, 0, INT_MIN).astype(I32)

        def search(i, tau):
            cand = tau | jnp.left_shift(jnp.int32(1), 30 - i)
            return jnp.where(count_ge(cand) >= topf, cand, tau)
        tau = lax.fori_loop(0, 31, search, tau)

        need = topf - count_ge(tau + 1)
        j_s[...] = jnp.full(j_s.shape, seq, I32)

        @pl.when(jnp.max(count_ge(tau)) > topf)
        def _():
            def count_eq_lt(J):
                return count(lambda kc, c: jnp.where(kc == tau, jnp.where(c * ck + krow < J, 1.0, 0.0), 0.0))

            nbits = seq.bit_length()

            def tie(i, J):
                cand = J + jnp.left_shift(jnp.int32(1), nbits - 1 - i)
                return jnp.where(count_eq_lt(cand) <= need, cand, J)
            J = lax.fori_loop(0, nbits, tie, jnp.zeros((1, tq), I32))
            j_s[...] = jnp.broadcast_to(J, j_s.shape)

        J = j_s[0:1, :]

        def bias_body(c, _):
            kc = key_s[chunk(c), :]
            sel = jnp.where(kc > tau, 0.0,
                            jnp.where(kc == tau, jnp.where(c * ck + krow < J, 0.0, NEG_INF), NEG_INF))
            bias_s[chunk(c), :] = sel.astype(F32)
            return 0
        lax.fori_loop(0, nch, bias_body, 0)

    acc_s[...] = jnp.zeros_like(acc_s)

    def step(c, carry, slot):
        ms, ls = carry
        kc = kv_ref[chunk(c), 0:HEAD_DIM]
        bias = bias_s[chunk(c), :]
        vt = kvt_s[HEAD_DIM:2 * HEAD_DIM, chunk(c)]
        new_ms, new_ls, alphas = [], [], []
        for h in range(DSA_HEADS):
            if h % 2 == 0:
                s_pair = _dot_t(kc, qall_s[h * tq:(h + 2) * tq, :])
            s = s_pair[:, (h % 2) * tq:(h % 2 + 1) * tq] + bias
            m_new = jnp.maximum(ms[h], jnp.max(s, axis=0, keepdims=True))
            alpha = jnp.exp(ms[h] - m_new)
            p = jnp.exp(s - m_new)
            new_ls.append(alpha * ls[h] + jnp.sum(p, axis=0, keepdims=True))
            new_ms.append(m_new)
            alphas.append(alpha)
            p_s[slot, :, h * tq:(h + 1) * tq] = p.astype(BF16)
        acc_s[...] = acc_s[...] * jnp.concatenate(alphas, axis=1) + _dot(vt, p_s[slot])
        return tuple(new_ms), tuple(new_ls)

    def pair(i, carry):
        return step(2 * i + 1, step(2 * i, carry, 0), 1)

    init = (tuple(jnp.full((1, tq), M_FLOOR, F32) for _ in range(DSA_HEADS)),
            tuple(jnp.zeros((1, tq), F32) for _ in range(DSA_HEADS)))
    carry = lax.fori_loop(0, nch // 2, pair, init)
    _, ls = lax.cond(nch % 2 == 1, lambda cr: step(nch - 1, cr, 0), lambda cr: cr, carry)
    o_all = acc_s[...] / jnp.concatenate(ls, axis=1)
    for h in range(0, DSA_HEADS, 2):
        pair = jnp.concatenate([o_all[:, h * tq:(h + 1) * tq], o_all[:, (h + 1) * tq:(h + 2) * tq]], axis=0)
        o_ref[:, h * HEAD_DIM:(h + 2) * HEAD_DIM] = pair.T.astype(o_ref.dtype)


def _dsa_attn(q, kv, iq, ikw, batch, seq, tq=LANES, ck=256):
    T = q.shape[0]
    nq = seq // tq
    top = min(TOPK_MAX, seq // 4)
    assert tq == LANES and top % tq == 0 and seq % ck == 0
    kern = functools.partial(_dsa_kernel, tq=tq, ck=ck, top=top, seq=seq)
    W = DSA_HEADS * HEAD_DIM
    return pl.pallas_call(
        kern, grid=(batch, nq),
        in_specs=[pl.BlockSpec((tq, W), lambda b, i: (b * nq + i, 0)),
                  pl.BlockSpec((tq, IDX_HEADS * IDX_DIM), lambda b, i: (b * nq + i, 0)),
                  pl.BlockSpec((seq, LANES), lambda b, i: (b, 0)),
                  pl.BlockSpec((seq, LANES), lambda b, i: (b, 0))],
        out_specs=pl.BlockSpec((tq, W), lambda b, i: (b * nq + i, 0)),
        out_shape=jax.ShapeDtypeStruct((T, W), BF16),
        scratch_shapes=[pltpu.VMEM((DSA_HEADS * tq, HEAD_DIM), BF16),
                        pltpu.VMEM((IDX_HEADS * tq, IDX_DIM), BF16),
                        pltpu.VMEM((LANES, seq), BF16),
                        pltpu.VMEM((seq, tq), I32),
                        pltpu.VMEM((seq, tq), F32),
                        pltpu.VMEM((8, tq), I32),
                        pltpu.VMEM((HEAD_DIM, DSA_HEADS * tq), F32),
                        pltpu.VMEM((2, ck, DSA_HEADS * tq), BF16)],
        compiler_params=_params("arbitrary", "arbitrary"), name="dsa_attn")(q, iq, kv, ikw)


def _mix_out_kernel(x_ref, oa_ref, ob_ref, oc_ref, wg_ref, bg_ref, wa_ref, wb_ref, wc_ref, wo_ref,
                    lg_ref, lb_ref, o_ref, *, alpha, d_model):
    x = x_ref[...]
    xb = x.astype(BF16)
    merged = None
    for br, (o_r, w_r) in enumerate(((oa_ref, wa_ref), (ob_ref, wb_ref), (oc_ref, wc_ref))):
        sl = slice(br * d_model, (br + 1) * d_model)
        gate = jax.nn.sigmoid(_dot(xb, wg_ref[:, sl]) + bg_ref[:, sl])
        term = gate * _dot(o_r[...], w_r[...])
        merged = term if merged is None else merged + term
    mix = _dot(merged.astype(BF16), wo_ref[...])
    o_ref[...] = _layer_norm(alpha * x + mix, lg_ref[...], lb_ref[...])


def _mix_out(x, oa, ob, oc, wg, bg, wa, wb, wc, wo, lg, lb, alpha, tm=256):
    T, D = x.shape
    const = lambda i: (0, 0)
    full = lambda a: pl.BlockSpec(a.shape, const)
    row = lambda a: pl.BlockSpec((tm, a.shape[1]), lambda i: (i, 0))
    kern = functools.partial(_mix_out_kernel, alpha=alpha, d_model=D)
    return pl.pallas_call(
        kern, grid=(T // tm,),
        in_specs=[row(x), row(oa), row(ob), row(oc), full(wg), full(bg), full(wa), full(wb), full(wc),
                  full(wo), full(lg), full(lb)],
        out_specs=pl.BlockSpec((tm, D), lambda i: (i, 0)),
        out_shape=jax.ShapeDtypeStruct((T, D), F32),
        compiler_params=_params("parallel"), name="mix_out")(x, oa, ob, oc, wg, bg, wa, wb, wc, wo, lg, lb)


def _route(x, wr_hi_ref, wr_lo_ref, br_ref):
    xh = x.astype(BF16)
    xl = (x - xh.astype(F32)).astype(BF16)
    logits = _dot(xh, wr_hi_ref[...]) + _dot(xl, wr_hi_ref[...]) + _dot(xh, wr_lo_ref[...]) + br_ref[...]
    lane = lax.broadcasted_iota(I32, logits.shape, 1).astype(F32)
    none = float(LANES)
    glog = jnp.where(lane < N_GROUPS, logits, NEG_INF)
    gmax = jnp.max(glog, axis=1, keepdims=True)
    g_p = 1.0 / jnp.sum(jnp.exp(glog - gmax), axis=1, keepdims=True)
    g_idx = jnp.min(jnp.where(glog == gmax, lane, none), axis=1, keepdims=True)
    first = N_GROUPS + g_idx * EXPERTS_PER_GROUP
    sub = jnp.where(lane >= first, jnp.where(lane < first + EXPERTS_PER_GROUP, logits, NEG_INF), NEG_INF)
    v1 = jnp.max(sub, axis=1, keepdims=True)
    i1 = jnp.min(jnp.where(sub == v1, lane, none), axis=1, keepdims=True)
    sub2 = jnp.where(lane == i1, NEG_INF, sub)
    v2 = jnp.max(sub2, axis=1, keepdims=True)
    i2 = jnp.min(jnp.where(sub2 == v2, lane, none), axis=1, keepdims=True)
    e2 = jnp.exp(v2 - v1)
    w1 = g_p / (1.0 + e2)
    w2 = g_p * e2 / (1.0 + e2)
    return jnp.where(lane == i1, w1, jnp.where(lane == i2, w2, 0.0)), g_idx


def _split3(v):
    hi = v.astype(BF16)
    r = v - hi.astype(F32)
    mid = r.astype(BF16)
    return hi, mid, (r - mid.astype(F32)).astype(BF16)


def _moe_kernel(x_ref, wrh_ref, wrl_ref, br_ref, w1_ref, w3_ref, w2_ref, lg_ref, lb_ref, o_ref,
                xs_s, combs_s, pt_s, acc_s, rng_s, *, alpha, tm, rb, ts, eps):
    e = pl.program_id(1)

    @pl.when(e == 0)
    def _():
        x = x_ref[...]
        comb, g_idx = _route(x, wrh_ref, wrl_ref, br_ref)
        lane = lax.broadcasted_iota(I32, (tm, LANES), 1).astype(F32)
        onehot_g = jnp.where(lane == g_idx, 1.0, 0.0)
        ti = lax.broadcasted_iota(I32, (tm, tm), 0)
        tj = lax.broadcasted_iota(I32, (tm, tm), 1)
        earlier = jnp.where(tj < ti, 1.0, 0.0).astype(BF16)
        rank = _dot(earlier, onehot_g.astype(BF16))
        counts = jnp.sum(onehot_g, axis=0, keepdims=True)
        padded = jnp.ceil(counts * (1.0 / ROW_ALIGN)) * ROW_ALIGN
        lane1 = lax.broadcasted_iota(I32, (1, LANES), 1)
        off = jnp.zeros((1, LANES), F32)
        for k in range(1, N_GROUPS):
            off = off + jnp.where(lane1 >= k, pltpu.roll(padded, k, 1), 0.0)
        pos = jnp.sum(onehot_g * (off + rank), axis=1, keepdims=True)
        pos_row = jnp.broadcast_to(pos, (tm, LANES)).T[0:1, :]
        sj = lax.broadcasted_iota(I32, (tm, ts), 1)
        si = lax.broadcasted_iota(I32, (ts, tm), 0)
        pt_s[...] = jnp.where(sj.astype(F32) == pos, 1.0, 0.0).astype(BF16)
        perm = jnp.where(si.astype(F32) == pos_row, 1.0, 0.0).astype(BF16)
        xs_s[...] = _dot(perm, x.astype(BF16)).astype(BF16)
        c_hi, c_mid, c_lo = _split3(comb)
        combs_s[...] = _dot(perm, c_hi) + _dot(perm, c_mid) + _dot(perm, c_lo)
        acc_s[...] = jnp.zeros_like(acc_s)
        for g in range(N_GROUPS):
            start = jnp.sum(jnp.where(lane1 == g, off, 0.0)).astype(I32)
            cnt = jnp.sum(jnp.where(lane1 == g, counts, 0.0)).astype(I32)
            rng_s[g] = start
            rng_s[N_GROUPS + g] = (cnt + rb - 1) // rb

    g = (e * eps) // EXPERTS_PER_GROUP
    lane_b = lax.broadcasted_iota(I32, (rb, LANES), 1)
    start = rng_s[g]

    def block(b, _):
        rows = pl.ds(pl.multiple_of(start + b * rb, ROW_ALIGN), rb)
        xb = xs_s[rows, :]
        comb = combs_s[rows, :]
        y = None
        for k in range(eps):
            we = jnp.sum(jnp.where(lane_b == N_GROUPS + e * eps + k, comb, 0.0), axis=1, keepdims=True)
            a = _dot(xb, w1_ref[k])
            b3 = _dot(xb, w3_ref[k])
            hid = (a * jax.nn.sigmoid(a) * b3).astype(BF16)
            yk = we * _dot(hid, w2_ref[k])
            y = yk if y is None else y + yk
        acc_s[rows, :] += y
        return 0
    lax.fori_loop(0, rng_s[N_GROUPS + g], block, 0)

    @pl.when(e == N_EXPERTS // eps - 1)
    def _():
        y = _dot(pt_s[...], acc_s[...].astype(BF16))
        o_ref[...] = _layer_norm(alpha * x_ref[...] + y, lg_ref[...], lb_ref[...])


def _moe(x, wrh, wrl, br, w1, w3, w2, lg, lb, alpha, tm=1024, rb=320, eps=4):
    T, D = x.shape
    assert EXPERTS_PER_GROUP % eps == 0
    const = lambda i, e: (0, 0)
    full = lambda a: pl.BlockSpec(a.shape, const)
    ts = -(-(tm + N_GROUPS * ROW_ALIGN + rb) // LANES) * LANES
    kern = functools.partial(_moe_kernel, alpha=alpha, tm=tm, rb=rb, ts=ts, eps=eps)
    return pl.pallas_call(
        kern, grid=(T // tm, N_EXPERTS // eps),
        in_specs=[pl.BlockSpec((tm, D), lambda i, e: (i, 0)), full(wrh), full(wrl), full(br),
                  pl.BlockSpec((eps, D, D_EXPERT), lambda i, e: (e, 0, 0)),
                  pl.BlockSpec((eps, D, D_EXPERT), lambda i, e: (e, 0, 0)),
                  pl.BlockSpec((eps, D_EXPERT, D), lambda i, e: (e, 0, 0)),
                  full(lg), full(lb)],
        out_specs=pl.BlockSpec((tm, D), lambda i, e: (i, 0)),
        out_shape=jax.ShapeDtypeStruct((T, D), F32),
        scratch_shapes=[pltpu.VMEM((ts, D), BF16),
                        pltpu.VMEM((ts, LANES), F32),
                        pltpu.VMEM((tm, ts), BF16),
                        pltpu.VMEM((ts, D), F32),
                        pltpu.SMEM((2 * N_GROUPS,), I32)],
        compiler_params=_params("parallel", "arbitrary"), name="moe")(x, wrh, wrl, br, w1, w3, w2, lg, lb)


def _rope_tables(seq):
    pos = jnp.arange(seq, dtype=F32)[:, None]
    one = lambda n: jnp.ones((seq, n), F32)
    zero = lambda n: jnp.zeros((seq, n), F32)

    inv_p = ROPE_THETA ** (-jnp.arange(0, ROT_DIM, 2, dtype=F32) / ROT_DIM)
    cp, sp = jnp.cos(pos * inv_p), jnp.sin(pos * inv_p)
    hp = ROT_DIM // 2
    rest = HEAD_DIM - ROT_DIM
    head = (jnp.concatenate([cp, cp, one(rest)], 1),
            jnp.concatenate([zero(hp), sp, zero(rest)], 1),
            jnp.concatenate([-sp, zero(hp), zero(rest)], 1))
    ident = (one(HEAD_DIM), zero(HEAD_DIM), zero(HEAD_DIM))
    both = jnp.stack([jnp.concatenate([a, a], 1) for a in head])
    first = jnp.stack([jnp.concatenate([a, b], 1) for a, b in zip(head, ident)])
    tab_p = jnp.stack([both, first])

    inv_m = ROPE_THETA ** (-jnp.arange(0, MLA_ROPE, 2, dtype=F32) / MLA_ROPE)
    cm, sm = jnp.cos(pos * inv_m), jnp.sin(pos * inv_m)
    hm = MLA_ROPE // 2
    pad = LANES - MLA_NOPE - MLA_ROPE
    tab_m = jnp.stack([jnp.concatenate([one(MLA_NOPE), cm, cm, one(pad)], 1),
                       jnp.concatenate([zero(MLA_NOPE), zero(hm), sm, zero(pad)], 1),
                       jnp.concatenate([zero(MLA_NOPE), -sm, zero(hm), zero(pad)], 1)])[None]
    return tab_p, tab_m


def _layer_weights(p, l):
    w_in = p["w_in"][l]
    D = w_in.shape[0]
    z = lambda n: jnp.zeros((D, n), F32)
    o = 0
    cuts = []
    for n in (MLA_Q_RANK, MLA_KV_RANK, MLA_ROPE, 3 * 3 * DIL_HEADS * HEAD_DIM, DSA_HEADS * HEAD_DIM,
              HEAD_DIM, HEAD_DIM, IDX_HEADS * IDX_DIM, IDX_DIM, IDX_HEADS):
        cuts.append(w_in[:, o:o + n])
        o += n
    w_cq, w_ckv, w_kr, w_dil, w_q, w_k, w_v, w_iq, w_ik, w_iw = cuts
    qs = HEAD_DIM ** -0.5
    w_mla = jnp.concatenate([w_cq, w_ckv, z(MLA_NOPE), w_kr, z(LANES - MLA_NOPE - MLA_ROPE)], 1)
    HW = DIL_HEADS * HEAD_DIM
    w_dil = w_dil.reshape(D, len(DIL_GROUPS), 3, HW)
    w_dqkv = jnp.concatenate([w_dil[:, :, 0] * qs, w_dil[:, :, 1], w_dil[:, :, 2]], axis=-1)
    w_dqkv = w_dqkv.transpose(1, 0, 2)
    w_dsa = jnp.concatenate([w_q * qs, w_k, w_v, w_iq, w_ik, w_iw, z(LANES - IDX_DIM - IDX_HEADS)], 1)

    def per_head(w, n_in, n_keep_lo, n_keep_hi):
        r = w.shape[0]
        w = w.reshape(r, MLA_HEADS, n_in)[:, :, n_keep_lo:n_keep_hi]
        w = jnp.pad(w, ((0, 0), (0, 0), (0, LANES - (n_keep_hi - n_keep_lo))))
        return w.reshape(r, MLA_HEADS * LANES)

    w_uq = per_head(p["w_uq"][l], MLA_NOPE + MLA_ROPE, 0, MLA_NOPE + MLA_ROPE)
    w_uq3 = w_uq.reshape(MLA_Q_RANK, MLA_HEADS, LANES)
    w_uq = jnp.concatenate([jnp.roll(w_uq3, sh, axis=2).reshape(MLA_Q_RANK, -1)
                            for sh in (0, MLA_ROPE // 2, -(MLA_ROPE // 2))], axis=1)
    w_uk = per_head(p["w_ukv"][l], MLA_NOPE + MLA_V, 0, MLA_NOPE)
    w_uv = p["w_ukv"][l].reshape(MLA_KV_RANK, MLA_HEADS, MLA_NOPE + MLA_V)[:, :, MLA_NOPE:]
    w_uv = w_uv.reshape(MLA_KV_RANK, MLA_HEADS * MLA_V).T

    w_r = jnp.concatenate([p["w_group"][l], p["w_sub"][l], z(LANES - N_GROUPS - N_EXPERTS)], 1)
    w_r_hi = w_r.astype(BF16)
    w_r_lo = (w_r - w_r_hi.astype(F32)).astype(BF16)
    b_r = jnp.concatenate([p["b_group"][l], p["b_sub"][l], jnp.zeros((LANES - N_GROUPS - N_EXPERTS,), F32)])

    bf = lambda a: a.astype(BF16)
    return dict(
        w_proj=bf(jnp.concatenate([w_mla, w_dsa] + [w_dqkv[g] for g in range(len(DIL_GROUPS))], axis=1)),
        n_mla=w_mla.shape[1],
        q_g=p["q_norm_g"][l][None], kv_g=p["kv_norm_g"][l][None],
        w_uq=bf(w_uq), w_uk=bf(w_uk), w_uv=bf(w_uv),
        w_gate=bf(p["w_gate"][l]), b_gate=p["b_gate"][l][None],
        w_a=bf(p["w_a"][l]), w_b=bf(p["w_b"][l]), w_c=bf(p["w_c"][l]), w_o=bf(p["w_o"][l]),
        ln1_g=p["ln1_g"][l][None], ln1_b=p["ln1_b"][l][None],
        w_r_hi=w_r_hi, w_r_lo=w_r_lo, b_r=b_r[None],
        w1=bf(p["w1"][l]), w3=bf(p["w3"][l]), w2=bf(p["w2"][l]),
        ln2_g=p["ln2_g"][l][None], ln2_b=p["ln2_b"][l][None])


def _project(xt, w, tab_p, batch, seq):
    HW = DIL_HEADS * HEAD_DIM
    qw, iw_ = DSA_HEADS * HEAD_DIM, IDX_HEADS * IDX_DIM
    widths = [w["n_mla"], qw, LANES, iw_, LANES] + [3 * HW] * len(DIL_GROUPS)
    starts = [sum(widths[:k]) for k in range(len(widths))]
    rope2, rope1, none = 0, 1, -1
    kinds = ((none,) * (w["n_mla"] // LANES)
             + (rope2,) * (qw // LANES) + (rope1,) + (rope2,) * (iw_ // LANES) + (rope1,)
             + ((rope2,) * (2 * HW // LANES) + (none,) * (HW // LANES)) * len(DIL_GROUPS))
    return _proj(xt, w["w_proj"], tab_p, batch, seq, segs=tuple(zip(starts, widths)),
                 dtypes=(F32, BF16, BF16, BF16, F32) + (BF16,) * len(DIL_GROUPS), kinds=kinds,
                 dils=(1,) * 5 + tuple(d for _, d in DIL_GROUPS))


def _layer(xt, w, tab_p, tab_m, batch, seq, alpha):
    grp, q_c, kv_c, iq_c, ikw_c, *dil_qkv = _project(xt, w, tab_p, batch, seq)
    dil_o, dil_lse = [], []
    for qkv_g, (window, d) in zip(dil_qkv, DIL_GROUPS):
        o_g, lse_g = _dil_lat_attn(qkv_g, seq, d, band=window // d)
        dil_o.append(o_g)
        dil_lse.append(lse_g)

    q_a, k_a, v_a = _mla_up(grp, w["q_g"], w["kv_g"], w["w_uq"], w["w_uk"], w["w_uv"], tab_m, seq)
    o_a = _mla_attn(q_a, k_a, v_a, batch, seq)
    o_b = _dil_merge(dil_o, dil_lse, batch, seq)
    o_c = _dsa_attn(q_c, kv_c, iq_c, ikw_c, batch, seq)

    x1 = _mix_out(xt, o_a, o_b, o_c, w["w_gate"], w["b_gate"], w["w_a"], w["w_b"], w["w_c"], w["w_o"],
                  w["ln1_g"], w["ln1_b"], alpha)
    return _moe(x1, w["w_r_hi"], w["w_r_lo"], w["b_r"], w["w1"], w["w3"], w["w2"],
                w["ln2_g"], w["ln2_b"], alpha)


def kernel(x, w_in, q_norm_g, w_uq, kv_norm_g, w_ukv, w_gate, b_gate, w_a, w_b, w_c, w_o, ln1_g, ln1_b,
           w_group, b_group, w_sub, b_sub, w1, w3, w2, ln2_g, ln2_b):
    batch, seq, d_model = x.shape
    depth = w_in.shape[0]
    alpha = (2 * depth) ** 0.25
    p = dict(w_in=w_in, q_norm_g=q_norm_g, w_uq=w_uq, kv_norm_g=kv_norm_g, w_ukv=w_ukv, w_gate=w_gate,
             b_gate=b_gate, w_a=w_a, w_b=w_b, w_c=w_c, w_o=w_o, ln1_g=ln1_g, ln1_b=ln1_b, w_group=w_group,
             b_group=b_group, w_sub=w_sub, b_sub=b_sub, w1=w1, w3=w3, w2=w2, ln2_g=ln2_g, ln2_b=ln2_b)
    tab_p, tab_m = _rope_tables(seq)
    xt = x.reshape(batch * seq, d_model)
    for l in range(depth):
        xt = _layer(xt, _layer_weights(p, l), tab_p, tab_m, batch, seq, alpha)
    return xt.reshape(batch, seq, d_model)
```

```python
import functools

import jax
import jax.numpy as jnp
from jax import lax
from jax.experimental import pallas as pl
from jax.experimental.pallas import tpu as pltpu

F32 = jnp.float32
BF16 = jnp.bfloat16
I32 = jnp.int32

HEAD_DIM = 64
ROT_DIM = HEAD_DIM // 4
ROPE_THETA = 500000.0
EPS = 1e-6
MLA_HEADS = 8
MLA_Q_RANK = 256
MLA_KV_RANK = 128
MLA_NOPE = 64
MLA_ROPE = 32
MLA_V = 64
DIL_GROUPS = ((128, 1), (512, 4), (2048, 16))
DIL_HEADS = 4
DSA_HEADS = 8
IDX_HEADS = 8
IDX_DIM = 64
TOPK_MAX = 256
N_GROUPS = 4
EXPERTS_PER_GROUP = 8
N_EXPERTS = N_GROUPS * EXPERTS_PER_GROUP
D_EXPERT = 256
N_BRANCH = 3

LANES = 128
VMEM_LIMIT = 56 * 1024 * 1024
NEG_INF = float("-inf")
INT_MIN = -2 ** 31
M_FLOOR = -1e30
LOG2E = 1.4426950408889634
ROW_ALIGN = 16


def _params(*sem):
    return pltpu.CompilerParams(dimension_semantics=sem, vmem_limit_bytes=VMEM_LIMIT)


def _dot(a, b):
    return jnp.dot(a, b, preferred_element_type=F32)


def _dot_t(a, b):
    return lax.dot_general(a, b, (((1,), (1,)), ((), ())), preferred_element_type=F32)


def _rope_chunk(v, tab_ref, kind, shift):
    c = tab_ref[kind, 0]
    s1 = tab_ref[kind, 1]
    s2 = tab_ref[kind, 2]
    return v * c + pltpu.roll(v, shift, 1) * s1 + pltpu.roll(v, LANES - shift, 1) * s2


def _layer_norm(z, g, b):
    mu = jnp.mean(z, axis=-1, keepdims=True)
    zc = z - mu
    var = jnp.mean(zc * zc, axis=-1, keepdims=True)
    return zc * lax.rsqrt(var + EPS) * g + b


def _proj_kernel(*refs, segs, kinds, dils, shift):
    x_ref, w_ref, tab_ref = refs[:3]
    outs, scr = refs[3:3 + len(segs)], refs[3 + len(segs)]
    xb = x_ref[...].astype(BF16)
    tm = xb.shape[0]
    for (c0, width), d, o_ref in zip(segs, dils, outs):
        acc = _dot(xb, w_ref[:, c0:c0 + width])
        nch = width // LANES
        for c in range(nch):
            sl = slice(c * LANES, (c + 1) * LANES)
            v = acc[:, sl]
            kind = kinds[c0 // LANES + c]
            if kind >= 0:
                v = _rope_chunk(v, tab_ref, kind, shift)
            if d == 1:
                o_ref[:, sl] = v.astype(o_ref.dtype)
            else:
                scr[c] = v
        if d > 1:
            for r in range(d):
                for c in range(nch):
                    o_ref[r, :, c * LANES:(c + 1) * LANES] = (
                        scr[c, pl.ds(r, tm // d, stride=d), :].astype(o_ref.dtype))


def _proj(x, w, tab, batch, seq, *, segs, dtypes, kinds, dils, shift=ROT_DIM // 2, tm=512):
    T, K = x.shape
    spt = seq // tm
    out_specs, out_shape = [], []
    for (_, wd), dt, d in zip(segs, dtypes, dils):
        if d == 1:
            out_specs.append(pl.BlockSpec((tm, wd), lambda i: (i, 0)))
            out_shape.append(jax.ShapeDtypeStruct((T, wd), dt))
        else:
            out_specs.append(pl.BlockSpec((None, d, tm // d, wd), lambda i: (i // spt, 0, i % spt, 0)))
            out_shape.append(jax.ShapeDtypeStruct((batch, d, seq // d, wd), dt))
    kern = functools.partial(_proj_kernel, segs=segs, kinds=kinds, dils=dils, shift=shift)
    outs = pl.pallas_call(
        kern, grid=(T // tm,),
        in_specs=[pl.BlockSpec((tm, K), lambda i: (i, 0)), pl.BlockSpec(w.shape, lambda i: (0, 0)),
                  pl.BlockSpec((tab.shape[0], 3, tm, LANES), lambda i: (0, 0, i % spt, 0))],
        out_specs=out_specs, out_shape=out_shape,
        scratch_shapes=[pltpu.VMEM((max(wd for _, wd in segs) // LANES, tm, LANES), F32)],
        compiler_params=_params("parallel"), name="in_proj")(x, w, tab)
    return [o.reshape(T, o.shape[-1]) for o in outs]


def _mla_up_kernel(g_ref, qg_ref, kvg_ref, wq_ref, wk_ref, wv_ref, tab_ref, q_out, k_out, v_out):
    g = g_ref[...]
    cq = g[:, :MLA_Q_RANK]
    ckv = g[:, MLA_Q_RANK:MLA_Q_RANK + MLA_KV_RANK]
    kr = g[:, MLA_Q_RANK + MLA_KV_RANK:]
    cqn = (cq * lax.rsqrt(jnp.mean(cq * cq, axis=-1, keepdims=True) + EPS) * qg_ref[...]).astype(BF16)
    ckvn = (ckv * lax.rsqrt(jnp.mean(ckv * ckv, axis=-1, keepdims=True) + EPS) * kvg_ref[...]).astype(BF16)
    q3 = _dot(cqn, wq_ref[...])
    k = _dot(ckvn, wk_ref[...])
    v_out[...] = _dot_t(wv_ref[...], ckvn).astype(v_out.dtype)
    W = MLA_HEADS * LANES
    c, s1, s2 = tab_ref[0, 0], tab_ref[0, 1], tab_ref[0, 2]
    kr_rot = _rope_chunk(kr, tab_ref, 0, MLA_ROPE // 2)
    for h in range(MLA_HEADS):
        sl = slice(h * LANES, (h + 1) * LANES)
        q_rot = (q3[:, sl] * c + q3[:, W + h * LANES:W + (h + 1) * LANES] * s1
                 + q3[:, 2 * W + h * LANES:2 * W + (h + 1) * LANES] * s2)
        q_out[:, sl] = q_rot.astype(q_out.dtype)
        k_out[:, sl] = (k[:, sl] + kr_rot).astype(k_out.dtype)


def _mla_up(grp, qg, kvg, wq, wk, wv, tab, seq, tm=512):
    T = grp.shape[0]
    spt = seq // tm
    const = lambda i: (0, 0)
    return pl.pallas_call(
        _mla_up_kernel, grid=(T // tm,),
        in_specs=[pl.BlockSpec((tm, grp.shape[1]), lambda i: (i, 0)),
                  pl.BlockSpec(qg.shape, const), pl.BlockSpec(kvg.shape, const),
                  pl.BlockSpec(wq.shape, const), pl.BlockSpec(wk.shape, const), pl.BlockSpec(wv.shape, const),
                  pl.BlockSpec((1, 3, tm, LANES), lambda i: (0, 0, i % spt, 0))],
        out_specs=[pl.BlockSpec((tm, MLA_HEADS * LANES), lambda i: (i, 0)),
                   pl.BlockSpec((tm, MLA_HEADS * LANES), lambda i: (i, 0)),
                   pl.BlockSpec((MLA_HEADS * MLA_V, tm), lambda i: (0, i))],
        out_shape=[jax.ShapeDtypeStruct((T, MLA_HEADS * LANES), BF16),
                   jax.ShapeDtypeStruct((T, MLA_HEADS * LANES), BF16),
                   jax.ShapeDtypeStruct((MLA_HEADS * MLA_V, T), BF16)],
        compiler_params=_params("parallel"), name="mla_up")(grp, qg, kvg, wq, wk, wv, tab)


def _mla_attn_kernel(q_ref, k_ref, vt_ref, o_ref, acc_s, s_s, p_s, *, tq, c_exp):
    qi = pl.program_id(1)
    krow = lax.broadcasted_iota(I32, (tq, tq), 0)
    qcol = lax.broadcasted_iota(I32, (tq, tq), 1)
    diag_bias = jnp.where(krow <= qcol, 0.0, NEG_INF).astype(F32)
    acc_s[...] = jnp.zeros_like(acc_s)

    def step(j, carry, masked, slot):
        ms, ls = carry
        ks = pl.multiple_of(j * tq, tq)
        for h in range(MLA_HEADS):
            kb = k_ref[pl.ds(ks, tq), h * LANES:(h + 1) * LANES]
            s_s[slot, h] = _dot_t(kb, q_ref[:, h * LANES:(h + 1) * LANES])
        new_ms, new_ls, alphas = [], [], []
        for h in range(MLA_HEADS):
            s = s_s[slot, h]
            if masked:
                s = s + diag_bias
            m_new = jnp.maximum(ms[h], jnp.max(s, axis=0, keepdims=True))
            alpha = jnp.exp2((ms[h] - m_new) * c_exp)
            p = jnp.exp2((s - m_new) * c_exp)
            new_ls.append(alpha * ls[h] + jnp.sum(p, axis=0, keepdims=True))
            new_ms.append(m_new)
            alphas.append(alpha)
            p_s[slot, h] = p.astype(BF16)
        for h in range(MLA_HEADS):
            sl = slice(h * MLA_V, (h + 1) * MLA_V)
            acc_s[sl, :] = alphas[h] * acc_s[sl, :] + _dot(vt_ref[sl, pl.ds(ks, tq)], p_s[slot, h])
        return tuple(new_ms), tuple(new_ls)

    init = (tuple(jnp.full((1, tq), NEG_INF, F32) for _ in range(MLA_HEADS)),
            tuple(jnp.zeros((1, tq), F32) for _ in range(MLA_HEADS)))
    def pair(i, carry):
        return step(2 * i + 1, step(2 * i, carry, False, 0), False, 1)

    carry = lax.fori_loop(0, qi // 2, pair, init)
    carry = lax.cond(qi % 2 == 1, lambda cr: step(qi - 1, cr, False, 0), lambda cr: cr, carry)
    _, ls = step(qi, carry, True, 1)
    for h in range(MLA_HEADS):
        sl = slice(h * MLA_V, (h + 1) * MLA_V)
        acc_s[sl, :] = acc_s[sl, :] / ls[h]
    o_ref[...] = acc_s[...].T.astype(o_ref.dtype)


def _mla_attn(q, k, vt, batch, seq, tq=256):
    T = q.shape[0]
    nq = seq // tq
    c_exp = (MLA_NOPE + MLA_ROPE) ** -0.5 * LOG2E
    kern = functools.partial(_mla_attn_kernel, tq=tq, c_exp=c_exp)
    W = MLA_HEADS * LANES
    return pl.pallas_call(
        kern, grid=(batch, nq),
        in_specs=[pl.BlockSpec((tq, W), lambda b, i: (b * nq + i, 0)),
                  pl.BlockSpec((seq, W), lambda b, i: (b, 0)),
                  pl.BlockSpec((MLA_HEADS * MLA_V, seq), lambda b, i: (0, b))],
        out_specs=pl.BlockSpec((tq, MLA_HEADS * MLA_V), lambda b, i: (b * nq + i, 0)),
        out_shape=jax.ShapeDtypeStruct((T, MLA_HEADS * MLA_V), BF16),
        scratch_shapes=[pltpu.VMEM((MLA_HEADS * MLA_V, tq), F32),
                        pltpu.VMEM((2, MLA_HEADS, tq, tq), F32), pltpu.VMEM((2, MLA_HEADS, tq, tq), BF16)],
        compiler_params=_params("parallel", "arbitrary"), name="mla_attn")(q, k, vt)


def _dil_lat_kernel(qkv_ref, halo_ref, o_ref, lse_ref, s_s, p_s, *, band, nsub, sub_per_seq):
    i = pl.program_id(0)
    HW = DIL_HEADS * HEAD_DIM
    v_t = lambda ref, rows: ref[rows, 2 * HW:3 * HW].astype(F32).T.astype(BF16)
    vts = [v_t(qkv_ref, slice(u * band, (u + 1) * band)) for u in range(nsub)]
    qk_ref, qkh_ref = qkv_ref, halo_ref
    kj = lax.broadcasted_iota(I32, (band, band), 0)
    qi = lax.broadcasted_iota(I32, (band, band), 1)
    bias_cur = jnp.where(kj <= qi, 0.0, NEG_INF).astype(F32)
    bias_prev = jnp.where(kj >= qi, 0.0, NEG_INF).astype(F32)
    prevs = []
    for u in range(nsub):
        if sub_per_seq == 1 or (u % sub_per_seq == 0 and nsub % sub_per_seq == 0):
            prevs.append(None)
        elif u > 0:
            prevs.append((qk_ref[(u - 1) * band:u * band, HW:2 * HW], vts[u - 1], bias_prev))
        else:
            has = (i * nsub) % sub_per_seq != 0
            prevs.append((qkh_ref[:, HW:2 * HW], v_t(qkh_ref, slice(0, band)),
                          jnp.where(has, bias_prev, NEG_INF)))

    for u in range(nsub):
        rows = slice(u * band, (u + 1) * band)
        q = qk_ref[rows, 0:HW]
        k_cur = qk_ref[rows, HW:2 * HW]
        for h in range(DIL_HEADS):
            hs = slice(h * HEAD_DIM, (h + 1) * HEAD_DIM)
            s_s[u, h, 0] = _dot_t(k_cur[:, hs], q[:, hs])
            if prevs[u] is not None:
                s_s[u, h, 1] = _dot_t(prevs[u][0][:, hs], q[:, hs])
    stats = {}
    for u in range(nsub):
        for h in range(DIL_HEADS):
            s_c = s_s[u, h, 0] + bias_cur
            m = jnp.max(s_c, axis=0, keepdims=True)
            if prevs[u] is not None:
                s_p = s_s[u, h, 1] + prevs[u][2]
                m = jnp.maximum(m, jnp.max(s_p, axis=0, keepdims=True))
            p_c = jnp.exp(s_c - m)
            l = jnp.sum(p_c, axis=0, keepdims=True)
            p_s[u, h, 0] = p_c.astype(BF16)
            if prevs[u] is not None:
                p_p = jnp.exp(s_p - m)
                l = l + jnp.sum(p_p, axis=0, keepdims=True)
                p_s[u, h, 1] = p_p.astype(BF16)
            stats[u, h] = (m, l)
    pad = jnp.zeros((band - DIL_HEADS, band), F32)
    for u in range(nsub):
        rows = slice(u * band, (u + 1) * band)
        outs, lses = [], []
        for h in range(DIL_HEADS):
            hs = slice(h * HEAD_DIM, (h + 1) * HEAD_DIM)
            m, l = stats[u, h]
            o = _dot(vts[u][hs, :], p_s[u, h, 0])
            if prevs[u] is not None:
                o = o + _dot(prevs[u][1][hs, :], p_s[u, h, 1])
            outs.append(o / l)
            lses.append(m + jnp.log(l))
        o_ref[rows, :] = jnp.concatenate(outs, axis=0).T.astype(o_ref.dtype)
        lse_ref[rows, :] = jnp.concatenate(lses + [pad], axis=0).T


def _dil_lat_attn(qkv, seq, d, band, nsub=8):
    T = qkv.shape[0]
    HW = DIL_HEADS * HEAD_DIM
    M = seq // d
    assert band == LANES and M % band == 0
    sub_per_seq = M // band
    assert sub_per_seq % nsub == 0 or nsub % sub_per_seq == 0
    R = nsub * band
    halo = lambda i: jnp.maximum(i * nsub - 1, 0)
    kern = functools.partial(_dil_lat_kernel, band=band, nsub=nsub, sub_per_seq=sub_per_seq)
    return pl.pallas_call(
        kern, grid=(T // R,),
        in_specs=[pl.BlockSpec((R, 3 * HW), lambda i: (i, 0)),
                  pl.BlockSpec((band, 3 * HW), lambda i: (halo(i), 0))],
        out_specs=[pl.BlockSpec((R, HW), lambda i: (i, 0)), pl.BlockSpec((R, LANES), lambda i: (i, 0))],
        out_shape=[jax.ShapeDtypeStruct((T, HW), BF16), jax.ShapeDtypeStruct((T, LANES), F32)],
        scratch_shapes=[pltpu.VMEM((nsub, DIL_HEADS, 2, band, band), F32),
                        pltpu.VMEM((nsub, DIL_HEADS, 2, band, band), BF16)],
        compiler_params=_params("parallel"), name="dil_lat")(qkv, qkv)


def _dil_merge_kernel(*refs, dils, tmm):
    ng = len(dils)
    o_refs, l_refs = refs[0:2 * ng:2], refs[1:2 * ng:2]
    out_ref, scr = refs[2 * ng], refs[2 * ng + 1]
    nch = DIL_HEADS * HEAD_DIM // LANES
    o_pos, l_pos = [], []
    for g, d in enumerate(dils):
        if d == 1:
            o_pos.append([o_refs[g][:, c * LANES:(c + 1) * LANES].astype(F32) for c in range(nch)])
            l_pos.append(l_refs[g][...])
            continue
        for r in range(d):
            rows = pl.ds(r, tmm // d, stride=d)
            for c in range(nch):
                scr[g, c, rows, :] = o_refs[g][r, :, c * LANES:(c + 1) * LANES].astype(F32)
            scr[g, nch, rows, :] = l_refs[g][r]
        o_pos.append([scr[g, c] for c in range(nch)])
        l_pos.append(scr[g, nch])
    mx = functools.reduce(jnp.maximum, l_pos)
    es = [jnp.exp(l - mx) for l in l_pos]
    den = functools.reduce(lambda a, b: a + b, es)
    hpc = LANES // HEAD_DIM
    for c in range(nch):
        acc = None
        for g in range(ng):
            w = es[g] / den
            wc = jnp.concatenate([jnp.broadcast_to(w[:, c * hpc + k:c * hpc + k + 1], (tmm, HEAD_DIM))
                                  for k in range(hpc)], axis=1)
            term = wc * o_pos[g][c]
            acc = term if acc is None else acc + term
        out_ref[:, c * LANES:(c + 1) * LANES] = acc.astype(out_ref.dtype)


def _dil_merge(os_, ls_, batch, seq, tmm=256):
    dils = tuple(d for _, d in DIL_GROUPS)
    HW = DIL_HEADS * HEAD_DIM
    T = batch * seq
    tps = seq // tmm
    args, in_specs = [], []
    for o, l, d in zip(os_, ls_, dils):
        if d == 1:
            args += [o, l]
            in_specs += [pl.BlockSpec((tmm, HW), lambda i: (i, 0)), pl.BlockSpec((tmm, LANES), lambda i: (i, 0))]
        else:
            M = seq // d
            args += [o.reshape(batch, d, M, HW), l.reshape(batch, d, M, LANES)]
            in_specs += [pl.BlockSpec((None, d, tmm // d, HW), lambda i: (i // tps, 0, i % tps, 0)),
                         pl.BlockSpec((None, d, tmm // d, LANES), lambda i: (i // tps, 0, i % tps, 0))]
    kern = functools.partial(_dil_merge_kernel, dils=dils, tmm=tmm)
    return pl.pallas_call(
        kern, grid=(T // tmm,), in_specs=in_specs,
        out_specs=pl.BlockSpec((tmm, HW), lambda i: (i, 0)),
        out_shape=jax.ShapeDtypeStruct((T, HW), BF16),
        scratch_shapes=[pltpu.VMEM((len(dils), HW // LANES + 1, tmm, LANES), F32)],
        compiler_params=_params("parallel"), name="dil_merge")(*args)


def _dsa_kernel(q_ref, iq_ref, kv_ref, ikw_ref, o_ref, qall_s, iqall_s, kvt_s, key_s, bias_s, j_s, acc_s, p_s,
                *, tq, ck, top, seq):
    qi = pl.program_id(1)
    lo = qi * tq
    nch = (lo + tq + ck - 1) // ck
    krow = lax.broadcasted_iota(I32, (ck, tq), 0)
    qcol = lo + lax.broadcasted_iota(I32, (ck, tq), 1)

    def chunk(c):
        return pl.ds(pl.multiple_of(c * ck, ck), ck)

    @pl.when(qi == 0)
    def _():
        def body(c, _):
            kvt_s[:, chunk(c)] = kv_ref[chunk(c), :].astype(F32).T.astype(BF16)
            return 0
        lax.fori_loop(0, seq // ck, body, 0)

    for h in range(DSA_HEADS):
        qall_s[h * tq:(h + 1) * tq, :] = q_ref[:, h * HEAD_DIM:(h + 1) * HEAD_DIM]
    for h in range(IDX_HEADS):
        iqall_s[h * tq:(h + 1) * tq, :] = iq_ref[:, h * IDX_DIM:(h + 1) * IDX_DIM]

    @pl.when(lo + tq <= top)
    def _():
        def body(c, _):
            bias_s[chunk(c), :] = jnp.where(c * ck + krow <= qcol, 0.0, NEG_INF).astype(F32)
            return 0
        lax.fori_loop(0, nch, body, 0)

    @pl.when(lo + tq > top)
    def _():
        iw_t = ikw_ref[pl.ds(pl.multiple_of(lo, tq), tq), :].T[IDX_DIM:IDX_DIM + IDX_HEADS, :]
        iw_t = iw_t * (IDX_HEADS ** -0.5 * IDX_DIM ** -0.5)

        def score_chunk(c):
            ik = ikw_ref[chunk(c), 0:IDX_DIM].astype(BF16)
            r = _dot_t(ik, iqall_s[...])
            sc = jnp.zeros((ck, tq), F32)
            for h in range(IDX_HEADS):
                sc = sc + iw_t[h:h + 1, :] * jnp.maximum(r[:, h * tq:(h + 1) * tq], 0.0)
            sc = jnp.where(c * ck + krow <= qcol, sc, NEG_INF)
            bits = lax.bitcast_convert_type(sc, I32)
            key_s[chunk(c), :] = bits ^ ((bits >> 31) & 0x7FFFFFFF)

        def score_body(i, _):
            score_chunk(2 * i)
            score_chunk(jnp.minimum(2 * i + 1, nch - 1))
            return 0
        lax.fori_loop(0, (nch + 1) // 2, score_body, 0)

        def count(pred):
            def body(c, acc):
                ind = pred(key_s[chunk(c), :], c)
                return acc + jnp.sum(ind.reshape(ck // 64, 64, tq), axis=0)
            acc = lax.fori_loop(0, nch, body, jnp.zeros((64, tq), F32))
            return jnp.sum(acc, axis=0, keepdims=True)

        def count_ge(cand):
            return count(lambda kc, c: jnp.where(kc >= cand, 1.0, 0.0))

        topf = float(top)
        tau = jnp.where(count_ge(jnp.zeros((1, tq), I32)) >= topf, 0, INT_MIN).astype(I32)

        def search(i, tau):
            cand = tau | jnp.left_shift(jnp.int32(1), 30 - i)
            return jnp.where(count_ge(cand) >= topf, cand, tau)
        tau = lax.fori_loop(0, 31, search, tau)

        need = topf - count_ge(tau + 1)
        j_s[...] = jnp.full(j_s.shape, seq, I32)

        @pl.when(jnp.max(count_ge(tau)) > topf)
        def _():
            def count_eq_lt(J):
                return count(lambda kc, c: jnp.where(kc == tau, jnp.where(c * ck + krow < J, 1.0, 0.0), 0.0))

            nbits = seq.bit_length()

            def tie(i, J):
                cand = J + jnp.left_shift(jnp.int32(1), nbits - 1 - i)
                return jnp.where(count_eq_lt(cand) <= need, cand, J)
            J = lax.fori_loop(0, nbits, tie, jnp.zeros((1, tq), I32))
            j_s[...] = jnp.broadcast_to(J, j_s.shape)

        J = j_s[0:1, :]

        def bias_body(c, _):
            kc = key_s[chunk(c), :]
            sel = jnp.where(kc > tau, 0.0,
                            jnp.where(kc == tau, jnp.where(c * ck + krow < J, 0.0, NEG_INF), NEG_INF))
            bias_s[chunk(c), :] = sel.astype(F32)
            return 0
        lax.fori_loop(0, nch, bias_body, 0)

    acc_s[...] = jnp.zeros_like(acc_s)

    def step(c, carry, slot):
        ms, ls = carry
        kc = kv_ref[chunk(c), 0:HEAD_DIM]
        bias = bias_s[chunk(c), :]
        vt = kvt_s[HEAD_DIM:2 * HEAD_DIM, chunk(c)]
        new_ms, new_ls, alphas = [], [], []
        for h in range(DSA_HEADS):
            if h % 2 == 0:
                s_pair = _dot_t(kc, qall_s[h * tq:(h + 2) * tq, :])
            s = s_pair[:, (h % 2) * tq:(h % 2 + 1) * tq] + bias
            m_new = jnp.maximum(ms[h], jnp.max(s, axis=0, keepdims=True))
            alpha = jnp.exp(ms[h] - m_new)
            p = jnp.exp(s - m_new)
            new_ls.append(alpha * ls[h] + jnp.sum(p, axis=0, keepdims=True))
            new_ms.append(m_new)
            alphas.append(alpha)
            p_s[slot, :, h * tq:(h + 1) * tq] = p.astype(BF16)
        acc_s[...] = acc_s[...] * jnp.concatenate(alphas, axis=1) + _dot(vt, p_s[slot])
        return tuple(new_ms), tuple(new_ls)

    def pair(i, carry):
        return step(2 * i + 1, step(2 * i, carry, 0), 1)

    init = (tuple(jnp.full((1, tq), M_FLOOR, F32) for _ in range(DSA_HEADS)),
            tuple(jnp.zeros((1, tq), F32) for _ in range(DSA_HEADS)))
    carry = lax.fori_loop(0, nch // 2, pair, init)
    _, ls = lax.cond(nch % 2 == 1, lambda cr: step(nch - 1, cr, 0), lambda cr: cr, carry)
    o_all = acc_s[...] / jnp.concatenate(ls, axis=1)
    for h in range(0, DSA_HEADS, 2):
        pair = jnp.concatenate([o_all[:, h * tq:(h + 1) * tq], o_all[:, (h + 1) * tq:(h + 2) * tq]], axis=0)
        o_ref[:, h * HEAD_DIM:(h + 2) * HEAD_DIM] = pair.T.astype(o_ref.dtype)


def _dsa_attn(q, kv, iq, ikw, batch, seq, tq=LANES, ck=256):
    T = q.shape[0]
    nq = seq // tq
    top = min(TOPK_MAX, seq // 4)
    assert tq == LANES and top % tq == 0 and seq % ck == 0
    kern = functools.partial(_dsa_kernel, tq=tq, ck=ck, top=top, seq=seq)
    W = DSA_HEADS * HEAD_DIM
    return pl.pallas_call(
        kern, grid=(batch, nq),
        in_specs=[pl.BlockSpec((tq, W), lambda b, i: (b * nq + i, 0)),
                  pl.BlockSpec((tq, IDX_HEADS * IDX_DIM), lambda b, i: (b * nq + i, 0)),
                  pl.BlockSpec((seq, LANES), lambda b, i: (b, 0)),
                  pl.BlockSpec((seq, LANES), lambda b, i: (b, 0))],
        out_specs=pl.BlockSpec((tq, W), lambda b, i: (b * nq + i, 0)),
        out_shape=jax.ShapeDtypeStruct((T, W), BF16),
        scratch_shapes=[pltpu.VMEM((DSA_HEADS * tq, HEAD_DIM), BF16),
                        pltpu.VMEM((IDX_HEADS * tq, IDX_DIM), BF16),
                        pltpu.VMEM((LANES, seq), BF16),
                        pltpu.VMEM((seq, tq), I32),
                        pltpu.VMEM((seq, tq), F32),
                        pltpu.VMEM((8, tq), I32),
                        pltpu.VMEM((HEAD_DIM, DSA_HEADS * tq), F32),
                        pltpu.VMEM((2, ck, DSA_HEADS * tq), BF16)],
        compiler_params=_params("arbitrary", "arbitrary"), name="dsa_attn")(q, iq, kv, ikw)


def _mix_out_kernel(x_ref, oa_ref, ob_ref, oc_ref, wg_ref, bg_ref, wa_ref, wb_ref, wc_ref, wo_ref,
                    lg_ref, lb_ref, o_ref, *, alpha, d_model):
    x = x_ref[...]
    xb = x.astype(BF16)
    merged = None
    for br, (o_r, w_r) in enumerate(((oa_ref, wa_ref), (ob_ref, wb_ref), (oc_ref, wc_ref))):
        sl = slice(br * d_model, (br + 1) * d_model)
        gate = jax.nn.sigmoid(_dot(xb, wg_ref[:, sl]) + bg_ref[:, sl])
        term = gate * _dot(o_r[...], w_r[...])
        merged = term if merged is None else merged + term
    mix = _dot(merged.astype(BF16), wo_ref[...])
    o_ref[...] = _layer_norm(alpha * x + mix, lg_ref[...], lb_ref[...])


def _mix_out(x, oa, ob, oc, wg, bg, wa, wb, wc, wo, lg, lb, alpha, tm=512):
    T, D = x.shape
    const = lambda i: (0, 0)
    full = lambda a: pl.BlockSpec(a.shape, const)
    row = lambda a: pl.BlockSpec((tm, a.shape[1]), lambda i: (i, 0))
    kern = functools.partial(_mix_out_kernel, alpha=alpha, d_model=D)
    return pl.pallas_call(
        kern, grid=(T // tm,),
        in_specs=[row(x), row(oa), row(ob), row(oc), full(wg), full(bg), full(wa), full(wb), full(wc),
                  full(wo), full(lg), full(lb)],
        out_specs=pl.BlockSpec((tm, D), lambda i: (i, 0)),
        out_shape=jax.ShapeDtypeStruct((T, D), F32),
        compiler_params=_params("parallel"), name="mix_out")(x, oa, ob, oc, wg, bg, wa, wb, wc, wo, lg, lb)


def _route(x, wr_hi_ref, wr_lo_ref, br_ref):
    xh = x.astype(BF16)
    xl = (x - xh.astype(F32)).astype(BF16)
    logits = _dot(xh, wr_hi_ref[...]) + _dot(xl, wr_hi_ref[...]) + _dot(xh, wr_lo_ref[...]) + br_ref[...]
    lane = lax.broadcasted_iota(I32, logits.shape, 1).astype(F32)
    none = float(LANES)
    glog = jnp.where(lane < N_GROUPS, logits, NEG_INF)
    gmax = jnp.max(glog, axis=1, keepdims=True)
    g_p = 1.0 / jnp.sum(jnp.exp(glog - gmax), axis=1, keepdims=True)
    g_idx = jnp.min(jnp.where(glog == gmax, lane, none), axis=1, keepdims=True)
    first = N_GROUPS + g_idx * EXPERTS_PER_GROUP
    sub = jnp.where(lane >= first, jnp.where(lane < first + EXPERTS_PER_GROUP, logits, NEG_INF), NEG_INF)
    v1 = jnp.max(sub, axis=1, keepdims=True)
    i1 = jnp.min(jnp.where(sub == v1, lane, none), axis=1, keepdims=True)
    sub2 = jnp.where(lane == i1, NEG_INF, sub)
    v2 = jnp.max(sub2, axis=1, keepdims=True)
    i2 = jnp.min(jnp.where(sub2 == v2, lane, none), axis=1, keepdims=True)
    e2 = jnp.exp(v2 - v1)
    w1 = g_p / (1.0 + e2)
    w2 = g_p * e2 / (1.0 + e2)
    return jnp.where(lane == i1, w1, jnp.where(lane == i2, w2, 0.0)), g_idx


def _split3(v):
    hi = v.astype(BF16)
    r = v - hi.astype(F32)
    mid = r.astype(BF16)
    return hi, mid, (r - mid.astype(F32)).astype(BF16)


def _moe_kernel(x_ref, wrh_ref, wrl_ref, br_ref, w1_ref, w3_ref, w2_ref, lg_ref, lb_ref, o_ref,
                xs_s, combs_s, pt_s, acc_s, rng_s, *, alpha, tm, rb, ts, eps):
    e = pl.program_id(1)

    @pl.when(e == 0)
    def _():
        x = x_ref[...]
        comb, g_idx = _route(x, wrh_ref, wrl_ref, br_ref)
        lane = lax.broadcasted_iota(I32, (tm, LANES), 1).astype(F32)
        onehot_g = jnp.where(lane == g_idx, 1.0, 0.0)
        ti = lax.broadcasted_iota(I32, (tm, tm), 0)
        tj = lax.broadcasted_iota(I32, (tm, tm), 1)
        earlier = jnp.where(tj < ti, 1.0, 0.0).astype(BF16)
        rank = _dot(earlier, onehot_g.astype(BF16))
        counts = jnp.sum(onehot_g, axis=0, keepdims=True)
        padded = jnp.ceil(counts * (1.0 / ROW_ALIGN)) * ROW_ALIGN
        lane1 = lax.broadcasted_iota(I32, (1, LANES), 1)
        off = jnp.zeros((1, LANES), F32)
        for k in range(1, N_GROUPS):
            off = off + jnp.where(lane1 >= k, pltpu.roll(padded, k, 1), 0.0)
        pos = jnp.sum(onehot_g * (off + rank), axis=1, keepdims=True)
        pos_row = jnp.broadcast_to(pos, (tm, LANES)).T[0:1, :]
        sj = lax.broadcasted_iota(I32, (tm, ts), 1)
        si = lax.broadcasted_iota(I32, (ts, tm), 0)
        pt_s[...] = jnp.where(sj.astype(F32) == pos, 1.0, 0.0).astype(BF16)
        perm = jnp.where(si.astype(F32) == pos_row, 1.0, 0.0).astype(BF16)
        xs_s[...] = _dot(perm, x.astype(BF16)).astype(BF16)
        c_hi, c_mid, c_lo = _split3(comb)
        combs_s[...] = _dot(perm, c_hi) + _dot(perm, c_mid) + _dot(perm, c_lo)
        acc_s[...] = jnp.zeros_like(acc_s)
        for g in range(N_GROUPS):
            start = jnp.sum(jnp.where(lane1 == g, off, 0.0)).astype(I32)
            cnt = jnp.sum(jnp.where(lane1 == g, counts, 0.0)).astype(I32)
            rng_s[g] = start
            rng_s[N_GROUPS + g] = (cnt + rb - 1) // rb

    g = (e * eps) // EXPERTS_PER_GROUP
    lane_b = lax.broadcasted_iota(I32, (rb, LANES), 1)
    start = rng_s[g]

    def block(b, _):
        rows = pl.ds(pl.multiple_of(start + b * rb, ROW_ALIGN), rb)
        xb = xs_s[rows, :]
        comb = combs_s[rows, :]
        y = None
        for k in range(eps):
            we = jnp.sum(jnp.where(lane_b == N_GROUPS + e * eps + k, comb, 0.0), axis=1, keepdims=True)
            a = _dot(xb, w1_ref[k])
            b3 = _dot(xb, w3_ref[k])
            hid = (a * jax.nn.sigmoid(a) * b3).astype(BF16)
            yk = we * _dot(hid, w2_ref[k])
            y = yk if y is None else y + yk
        acc_s[rows, :] += y
        return 0
    lax.fori_loop(0, rng_s[N_GROUPS + g], block, 0)

    @pl.when(e == N_EXPERTS // eps - 1)
    def _():
        y = _dot(pt_s[...], acc_s[...].astype(BF16))
        o_ref[...] = _layer_norm(alpha * x_ref[...] + y, lg_ref[...], lb_ref[...])


def _moe(x, wrh, wrl, br, w1, w3, w2, lg, lb, alpha, tm=1024, rb=320, eps=4):
    T, D = x.shape
    assert EXPERTS_PER_GROUP % eps == 0
    const = lambda i, e: (0, 0)
    full = lambda a: pl.BlockSpec(a.shape, const)
    ts = -(-(tm + N_GROUPS * ROW_ALIGN + rb) // LANES) * LANES
    kern = functools.partial(_moe_kernel, alpha=alpha, tm=tm, rb=rb, ts=ts, eps=eps)
    return pl.pallas_call(
        kern, grid=(T // tm, N_EXPERTS // eps),
        in_specs=[pl.BlockSpec((tm, D), lambda i, e: (i, 0)), full(wrh), full(wrl), full(br),
                  pl.BlockSpec((eps, D, D_EXPERT), lambda i, e: (e, 0, 0)),
                  pl.BlockSpec((eps, D, D_EXPERT), lambda i, e: (e, 0, 0)),
                  pl.BlockSpec((eps, D_EXPERT, D), lambda i, e: (e, 0, 0)),
                  full(lg), full(lb)],
        out_specs=pl.BlockSpec((tm, D), lambda i, e: (i, 0)),
        out_shape=jax.ShapeDtypeStruct((T, D), F32),
        scratch_shapes=[pltpu.VMEM((ts, D), BF16),
                        pltpu.VMEM((ts, LANES), F32),
                        pltpu.VMEM((tm, ts), BF16),
                        pltpu.VMEM((ts, D), F32),
                        pltpu.SMEM((2 * N_GROUPS,), I32)],
        compiler_params=_params("parallel", "arbitrary"), name="moe")(x, wrh, wrl, br, w1, w3, w2, lg, lb)


def _rope_tables(seq):
    pos = jnp.arange(seq, dtype=F32)[:, None]
    one = lambda n: jnp.ones((seq, n), F32)
    zero = lambda n: jnp.zeros((seq, n), F32)

    inv_p = ROPE_THETA ** (-jnp.arange(0, ROT_DIM, 2, dtype=F32) / ROT_DIM)
    cp, sp = jnp.cos(pos * inv_p), jnp.sin(pos * inv_p)
    hp = ROT_DIM // 2
    rest = HEAD_DIM - ROT_DIM
    head = (jnp.concatenate([cp, cp, one(rest)], 1),
            jnp.concatenate([zero(hp), sp, zero(rest)], 1),
            jnp.concatenate([-sp, zero(hp), zero(rest)], 1))
    ident = (one(HEAD_DIM), zero(HEAD_DIM), zero(HEAD_DIM))
    both = jnp.stack([jnp.concatenate([a, a], 1) for a in head])
    first = jnp.stack([jnp.concatenate([a, b], 1) for a, b in zip(head, ident)])
    tab_p = jnp.stack([both, first])

    inv_m = ROPE_THETA ** (-jnp.arange(0, MLA_ROPE, 2, dtype=F32) / MLA_ROPE)
    cm, sm = jnp.cos(pos * inv_m), jnp.sin(pos * inv_m)
    hm = MLA_ROPE // 2
    pad = LANES - MLA_NOPE - MLA_ROPE
    tab_m = jnp.stack([jnp.concatenate([one(MLA_NOPE), cm, cm, one(pad)], 1),
                       jnp.concatenate([zero(MLA_NOPE), zero(hm), sm, zero(pad)], 1),
                       jnp.concatenate([zero(MLA_NOPE), -sm, zero(hm), zero(pad)], 1)])[None]
    return tab_p, tab_m


def _layer_weights(p, l):
    w_in = p["w_in"][l]
    D = w_in.shape[0]
    z = lambda n: jnp.zeros((D, n), F32)
    o = 0
    cuts = []
    for n in (MLA_Q_RANK, MLA_KV_RANK, MLA_ROPE, 3 * 3 * DIL_HEADS * HEAD_DIM, DSA_HEADS * HEAD_DIM,
              HEAD_DIM, HEAD_DIM, IDX_HEADS * IDX_DIM, IDX_DIM, IDX_HEADS):
        cuts.append(w_in[:, o:o + n])
        o += n
    w_cq, w_ckv, w_kr, w_dil, w_q, w_k, w_v, w_iq, w_ik, w_iw = cuts
    qs = HEAD_DIM ** -0.5
    w_mla = jnp.concatenate([w_cq, w_ckv, z(MLA_NOPE), w_kr, z(LANES - MLA_NOPE - MLA_ROPE)], 1)
    HW = DIL_HEADS * HEAD_DIM
    w_dil = w_dil.reshape(D, len(DIL_GROUPS), 3, HW)
    w_dqkv = jnp.concatenate([w_dil[:, :, 0] * qs, w_dil[:, :, 1], w_dil[:, :, 2]], axis=-1)
    w_dqkv = w_dqkv.transpose(1, 0, 2)
    w_dsa = jnp.concatenate([w_q * qs, w_k, w_v, w_iq, w_ik, w_iw, z(LANES - IDX_DIM - IDX_HEADS)], 1)

    def per_head(w, n_in, n_keep_lo, n_keep_hi):
        r = w.shape[0]
        w = w.reshape(r, MLA_HEADS, n_in)[:, :, n_keep_lo:n_keep_hi]
        w = jnp.pad(w, ((0, 0), (0, 0), (0, LANES - (n_keep_hi - n_keep_lo))))
        return w.reshape(r, MLA_HEADS * LANES)

    w_uq = per_head(p["w_uq"][l], MLA_NOPE + MLA_ROPE, 0, MLA_NOPE + MLA_ROPE)
    w_uq3 = w_uq.reshape(MLA_Q_RANK, MLA_HEADS, LANES)
    w_uq = jnp.concatenate([jnp.roll(w_uq3, sh, axis=2).reshape(MLA_Q_RANK, -1)
                            for sh in (0, MLA_ROPE // 2, -(MLA_ROPE // 2))], axis=1)
    w_uk = per_head(p["w_ukv"][l], MLA_NOPE + MLA_V, 0, MLA_NOPE)
    w_uv = p["w_ukv"][l].reshape(MLA_KV_RANK, MLA_HEADS, MLA_NOPE + MLA_V)[:, :, MLA_NOPE:]
    w_uv = w_uv.reshape(MLA_KV_RANK, MLA_HEADS * MLA_V).T

    w_r = jnp.concatenate([p["w_group"][l], p["w_sub"][l], z(LANES - N_GROUPS - N_EXPERTS)], 1)
    w_r_hi = w_r.astype(BF16)
    w_r_lo = (w_r - w_r_hi.astype(F32)).astype(BF16)
    b_r = jnp.concatenate([p["b_group"][l], p["b_sub"][l], jnp.zeros((LANES - N_GROUPS - N_EXPERTS,), F32)])

    bf = lambda a: a.astype(BF16)
    return dict(
        w_proj=bf(jnp.concatenate([w_mla, w_dsa] + [w_dqkv[g] for g in range(len(DIL_GROUPS))], axis=1)),
        n_mla=w_mla.shape[1],
        q_g=p["q_norm_g"][l][None], kv_g=p["kv_norm_g"][l][None],
        w_uq=bf(w_uq), w_uk=bf(w_uk), w_uv=bf(w_uv),
        w_gate=bf(p["w_gate"][l]), b_gate=p["b_gate"][l][None],
        w_a=bf(p["w_a"][l]), w_b=bf(p["w_b"][l]), w_c=bf(p["w_c"][l]), w_o=bf(p["w_o"][l]),
        ln1_g=p["ln1_g"][l][None], ln1_b=p["ln1_b"][l][None],
        w_r_hi=w_r_hi, w_r_lo=w_r_lo, b_r=b_r[None],
        w1=bf(p["w1"][l]), w3=bf(p["w3"][l]), w2=bf(p["w2"][l]),
        ln2_g=p["ln2_g"][l][None], ln2_b=p["ln2_b"][l][None])


def _project(xt, w, tab_p, batch, seq):
    HW = DIL_HEADS * HEAD_DIM
    qw, iw_ = DSA_HEADS * HEAD_DIM, IDX_HEADS * IDX_DIM
    widths = [w["n_mla"], qw, LANES, iw_, LANES] + [3 * HW] * len(DIL_GROUPS)
    starts = [sum(widths[:k]) for k in range(len(widths))]
    rope2, rope1, none = 0, 1, -1
    kinds = ((none,) * (w["n_mla"] // LANES)
             + (rope2,) * (qw // LANES) + (rope1,) + (rope2,) * (iw_ // LANES) + (rope1,)
             + ((rope2,) * (2 * HW // LANES) + (none,) * (HW // LANES)) * len(DIL_GROUPS))
    return _proj(xt, w["w_proj"], tab_p, batch, seq, segs=tuple(zip(starts, widths)),
                 dtypes=(F32, BF16, BF16, BF16, F32) + (BF16,) * len(DIL_GROUPS), kinds=kinds,
                 dils=(1,) * 5 + tuple(d for _, d in DIL_GROUPS))


def _layer(xt, w, tab_p, tab_m, batch, seq, alpha):
    grp, q_c, kv_c, iq_c, ikw_c, *dil_qkv = _project(xt, w, tab_p, batch, seq)
    dil_o, dil_lse = [], []
    for qkv_g, (window, d) in zip(dil_qkv, DIL_GROUPS):
        o_g, lse_g = _dil_lat_attn(qkv_g, seq, d, band=window // d)
        dil_o.append(o_g)
        dil_lse.append(lse_g)

    q_a, k_a, v_a = _mla_up(grp, w["q_g"], w["kv_g"], w["w_uq"], w["w_uk"], w["w_uv"], tab_m, seq)
    o_a = _mla_attn(q_a, k_a, v_a, batch, seq)
    o_b = _dil_merge(dil_o, dil_lse, batch, seq)
    o_c = _dsa_attn(q_c, kv_c, iq_c, ikw_c, batch, seq)

    x1 = _mix_out(xt, o_a, o_b, o_c, w["w_gate"], w["b_gate"], w["w_a"], w["w_b"], w["w_c"], w["w_o"],
                  w["ln1_g"], w["ln1_b"], alpha)
    return _moe(x1, w["w_r_hi"], w["w_r_lo"], w["b_r"], w["w1"], w["w3"], w["w2"],
                w["ln2_g"], w["ln2_b"], alpha)


def kernel(x, w_in, q_norm_g, w_uq, kv_norm_g, w_ukv, w_gate, b_gate, w_a, w_b, w_c, w_o, ln1_g, ln1_b,
           w_group, b_group, w_sub, b_sub, w1, w3, w2, ln2_g, ln2_b):
    batch, seq, d_model = x.shape
    depth = w_in.shape[0]
    alpha = (2 * depth) ** 0.25
    p = dict(w_in=w_in, q_norm_g=q_norm_g, w_uq=w_uq, kv_norm_g=kv_norm_g, w_ukv=w_ukv, w_gate=w_gate,
             b_gate=b_gate, w_a=w_a, w_b=w_b, w_c=w_c, w_o=w_o, ln1_g=ln1_g, ln1_b=ln1_b, w_group=w_group,
             b_group=b_group, w_sub=w_sub, b_sub=b_sub, w1=w1, w3=w3, w2=w2, ln2_g=ln2_g, ln2_b=ln2_b)
    tab_p, tab_m = _rope_tables(seq)
    xt = x.reshape(batch * seq, d_model)
    for l in range(depth):
        xt = _layer(xt, _layer_weights(p, l), tab_p, tab_m, batch, seq, alpha)
    return xt.reshape(batch, seq, d_model)
```
